```python
import math, functools
import jax, jax.numpy as jnp
from jax import lax
import numpy as np

D_MODEL = 1024
BATCH = 4
SEQ = 8192
DEPTH = 2
DEC_BATCH = 32
DEC_SEQ = 1
PAST_LEN = 16384
PAGE_SIZE = 128

HEAD_DIM = 64
A_HEADS = D_MODEL // 2 // HEAD_DIM
A_WIDTH = A_HEADS * HEAD_DIM
IDX_HEADS = 16
IDX_DIM = 64
IDX_ROPE = 32
TOPK_MAX = 256
B_HEADS = 4
B_DK = 64
B_DV = 64
B_WIDTH = B_HEADS * B_DV
B_CONV = 2 * B_HEADS * B_DK + B_HEADS * B_DV
CONV_W = 4
C_HEADS = 4
C_DK = 64
C_DV = 64
C_WIDTH = C_HEADS * C_DV
MIX_WIDTH = A_WIDTH + B_WIDTH + C_WIDTH
D_FF = 4 * D_MODEL
ROPE_THETA = 10000.0
EPS = 1e-6
Q_BLOCK = 128
CHUNK = 64
IN_SPLITS = (A_WIDTH, A_WIDTH, A_WIDTH,
             IDX_HEADS * IDX_DIM, IDX_HEADS, IDX_DIM,
             B_HEADS * B_DK, B_HEADS * B_DK, B_HEADS * B_DV,
             B_HEADS, B_HEADS, B_WIDTH,
             C_HEADS * C_DK, C_HEADS * C_DK, C_WIDTH)
N_IN = sum(IN_SPLITS)

kernel_name = "hybrid_dsa_gdn_hgrn2_step"


def split_cols(z):
    offs = [int(o) for o in np.cumsum(IN_SPLITS)[:-1]]
    return jnp.split(z, offs, axis=-1)


def rmsnorm(x, g):
    xf = x.astype(jnp.float32)
    y = xf * lax.rsqrt(jnp.mean(xf * xf, -1, keepdims=True) + EPS)
    return (y * g.astype(jnp.float32)).astype(x.dtype)


def layernorm(x, g, b):
    xf = x.astype(jnp.float32)
    mu = jnp.mean(xf, -1, keepdims=True)
    xc = xf - mu
    y = xc * lax.rsqrt(jnp.mean(xc * xc, -1, keepdims=True) + EPS)
    return (y * g.astype(jnp.float32) + b.astype(jnp.float32)).astype(x.dtype)


def l2norm(x):
    return x * lax.rsqrt(jnp.sum(x * x, -1, keepdims=True) + EPS)


def rope(x, pos):
    half = x.shape[-1] // 2
    inv = ROPE_THETA ** (-jnp.arange(half, dtype=jnp.float32) / half)
    ang = pos.astype(jnp.float32)[:, None] * inv[None, :]
    shape = (1, pos.shape[0]) + (1,) * (x.ndim - 3) + (half,)
    cos, sin = jnp.cos(ang).reshape(shape), jnp.sin(ang).reshape(shape)
    xf = x.astype(jnp.float32)
    x1, x2 = xf[..., :half], xf[..., half:]
    return jnp.concatenate([x1 * cos - x2 * sin, x1 * sin + x2 * cos], -1).astype(x.dtype)


def index_scores(qi, wi, ki):
    s = jnp.einsum('bthd,bsd->bths', qi, ki, preferred_element_type=jnp.float32)
    return jnp.einsum('bths,bth->bts', jax.nn.relu(s), wi.astype(jnp.float32))


def sparse_attend(q, ksel, vsel, valid):
    s = jnp.einsum('bthd,btkhd->bthk', q, ksel, preferred_element_type=jnp.float32) * (HEAD_DIM ** -0.5)
    s = jnp.where(valid[:, :, None, :], s, -jnp.inf)
    p = jax.nn.softmax(s, axis=-1)
    return jnp.einsum('bthk,btkhd->bthd', p.astype(vsel.dtype), vsel)


def attn_prompt(q, k, v, qi, wi, ki):
    B, S = q.shape[:2]
    topk = min(TOPK_MAX, S // 4)
    nblk = S // Q_BLOCK
    kpos = jnp.arange(S)
    bi = jnp.arange(B)[:, None, None]

    def block(n):
        t0 = n * Q_BLOCK
        qb = lax.dynamic_slice_in_dim(q, t0, Q_BLOCK, axis=1)
        qib = lax.dynamic_slice_in_dim(qi, t0, Q_BLOCK, axis=1)
        wib = lax.dynamic_slice_in_dim(wi, t0, Q_BLOCK, axis=1)
        tpos = t0 + jnp.arange(Q_BLOCK)
        score = index_scores(qib, wib, ki)
        score = jnp.where(kpos[None, None, :] <= tpos[None, :, None], score, -jnp.inf)
        _, idx = lax.top_k(score, topk)
        valid = idx <= tpos[None, :, None]
        return sparse_attend(qb, k[bi, idx], v[bi, idx], valid)

    out = lax.map(block, jnp.arange(nblk))
    return jnp.moveaxis(out, 0, 1).reshape(B, S, A_HEADS, HEAD_DIM)


def attn_sample(q, k, v, qi, wi, ki, cache_k, cache_v, cache_ik, page_table, layer):
    Bd, T = q.shape[:2]
    past = page_table.shape[1] * PAGE_SIZE
    L = past + T
    topk = min(TOPK_MAX, L // 4)
    bi = jnp.arange(Bd)[:, None, None]
    past_ik = cache_ik[layer, page_table].reshape(Bd, past, IDX_DIM).astype(ki.dtype)
    ki_all = jnp.concatenate([past_ik, ki], axis=1)
    tpos = past + jnp.arange(T)
    score = index_scores(qi, wi, ki_all)
    score = jnp.where(jnp.arange(L)[None, None, :] <= tpos[None, :, None], score, -jnp.inf)
    _, idx = lax.top_k(score, topk)
    valid = idx <= tpos[None, :, None]
    in_past = (idx < past)[..., None, None]
    pidx = jnp.minimum(idx, past - 1)
    phys = page_table[bi, pidx // PAGE_SIZE]
    off = pidx % PAGE_SIZE
    nidx = jnp.clip(idx - past, 0, T - 1)
    ksel = jnp.where(in_past, cache_k[layer, phys, off].astype(k.dtype), k[bi, nidx])
    vsel = jnp.where(in_past, cache_v[layer, phys, off].astype(v.dtype), v[bi, nidx])
    return sparse_attend(q, ksel, vsel, valid)


def short_conv(xc, buf, w):
    xp = jnp.concatenate([buf.astype(xc.dtype), xc], axis=1)
    y = lax.conv_general_dilated(xp, w[:, None, :].astype(xc.dtype), window_strides=(1,), padding='VALID',
                                 dimension_numbers=('NWC', 'WIO', 'NWC'), feature_group_count=xc.shape[-1])
    return jax.nn.silu(y), xp[:, -(CONV_W - 1):]


def _to_chunks(a, C, N, pad):
    B = a.shape[0]
    a = jnp.pad(a, [(0, 0), (0, pad)] + [(0, 0)] * (a.ndim - 2))
    a = a.reshape((B, N, C) + a.shape[2:])
    return jnp.moveaxis(a, (1, 3), (0, 2))


def _from_chunks(o, T):
    o = jnp.moveaxis(o, (0, 2), (1, 3))
    B, N, C = o.shape[:3]
    return o.reshape((B, N * C) + o.shape[3:])[:, :T]


def gated_delta_rule(q, k, v, beta, g, S0):
    T = q.shape[1]
    DV = v.shape[-1]
    C = min(CHUNK, T)
    N = -(-T // C)
    pad = N * C - T
    qc, kc, vc, bc, gc = (_to_chunks(a, C, N, pad) for a in (q, k, v, beta, g))
    gcum = jnp.cumsum(gc, axis=-1)
    diff = gcum[..., :, None] - gcum[..., None, :]
    strict = jnp.tril(jnp.ones((C, C), bool), -1)
    incl = jnp.tril(jnp.ones((C, C), bool))
    M = jnp.where(strict, bc[..., :, None] * jnp.einsum('nbhcd,nbhsd->nbhcs', kc, kc)
                  * jnp.exp(jnp.where(strict, diff, 0.0)), 0.0)
    rhs = jnp.concatenate([bc[..., None] * vc, (bc * jnp.exp(gcum))[..., None] * kc], axis=-1)
    sol = lax.linalg.triangular_solve(M + jnp.eye(C, dtype=M.dtype), rhs, left_side=True, lower=True,
                                      unit_diagonal=True)
    U, W = sol[..., :DV], sol[..., DV:]
    P = jnp.where(incl, jnp.einsum('nbhcd,nbhsd->nbhcs', qc, kc) * jnp.exp(jnp.where(incl, diff, 0.0)), 0.0)
    Qg = qc * jnp.exp(gcum)[..., None]
    glast = gcum[..., -1:]
    Kd = kc * jnp.exp(glast - glast * 0.0 - gcum)[..., None] if False else kc * jnp.exp(glast - gcum)[..., None]
    gl = jnp.exp(glast)[..., None]

    def step(S, xs):
        U_n, W_n, P_n, Qg_n, Kd_n, gl_n = xs
        delta = U_n - jnp.einsum('bhck,bhkv->bhcv', W_n, S)
        o = jnp.einsum('bhck,bhkv->bhcv', Qg_n, S) + jnp.einsum('bhcs,bhsv->bhcv', P_n, delta)
        S = gl_n * S + jnp.einsum('bhck,bhcv->bhkv', Kd_n, delta)
        return S, o

    S, o = lax.scan(step, S0, (U, W, P, Qg, Kd, gl))
    return _from_chunks(o, T), S


def hgrn2_recurrence(q, k, v, logf, S0):
    T = q.shape[1]
    C = min(CHUNK, T)
    N = -(-T // C)
    pad = N * C - T
    qc, kc, vc, lc = (_to_chunks(a, C, N, pad) for a in (q, k, v, logf))
    incl = jnp.tril(jnp.ones((C, C), bool))

    def step(S, xs):
        qn, kn, vn, lfn = xs
        b = jnp.cumsum(lfn, axis=2)
        diff = b[:, :, :, None, :] - b[:, :, None, :, :]
        dec = jnp.exp(jnp.where(incl[:, :, None], diff, -jnp.inf))
        A = jnp.einsum('bhtd,bhsd,bhtsd->bhts', qn, kn, dec)
        o = jnp.einsum('bhtd,bhdv->bhtv', qn * jnp.exp(b), S) + jnp.einsum('bhts,bhsv->bhtv', A, vn)
        blast = b[:, :, -1:, :]
        S = jnp.exp(blast[:, :, 0, :, None]) * S + jnp.einsum('bhsd,bhsv->bhdv', kn * jnp.exp(blast - b), vn)
        return S, o

    S, o = lax.scan(step, S0, (qc, kc, vc, lc))
    return _from_chunks(o, T), S


def layer_forward(x, pos, lw, attn_fn, d_S0, d_buf, c_S0):
    B, T = x.shape[:2]
    f32 = jnp.float32
    h = rmsnorm(x, lw['norm1'])
    z = jnp.einsum('btd,dn->btn', h, lw['w_in'])
    aq, ak, av, iq, iw, ik, bq, bk, bv, ba, bb, bg, cq, cf, ci = split_cols(z)
    aq = rope(rmsnorm(aq.reshape(B, T, A_HEADS, HEAD_DIM), lw['q_norm']), pos)
    ak = rope(rmsnorm(ak.reshape(B, T, A_HEADS, HEAD_DIM), lw['k_norm']), pos)
    av = av.reshape(B, T, A_HEADS, HEAD_DIM)
    iq = iq.reshape(B, T, IDX_HEADS, IDX_DIM)
    iq = jnp.concatenate([rope(iq[..., :IDX_ROPE], pos), iq[..., IDX_ROPE:]], axis=-1)
    ik = layernorm(ik, lw['ik_g'], lw['ik_b'])
    ik = jnp.concatenate([rope(ik[..., :IDX_ROPE], pos), ik[..., IDX_ROPE:]], axis=-1)
    iw = iw * (IDX_HEADS ** -0.5 * IDX_DIM ** -0.5)
    oa = attn_fn(aq, ak, av, iq, iw, ik)
    qkv, new_buf = short_conv(jnp.concatenate([bq, bk, bv], axis=-1), d_buf, lw['conv_w'])
    qkv = qkv.astype(f32)
    dq = l2norm(qkv[..., :B_HEADS * B_DK].reshape(B, T, B_HEADS, B_DK)) * (B_DK ** -0.5)
    dk = l2norm(qkv[..., B_HEADS * B_DK:2 * B_HEADS * B_DK].reshape(B, T, B_HEADS, B_DK))
    dv = qkv[..., 2 * B_HEADS * B_DK:].reshape(B, T, B_HEADS, B_DV)
    beta = jax.nn.sigmoid(bb.astype(f32))
    g = -jnp.exp(lw['a_log'].astype(f32)) * jax.nn.softplus(ba.astype(f32) + lw['dt_bias'].astype(f32))
    ob, d_S = gated_delta_rule(dq, dk, dv, beta, g, d_S0.astype(f32))
    ob = rmsnorm(ob, lw['d_norm']) * jax.nn.silu(bg.astype(f32).reshape(B, T, B_HEADS, B_DV))
    lb = lw['lb'].reshape(C_HEADS, C_DK)
    cf = cf.astype(f32).reshape(B, T, C_HEADS, C_DK)
    f = lb + (1.0 - lb) * jax.nn.sigmoid(cf)
    ck = (1.0 - lb) * jax.nn.sigmoid(-cf)
    cq = jax.nn.silu(cq.astype(f32)).reshape(B, T, C_HEADS, C_DK)
    cv = ci.astype(f32).reshape(B, T, C_HEADS, C_DV)
    oc, c_S = hgrn2_recurrence(cq, ck, cv, jnp.log(f), c_S0.astype(f32))
    oc = rmsnorm(oc, lw['c_norm'])
    mix = jnp.concatenate([oa.reshape(B, T, A_WIDTH).astype(x.dtype), ob.reshape(B, T, B_WIDTH).astype(x.dtype),
                           oc.reshape(B, T, C_WIDTH).astype(x.dtype)], axis=-1)
    x = x + jnp.einsum('btm,md->btd', mix, lw['w_out'])
    h2 = rmsnorm(x, lw['norm2'])
    u = jax.nn.relu(jnp.einsum('btd,df->btf', h2, lw['w1']))
    x = x + jnp.einsum('btf,fd->btd', u * u, lw['w2'])
    return x, (ak, av, ik, d_S.astype(x.dtype), new_buf.astype(x.dtype), c_S.astype(x.dtype))


def setup_inputs(seed: int = 0) -> dict:
    key = jax.random.key(seed)
    ks = jax.random.split(key, 32)
    n_pages = PAST_LEN // PAGE_SIZE
    n_pool = (5 * DEC_BATCH * n_pages) // 4

    def nrm(k, shape, s=1.0):
        return s * jax.random.normal(k, shape, jnp.float32)

    dt = jax.random.uniform(ks[20], (DEPTH, B_HEADS), jnp.float32, 0.001, 0.1)
    return {
        "x_prompt": nrm(ks[0], (BATCH, SEQ, D_MODEL)),
        "x_sample": nrm(ks[1], (DEC_BATCH, DEC_SEQ, D_MODEL)),
        "cache_k": nrm(ks[2], (DEPTH, n_pool, PAGE_SIZE, A_HEADS, HEAD_DIM)),
        "cache_v": nrm(ks[3], (DEPTH, n_pool, PAGE_SIZE, A_HEADS, HEAD_DIM)),
        "cache_idx_k": nrm(ks[4], (DEPTH, n_pool, PAGE_SIZE, IDX_DIM)),
        "state_delta": nrm(ks[5], (DEPTH, DEC_BATCH, B_HEADS, B_DK, B_DV), 0.1),
        "state_delta_conv": nrm(ks[6], (DEPTH, DEC_BATCH, CONV_W - 1, B_CONV)),
        "state_hgrn": nrm(ks[7], (DEPTH, DEC_BATCH, C_HEADS, C_DK, C_DV), 0.3),
        "page_table": jax.random.permutation(ks[8], n_pool)[:DEC_BATCH * n_pages]
                      .reshape(DEC_BATCH, n_pages).astype(jnp.int32),
        "norm1_g": 1.0 + nrm(ks[9], (DEPTH, D_MODEL), 0.01),
        "w_in": nrm(ks[10], (DEPTH, D_MODEL, N_IN), D_MODEL ** -0.5),
        "q_norm_g": 1.0 + nrm(ks[11], (DEPTH, HEAD_DIM), 0.01),
        "k_norm_g": 1.0 + nrm(ks[12], (DEPTH, HEAD_DIM), 0.01),
        "idx_k_ln_g": 1.0 + nrm(ks[13], (DEPTH, IDX_DIM), 0.01),
        "idx_k_ln_b": nrm(ks[14], (DEPTH, IDX_DIM), 0.01),
        "conv_w": nrm(ks[15], (DEPTH, CONV_W, B_CONV), CONV_W ** -0.5),
        "a_log": jnp.log(jax.random.uniform(ks[16], (DEPTH, B_HEADS), jnp.float32, 1.0, 16.0)),
        "dt_bias": dt + jnp.log(-jnp.expm1(-dt)),
        "delta_norm_g": 1.0 + nrm(ks[17], (DEPTH, B_DV), 0.01),
        "lb_param": nrm(ks[18], (DEPTH, C_HEADS * C_DK), 0.5),
        "hgrn_norm_g": 1.0 + nrm(ks[19], (DEPTH, C_DV), 0.01),
        "w_out": nrm(ks[21], (DEPTH, MIX_WIDTH, D_MODEL), MIX_WIDTH ** -0.5),
        "norm2_g": 1.0 + nrm(ks[22], (DEPTH, D_MODEL), 0.01),
        "w_mlp_in": nrm(ks[23], (DEPTH, D_MODEL, D_FF), D_MODEL ** -0.5),
        "w_mlp_out": nrm(ks[24], (DEPTH, D_FF, D_MODEL), D_FF ** -0.5),
    }


def reference(x_prompt, x_sample, cache_k, cache_v, cache_idx_k, state_delta, state_delta_conv, state_hgrn,
              page_table, norm1_g, w_in, q_norm_g, k_norm_g, idx_k_ln_g, idx_k_ln_b, conv_w, a_log, dt_bias,
              delta_norm_g, lb_param, hgrn_norm_g, w_out, norm2_g, w_mlp_in, w_mlp_out):
    lb_all = jnp.cumsum(jax.nn.softmax(lb_param.astype(jnp.float32), axis=0), axis=0)
    lb_all = lb_all - lb_all[0:1]
    Bp, S = x_prompt.shape[:2]
    Bd, T = x_sample.shape[:2]
    past = page_table.shape[1] * PAGE_SIZE
    pos_p = jnp.arange(S)
    pos_s = past + jnp.arange(T)
    xp, xs = x_prompt, x_sample
    st_p, st_s = [], []
    for l in range(DEPTH):
        lw = {'norm1': norm1_g[l], 'w_in': w_in[l], 'q_norm': q_norm_g[l], 'k_norm': k_norm_g[l],
              'ik_g': idx_k_ln_g[l], 'ik_b': idx_k_ln_b[l], 'conv_w': conv_w[l], 'a_log': a_log[l],
              'dt_bias': dt_bias[l], 'd_norm': delta_norm_g[l], 'lb': lb_all[l], 'c_norm': hgrn_norm_g[l],
              'w_out': w_out[l], 'norm2': norm2_g[l], 'w1': w_mlp_in[l], 'w2': w_mlp_out[l]}
        xp, sp = layer_forward(xp, pos_p, lw, attn_prompt,
                               jnp.zeros((Bp, B_HEADS, B_DK, B_DV), xp.dtype),
                               jnp.zeros((Bp, CONV_W - 1, B_CONV), xp.dtype),
                               jnp.zeros((Bp, C_HEADS, C_DK, C_DV), xp.dtype))
        attn_s = functools.partial(attn_sample, cache_k=cache_k, cache_v=cache_v, cache_ik=cache_idx_k,
                                   page_table=page_table, layer=l)
        xs, ss = layer_forward(xs, pos_s, lw, attn_s, state_delta[l], state_delta_conv[l], state_hgrn[l])
        st_p.append(sp)
        st_s.append(ss)
    k_p = jnp.stack([s[0] for s in st_p])
    v_p = jnp.stack([s[1] for s in st_p])
    ik_p = jnp.stack([s[2] for s in st_p])
    ds_p = jnp.stack([s[3] for s in st_p])
    dc_p = jnp.stack([s[4] for s in st_p])
    hs_p = jnp.stack([s[5] for s in st_p])
    k_s = jnp.stack([s[0] for s in st_s])
    v_s = jnp.stack([s[1] for s in st_s])
    ik_s = jnp.stack([s[2] for s in st_s])
    ds_s = jnp.stack([s[3] for s in st_s])
    dc_s = jnp.stack([s[4] for s in st_s])
    hs_s = jnp.stack([s[5] for s in st_s])
    return (xp, xs, k_p, v_p, ik_p, ds_p, dc_p, hs_p, k_s, v_s, ik_s, ds_s, dc_s, hs_s)
```

```python
import functools

import numpy as np
import jax
import jax.numpy as jnp
from jax import lax
from jax.experimental import pallas as pl
from jax.experimental.pallas import tpu as pltpu

F32 = jnp.float32
BF16 = jnp.bfloat16
I32 = jnp.int32
HIGHEST = lax.Precision.HIGHEST

D_MODEL = 1024
HEAD_DIM = 64
A_HEADS = 8
A_WIDTH = A_HEADS * HEAD_DIM
IDX_HEADS = 16
IDX_DIM = 64
IDX_ROPE = 32
TOPK_MAX = 256
B_HEADS = 4
B_DK = 64
B_CONV = 768
CONV_W = 4
C_HEADS = 4
D_FF = 4 * D_MODEL
ROPE_THETA = 10000.0
EPS = 1e-6
PAGE_SIZE = 128
CHUNK = 64

LANES = 128
VMEM_LIMIT = 48 * 1024 * 1024

Z_IQ = 3 * A_WIDTH
Z_BG = Z_IQ + IDX_HEADS * IDX_DIM
Z_MISC = Z_BG + 256
Z_CONV = 3072
Z_HG = Z_CONV + B_CONV
NZ = Z_HG + 768
M_IW = 64
M_BA = 80
M_BB = 84

NEG = -1e30
INT_MIN = -2 ** 31


def _cparams(sem):
    return pltpu.CompilerParams(dimension_semantics=sem, vmem_limit_bytes=VMEM_LIMIT)


def _dot(a, b, precision=None):
    return jnp.dot(a, b, preferred_element_type=F32, precision=precision)


def _dot_nt(a, b, precision=None):
    return lax.dot_general(a, b, (((1,), (1,)), ((), ())), preferred_element_type=F32, precision=precision)


def _dot_tn(a, b, precision=None):
    return lax.dot_general(a, b, (((0,), (0,)), ((), ())), preferred_element_type=F32, precision=precision)


def _group_sum(y, bd):
    hi = y.astype(BF16)
    lo = (y - hi.astype(F32)).astype(BF16)
    return _dot(hi, bd) + _dot(lo, bd)


def _to_key(x):
    bits = lax.bitcast_convert_type(x, I32)
    return jnp.where(bits < 0, bits ^ jnp.int32(0x7FFFFFFF), bits)


def _silu(x):
    return x * jax.nn.sigmoid(x)


def _softplus(x):
    return jnp.maximum(x, 0.0) + jnp.log(1.0 + jnp.exp(-jnp.abs(x)))


def _norm_matmul_kernel(x_ref, g_ref, w_ref, o_ref, h_scr):
    @pl.when(pl.program_id(1) == 0)
    def _():
        x = x_ref[...]
        ms = jnp.mean(x * x, axis=-1, keepdims=True)
        h_scr[...] = (x * lax.rsqrt(ms + EPS) * g_ref[...]).astype(BF16)

    o_ref[...] = _dot(h_scr[...], w_ref[...])


def _norm_matmul(x, g, w, tn):
    m, d = x.shape
    n = w.shape[1]
    tm = min(512, m)
    return pl.pallas_call(
        _norm_matmul_kernel,
        grid=(m // tm, n // tn),
        in_specs=[pl.BlockSpec((tm, d), lambda i, j: (i, 0)),
                  pl.BlockSpec((1, d), lambda i, j: (0, 0)),
                  pl.BlockSpec((d, tn), lambda i, j: (0, j))],
        out_specs=pl.BlockSpec((tm, tn), lambda i, j: (i, j)),
        out_shape=jax.ShapeDtypeStruct((m, n), F32),
        scratch_shapes=[pltpu.VMEM((tm, d), BF16)],
        compiler_params=_cparams(("parallel", "arbitrary")),
        name="norm_matmul",
    )(x, g, w)


def _prep_kernel(za_ref, zm_ref, ca_ref, sa_ref, ci_ref, si_ref, qg_ref, kg_ref, ikg_ref, ikb_ref, bd_ref,
                 qpad_ref, kf_ref, kb_ref, vf_ref, vb_ref, iqpad_ref, ikf_ref, ikdup_ref, misc_ref):
    tm = za_ref.shape[0]
    bd = bd_ref[...]
    ca = jnp.concatenate([ca_ref[...]] * 4, axis=1)
    sa = jnp.concatenate([sa_ref[...]] * 4, axis=1)
    lane_a = lax.broadcasted_iota(I32, (tm, A_WIDTH), 1)
    lo_half = (lane_a % HEAD_DIM) < (HEAD_DIM // 2)
    lane128 = lax.broadcasted_iota(I32, (tm, LANES), 1)
    left = lane128 < 64

    def norm_rope(x, g):
        ss = _group_sum(x * x, bd)
        y = x * lax.rsqrt(ss * (1.0 / HEAD_DIM) + EPS) * g
        sw = jnp.where(lo_half, pltpu.roll(y, A_WIDTH - 32, 1), pltpu.roll(y, 32, 1))
        return y * ca + sw * sa

    q = norm_rope(za_ref[:, 0:A_WIDTH], qg_ref[...]) * (HEAD_DIM ** -0.5)
    for h in range(A_HEADS):
        blk = q[:, (h // 2) * LANES:(h // 2 + 1) * LANES]
        keep = left if h % 2 == 0 else jnp.logical_not(left)
        qpad_ref[h] = jnp.where(keep, blk, 0.0).astype(BF16)
    k = norm_rope(za_ref[:, A_WIDTH:2 * A_WIDTH], kg_ref[...])
    kf_ref[...] = k
    kb_ref[...] = k.astype(BF16)
    v = za_ref[:, 2 * A_WIDTH:3 * A_WIDTH]
    vf_ref[...] = v
    vb_ref[...] = v.astype(BF16)

    nq = IDX_HEADS * IDX_DIM
    xi = za_ref[:, Z_IQ:Z_IQ + nq]
    lane_i = lax.broadcasted_iota(I32, (tm, nq), 1)
    ci = jnp.concatenate([ci_ref[...]] * 8, axis=1)
    si = jnp.concatenate([si_ref[...]] * 8, axis=1)
    swi = jnp.where((lane_i % IDX_DIM) < 16, pltpu.roll(xi, nq - 16, 1), pltpu.roll(xi, 16, 1))
    iq = xi * ci + swi * si
    for h in range(IDX_HEADS):
        blk = iq[:, (h // 2) * LANES:(h // 2 + 1) * LANES]
        keep = left if h % 2 == 0 else jnp.logical_not(left)
        iqpad_ref[h] = jnp.where(keep, blk, 0.0).astype(BF16)

    m = zm_ref[...]
    mu = jnp.sum(jnp.where(left, m, 0.0), axis=-1, keepdims=True) * (1.0 / IDX_DIM)
    xc = jnp.where(left, m - mu, 0.0)
    var = jnp.sum(xc * xc, axis=-1, keepdims=True) * (1.0 / IDX_DIM)
    y = xc * lax.rsqrt(var + EPS) * ikg_ref[...] + ikb_ref[...]
    swk = jnp.where(lane128 < 16, pltpu.roll(y, LANES - 16, 1), pltpu.roll(y, 16, 1))
    ikr = y * ci_ref[...] + swk * si_ref[...]
    ikf_ref[...] = ikr[:, 0:IDX_DIM]
    ikdup_ref[...] = jnp.where(left, ikr, pltpu.roll(ikr, 64, 1)).astype(BF16)
    misc_ref[...] = m * (IDX_HEADS ** -0.5 * IDX_DIM ** -0.5)


def _prep(z, tabs, t_seq, qg, kg, ikg, ikb, bd512):
    m = z.shape[0]
    ca, sa, ci, si = tabs
    tm = min(256, m)
    if t_seq == 1:
        tab_spec = pl.BlockSpec((1, LANES), lambda i: (0, 0))
    else:
        assert t_seq % tm == 0
        nb = t_seq // tm
        tab_spec = pl.BlockSpec((tm, LANES), lambda i: (i % nb, 0))
    row = lambda w: pl.BlockSpec((1, w), lambda i: (0, 0))
    outs = pl.pallas_call(
        _prep_kernel,
        grid=(m // tm,),
        in_specs=[pl.BlockSpec((tm, Z_BG), lambda i: (i, 0)),
                  pl.BlockSpec((tm, LANES), lambda i: (i, Z_MISC // LANES)),
                  tab_spec, tab_spec, tab_spec, tab_spec,
                  row(A_WIDTH), row(A_WIDTH), row(LANES), row(LANES),
                  pl.BlockSpec((A_WIDTH, A_WIDTH), lambda i: (0, 0))],
        out_specs=[pl.BlockSpec((A_HEADS, tm, LANES), lambda i: (0, i, 0)),
                   pl.BlockSpec((tm, A_WIDTH), lambda i: (i, 0)),
                   pl.BlockSpec((tm, A_WIDTH), lambda i: (i, 0)),
                   pl.BlockSpec((tm, A_WIDTH), lambda i: (i, 0)),
                   pl.BlockSpec((tm, A_WIDTH), lambda i: (i, 0)),
                   pl.BlockSpec((IDX_HEADS, tm, LANES), lambda i: (0, i, 0)),
                   pl.BlockSpec((tm, IDX_DIM), lambda i: (i, 0)),
                   pl.BlockSpec((tm, LANES), lambda i: (i, 0)),
                   pl.BlockSpec((tm, LANES), lambda i: (i, 0))],
        out_shape=[jax.ShapeDtypeStruct((A_HEADS, m, LANES), BF16),
                   jax.ShapeDtypeStruct((m, A_WIDTH), F32),
                   jax.ShapeDtypeStruct((m, A_WIDTH), BF16),
                   jax.ShapeDtypeStruct((m, A_WIDTH), F32),
                   jax.ShapeDtypeStruct((m, A_WIDTH), BF16),
                   jax.ShapeDtypeStruct((IDX_HEADS, m, LANES), BF16),
                   jax.ShapeDtypeStruct((m, IDX_DIM), F32),
                   jax.ShapeDtypeStruct((m, LANES), BF16),
                   jax.ShapeDtypeStruct((m, LANES), F32)],
        compiler_params=_cparams(("parallel",)),
        name="prep",
    )(z, z, ca, sa, ci, si, qg, kg, ikg, ikb, bd512)
    names = ("qpad", "kf", "kb", "vf", "vb", "iqpad", "ikf", "ikdup", "misc")
    return dict(zip(names, outs))


def _attn_prompt_kernel(qb_tab, kb_tab, nkb_tab, q_ref, iq_ref, iw_ref, ik_ref, k_ref, v_ref, o_ref,
                        key_scr, thr_scr, m_scr, l_scr, acc_scr, *, tq, tk, topk):
    s = pl.program_id(1)
    qb = qb_tab[s]
    kb = kb_tab[s]
    nkb = nkb_tab[s]
    tpos = qb * tq + lax.broadcasted_iota(I32, (tq, tk), 0)
    koff = lax.broadcasted_iota(I32, (tq, tk), 1)

    @pl.when(kb == 0)
    def _():
        iw = iw_ref[...]

        def score_chunk(c, carry):
            ikc = ik_ref[pl.ds(pl.multiple_of(c * tk, tk), tk), :]
            acc = jnp.zeros((tq, tk), F32)
            for h in range(IDX_HEADS):
                sh = _dot_nt(iq_ref[h], ikc)
                acc = acc + jnp.maximum(sh, 0.0) * iw[:, M_IW + h:M_IW + h + 1]
            acc = jnp.where(c * tk + koff <= tpos, acc, -jnp.inf)
            key_scr[c] = _to_key(acc)
            return carry

        lax.fori_loop(0, nkb, score_chunk, 0)

        def count_ge(cand):
            def body(c, cnt):
                ge = (key_scr[c] >= cand).astype(I32)
                for u in range(tk // LANES):
                    cnt = cnt + ge[:, u * LANES:(u + 1) * LANES]
                return cnt

            cnt = lax.fori_loop(0, nkb, body, jnp.zeros((tq, LANES), I32))
            return jnp.sum(cnt, axis=1, keepdims=True)

        thr = jnp.full((tq, 1), INT_MIN, I32)
        zero = jnp.zeros((tq, 1), I32)
        thr = jnp.where(count_ge(zero) >= topk, zero, thr)

        def bit_step(i, thr):
            cand = thr | jnp.left_shift(jnp.int32(1), 30 - i)
            return jnp.where(count_ge(cand) >= topk, cand, thr)

        thr_scr[...] = lax.fori_loop(0, 31, bit_step, thr)
        m_scr[...] = jnp.full(m_scr.shape, NEG, F32)
        l_scr[...] = jnp.zeros(l_scr.shape, F32)
        acc_scr[...] = jnp.zeros(acc_scr.shape, F32)

    sel = jnp.logical_and(key_scr[kb] >= thr_scr[...], kb * tk + koff <= tpos)
    for h in range(A_HEADS):
        p0 = (h // 2) * LANES
        sc = _dot_nt(q_ref[h], k_ref[:, p0:p0 + LANES])
        sc = jnp.where(sel, sc, NEG)
        m_old = m_scr[h]
        m_new = jnp.maximum(m_old, jnp.max(sc, axis=1, keepdims=True))
        alpha = jnp.exp(m_old - m_new)
        p = jnp.where(sel, jnp.exp(sc - m_new), 0.0)
        l_scr[h] = alpha * l_scr[h] + jnp.sum(p, axis=1, keepdims=True)
        acc_scr[h] = alpha * acc_scr[h] + _dot(p.astype(BF16), v_ref[:, p0:p0 + LANES])
        m_scr[h] = m_new

    @pl.when(kb == nkb - 1)
    def _():
        left = lax.broadcasted_iota(I32, (tq, LANES), 1) < 64
        for p in range(A_HEADS // 2):
            oa = acc_scr[2 * p] / l_scr[2 * p]
            ob = acc_scr[2 * p + 1] / l_scr[2 * p + 1]
            o_ref[:, p * LANES:(p + 1) * LANES] = jnp.where(left, oa, ob)


def _attn_prompt(pp, nb, s_len):
    tq, tk = 128, 512
    tk = min(tk, s_len)
    topk = min(TOPK_MAX, s_len // 4)
    nqb = s_len // tq
    nkblk = s_len // tk
    qb_l, kb_l, nkb_l = [], [], []
    for qb in range(nqb):
        nkb = ((qb + 1) * tq + tk - 1) // tk
        for kb in range(nkb):
            qb_l.append(qb)
            kb_l.append(kb)
            nkb_l.append(nkb)
    tabs = [jnp.asarray(np.array(a, np.int32)) for a in (qb_l, kb_l, nkb_l)]
    nsteps = len(qb_l)
    m = nb * s_len
    grid_spec = pltpu.PrefetchScalarGridSpec(
        num_scalar_prefetch=3,
        grid=(nb, nsteps),
        in_specs=[
            pl.BlockSpec((A_HEADS, tq, LANES), lambda b, s, qt, kt, nt: (0, b * nqb + qt[s], 0)),
            pl.BlockSpec((IDX_HEADS, tq, LANES), lambda b, s, qt, kt, nt: (0, b * nqb + qt[s], 0)),
            pl.BlockSpec((tq, LANES), lambda b, s, qt, kt, nt: (b * nqb + qt[s], 0)),
            pl.BlockSpec((s_len, LANES), lambda b, s, qt, kt, nt: (b, 0)),
            pl.BlockSpec((tk, A_WIDTH), lambda b, s, qt, kt, nt: (b * nkblk + kt[s], 0)),
            pl.BlockSpec((tk, A_WIDTH), lambda b, s, qt, kt, nt: (b * nkblk + kt[s], 0)),
        ],
        out_specs=pl.BlockSpec((tq, A_WIDTH), lambda b, s, qt, kt, nt: (b * nqb + qt[s], 0)),
        scratch_shapes=[pltpu.VMEM((nkblk, tq, tk), I32),
                        pltpu.VMEM((tq, 1), I32),
                        pltpu.VMEM((A_HEADS, tq, 1), F32),
                        pltpu.VMEM((A_HEADS, tq, 1), F32),
                        pltpu.VMEM((A_HEADS, tq, LANES), F32)],
    )
    return pl.pallas_call(
        functools.partial(_attn_prompt_kernel, tq=tq, tk=tk, topk=topk),
        grid_spec=grid_spec,
        out_shape=jax.ShapeDtypeStruct((m, A_WIDTH), F32),
        compiler_params=_cparams(("arbitrary", "arbitrary")),
        name="attn_prompt",
    )(*tabs, pp["qpad"], pp["iqpad"], pp["misc"], pp["ikdup"], pp["kb"], pp["vb"])


def _sample_score_kernel(pt_ref, iq_ref, iw_ref, *rest, npg):
    page_refs = rest[:npg]
    o_ref = rest[npg]
    iq = iq_ref[...]
    w = iw_ref[...]
    for g in range(npg):
        pg = page_refs[g][...].astype(BF16)
        sc = _dot_nt(iq, pg)
        o_ref[g:g + 1, :] = jnp.sum(jnp.maximum(sc, 0.0) * w, axis=0, keepdims=True)


def _sample_attn_kernel(pt_ref, sc_ref, iq_ref, iw_ref, ikn_ref, qbd_ref, kn_ref, vn_ref, *rest, npg, topk):
    k_refs = rest[:npg]
    v_refs = rest[npg:2 * npg]
    o_ref = rest[2 * npg]
    key_scr, thr_scr, nk_scr, m_scr, l_scr, acc_scr = rest[2 * npg + 1:]
    j = pl.program_id(1)
    nj = pl.num_programs(1)

    @pl.when(j == 0)
    def _():
        keys = _to_key(sc_ref[...])
        key_scr[...] = keys
        sn = jnp.sum(iq_ref[...].astype(F32) * ikn_ref[...].astype(F32), axis=1, keepdims=True)
        snew = jnp.sum(jnp.maximum(sn, 0.0) * iw_ref[...], axis=0, keepdims=True)
        knew = _to_key(snew)
        nk_scr[...] = knew

        def count_ge(cand):
            c = jnp.sum((keys >= cand).astype(I32), axis=1, keepdims=True)
            return jnp.sum(c, axis=0, keepdims=True) + (knew >= cand).astype(I32)

        thr = jnp.full((1, 1), INT_MIN, I32)
        zero = jnp.zeros((1, 1), I32)
        thr = jnp.where(count_ge(zero) >= topk, zero, thr)

        def bit_step(i, thr):
            cand = thr | jnp.left_shift(jnp.int32(1), 30 - i)
            return jnp.where(count_ge(cand) >= topk, cand, thr)

        thr_scr[...] = lax.fori_loop(0, 31, bit_step, thr)
        m_scr[...] = jnp.full(m_scr.shape, NEG, F32)
        l_scr[...] = jnp.zeros(l_scr.shape, F32)
        acc_scr[...] = jnp.zeros(acc_scr.shape, F32)

    qbd = qbd_ref[...]
    thr = thr_scr[...]
    for g in range(npg):
        sel = key_scr[pl.ds(j * npg + g, 1), :] >= thr
        sc = _dot_nt(qbd, k_refs[g][...].astype(BF16))
        sc = jnp.where(sel, sc, NEG)
        m_old = m_scr[...]
        m_new = jnp.maximum(m_old, jnp.max(sc, axis=1, keepdims=True))
        alpha = jnp.exp(m_old - m_new)
        p = jnp.where(sel, jnp.exp(sc - m_new), 0.0)
        l_scr[...] = alpha * l_scr[...] + jnp.sum(p, axis=1, keepdims=True)
        acc_scr[...] = alpha * acc_scr[...] + _dot(p.astype(BF16), v_refs[g][...].astype(BF16))
        m_scr[...] = m_new

    @pl.when(j == nj - 1)
    def _():
        sel = nk_scr[...] >= thr
        sc = jnp.sum(qbd.astype(F32) * kn_ref[...].astype(BF16).astype(F32), axis=1, keepdims=True)
        sc = jnp.where(sel, sc, NEG)
        m_old = m_scr[...]
        m_new = jnp.maximum(m_old, sc)
        alpha = jnp.exp(m_old - m_new)
        p = jnp.where(sel, jnp.exp(sc - m_new), 0.0)
        l_new = alpha * l_scr[...] + p
        acc = alpha * acc_scr[...] + p.astype(BF16).astype(F32) * vn_ref[...].astype(BF16).astype(F32)
        o = acc / l_new
        hrow = lax.broadcasted_iota(I32, (A_HEADS, A_WIDTH), 0)
        hcol = lax.broadcasted_iota(I32, (A_HEADS, A_WIDTH), 1) // HEAD_DIM
        o_ref[...] = jnp.sum(jnp.where(hrow == hcol, o, 0.0), axis=0, keepdims=True)


def _attn_sample(pp, cache_k, cache_v, cache_ik, page_table, layer):
    nb, npages = page_table.shape
    npg = 8
    assert npages % npg == 0
    topk = min(TOPK_MAX, (npages * PAGE_SIZE + 1) // 4)
    pt = page_table.reshape(-1).astype(I32)
    n_pool = cache_k.shape[1]
    ck = cache_k.reshape(cache_k.shape[0], n_pool, PAGE_SIZE, A_WIDTH)
    cv = cache_v.reshape(cache_v.shape[0], n_pool, PAGE_SIZE, A_WIDTH)

    iq = pp["iqpad"]
    iq = jnp.transpose(iq[:, :, :64] + iq[:, :, 64:], (1, 0, 2))
    iw = pp["misc"][:, M_IW:M_IW + IDX_HEADS].reshape(nb, IDX_HEADS, 1)
    ikn = pp["ikdup"][:, :IDX_DIM].reshape(nb, 1, IDX_DIM)
    qp = jnp.transpose(pp["qpad"], (1, 0, 2))
    qbd = jnp.zeros((nb, A_HEADS, A_HEADS // 2, LANES), BF16)
    for h in range(A_HEADS):
        qbd = qbd.at[:, h, h // 2, :].set(qp[:, h, :])
    qbd = qbd.reshape(nb, A_HEADS, A_WIDTH)
    kn = pp["kf"].reshape(nb, 1, A_WIDTH)
    vn = pp["vf"].reshape(nb, 1, A_WIDTH)

    def page_spec(width, g):
        return pl.BlockSpec((None, None, PAGE_SIZE, width),
                            lambda b, j, ptr: (layer, ptr[b * npages + j * npg + g], 0, 0))

    per_b = lambda *shape: pl.BlockSpec((None,) + shape, lambda b, j, ptr: (b,) + (0,) * len(shape))

    scores = pl.pallas_call(
        functools.partial(_sample_score_kernel, npg=npg),
        grid_spec=pltpu.PrefetchScalarGridSpec(
            num_scalar_prefetch=1,
            grid=(nb, npages // npg),
            in_specs=[per_b(IDX_HEADS, IDX_DIM), per_b(IDX_HEADS, 1)]
                     + [page_spec(IDX_DIM, g) for g in range(npg)],
            out_specs=pl.BlockSpec((None, npg, PAGE_SIZE), lambda b, j, ptr: (b, j, 0)),
        ),
        out_shape=jax.ShapeDtypeStruct((nb, npages, PAGE_SIZE), F32),
        compiler_params=_cparams(("arbitrary", "arbitrary")),
        name="sample_score",
    )(pt, iq, iw, *([cache_ik] * npg))

    out = pl.pallas_call(
        functools.partial(_sample_attn_kernel, npg=npg, topk=topk),
        grid_spec=pltpu.PrefetchScalarGridSpec(
            num_scalar_prefetch=1,
            grid=(nb, npages // npg),
            in_specs=[per_b(npages, PAGE_SIZE), per_b(IDX_HEADS, IDX_DIM), per_b(IDX_HEADS, 1),
                      per_b(1, IDX_DIM), per_b(A_HEADS, A_WIDTH), per_b(1, A_WIDTH), per_b(1, A_WIDTH)]
                     + [page_spec(A_WIDTH, g) for g in range(npg)]
                     + [page_spec(A_WIDTH, g) for g in range(npg)],
            out_specs=pl.BlockSpec((None, 1, A_WIDTH), lambda b, j, ptr: (b, 0, 0)),
            scratch_shapes=[pltpu.VMEM((npages, PAGE_SIZE), I32),
                            pltpu.VMEM((1, 1), I32),
                            pltpu.VMEM((1, 1), I32),
                            pltpu.VMEM((A_HEADS, 1), F32),
                            pltpu.VMEM((A_HEADS, 1), F32),
                            pltpu.VMEM((A_HEADS, A_WIDTH), F32)],
        ),
        out_shape=jax.ShapeDtypeStruct((nb, 1, A_WIDTH), F32),
        compiler_params=_cparams(("arbitrary", "arbitrary")),
        name="sample_attn",
    )(pt, scores, iq, iw, ikn, qbd, kn, vn, *([ck] * npg), *([cv] * npg))
    return out.reshape(nb, A_WIDTH)


def _unit_lower_inverse(mm, c):
    r = lax.broadcasted_iota(I32, (c, c), 0)
    q = lax.broadcasted_iota(I32, (c, c), 1)
    eye = (r == q).astype(F32)
    blk = min(16, c)
    pw = jnp.where(r // blk == q // blk, -mm, 0.0)
    inv = eye + pw
    k = 1
    while 2 * k < blk:
        pw = _dot(pw, pw, HIGHEST)
        inv = inv + _dot(inv, pw, HIGHEST)
        k *= 2
    sz = 2 * blk
    while sz <= c:
        half = sz // 2
        e = jnp.where(jnp.logical_and(r // sz == q // sz, r // half != q // half), mm, 0.0)
        inv = inv - _dot(inv, _dot(e, inv, HIGHEST), HIGHEST)
        sz *= 2
    return inv


def _delta_kernel(xc_ref, bg_ref, ms_ref, buf_ref, s0_ref, cw_ref, al_ref, dt_ref, dn_ref, bd512_ref, bd128_ref,
                  o_ref, sout_ref, nbuf_ref, xp_scr, s_scr, *, c, t_valid, nt):
    t = pl.program_id(1)

    @pl.when(t == 0)
    def _():
        xp_scr[5:8, :] = buf_ref[...]
        s_scr[...] = s0_ref[...]

    xp_scr[8:8 + c, :] = xc_ref[...]
    cw = cw_ref[...]
    y = (cw[0:1] * xp_scr[5:5 + c, :] + cw[1:2] * xp_scr[6:6 + c, :]
         + cw[2:3] * xp_scr[7:7 + c, :] + cw[3:4] * xp_scr[8:8 + c, :])

    tb, lt = (t_valid - 1) // c, (t_valid - 1) % c

    @pl.when(t == tb)
    def _():
        nbuf_ref[...] = xp_scr[8 + lt - 2:8 + lt + 1, :]

    hist = xp_scr[c + 5:c + 8, :]
    xp_scr[5:8, :] = hist

    y = _silu(y)
    ss = _group_sum(y[:, 0:512] * y[:, 0:512], bd512_ref[...])
    inv_n = lax.rsqrt(ss + EPS)
    qn = y[:, 0:256] * inv_n[:, 0:256] * (B_DK ** -0.5)
    kn = y[:, 256:512] * inv_n[:, 256:512]
    vv = y[:, 512:768]
    ms = ms_ref[...]
    g_all = -jnp.exp(al_ref[...]) * _softplus(ms + dt_ref[...])
    beta_all = jax.nn.sigmoid(ms)
    if t_valid % c != 0:
        vm = (t * c + lax.broadcasted_iota(I32, (c, 1), 0) < t_valid).astype(F32)
        g_all = g_all * vm
        beta_all = beta_all * vm
        kn = kn * vm
        vv = vv * vm

    r = lax.broadcasted_iota(I32, (c, c), 0)
    q = lax.broadcasted_iota(I32, (c, c), 1)
    incl = r >= q
    strict = r > q
    gcum_all = _dot(incl.astype(F32), g_all, HIGHEST)
    lane = lax.broadcasted_iota(I32, (c, LANES), 1)
    left = lane < 64
    ones = jnp.ones((c, LANES), F32)
    row128 = lax.broadcasted_iota(I32, (LANES, LANES), 0)
    col128 = lax.broadcasted_iota(I32, (LANES, LANES), 1)
    bdmask = (row128 // 64) == (col128 // 64)

    for p in range(B_HEADS // 2):
        kp = kn[:, p * LANES:(p + 1) * LANES]
        qp = qn[:, p * LANES:(p + 1) * LANES]
        vp = vv[:, p * LANES:(p + 1) * LANES]
        s_old = s_scr[p]
        u_pair = jnp.zeros((c, LANES), F32)
        w_pair = jnp.zeros((c, LANES), F32)
        pmats, gcs = [], []
        for e in range(2):
            h = 2 * p + e
            hm = left if e == 0 else jnp.logical_not(left)
            gc = gcum_all[:, M_BA + h:M_BA + h + 1]
            bt = beta_all[:, M_BB + h:M_BB + h + 1]
            grow = _dot_nt(ones, jnp.where(lane == M_BA + h, gcum_all, 0.0), HIGHEST)
            dec = jnp.exp(jnp.where(incl, gc - grow, 0.0))
            km = jnp.where(hm, kp, 0.0)
            kk = _dot_nt(km, kp, HIGHEST)
            mm = jnp.where(strict, bt * kk * dec, 0.0)
            inv = _unit_lower_inverse(mm, c)
            u_pair = u_pair + _dot(inv, bt * jnp.where(hm, vp, 0.0), HIGHEST)
            w_pair = w_pair + _dot(inv, (bt * jnp.exp(gc)) * km, HIGHEST)
            qk = _dot_nt(jnp.where(hm, qp, 0.0), kp, HIGHEST)
            pmats.append(jnp.where(incl, qk * dec, 0.0))
            gcs.append(gc)
        gc_pair = jnp.where(left, gcs[0], gcs[1])
        delta = u_pair - _dot(w_pair, s_old, HIGHEST)
        o = _dot(qp * jnp.exp(gc_pair), s_old, HIGHEST)
        o = o + _dot(pmats[0], jnp.where(left, delta, 0.0), HIGHEST)
        o = o + _dot(pmats[1], jnp.where(left, 0.0, delta), HIGHEST)
        glast = gc_pair[c - 1:c, :]
        kd = kp * jnp.exp(glast - gc_pair)
        gl_rows = jnp.where(row128 < 64, jnp.exp(gcs[0][c - 1:c, :]), jnp.exp(gcs[1][c - 1:c, :]))
        s_scr[p] = gl_rows * s_old + jnp.where(bdmask, _dot_tn(kd, delta, HIGHEST), 0.0)
        ms_o = _group_sum(o * o, bd128_ref[...]) * (1.0 / 64)
        o_ref[:, p * LANES:(p + 1) * LANES] = (o * lax.rsqrt(ms_o + EPS) * dn_ref[...]
                                               * _silu(bg_ref[:, p * LANES:(p + 1) * LANES]))

    @pl.when(t == nt - 1)
    def _():
        sout_ref[...] = s_scr[...]


def _delta(z, nb, t_pad, t_valid, buf0, s0_pair, cw, al_row, dt_row, dn_row, bd512, bd128):
    c = CHUNK
    nt = t_pad // c
    m = nb * t_pad
    const = lambda *shape: pl.BlockSpec(shape, lambda b, t: (0,) * len(shape))
    return pl.pallas_call(
        functools.partial(_delta_kernel, c=c, t_valid=t_valid, nt=nt),
        grid=(nb, nt),
        in_specs=[pl.BlockSpec((c, B_CONV), lambda b, t: (b * nt + t, Z_CONV // B_CONV)),
                  pl.BlockSpec((c, 256), lambda b, t: (b * nt + t, Z_BG // 256)),
                  pl.BlockSpec((c, LANES), lambda b, t: (b * nt + t, Z_MISC // LANES)),
                  pl.BlockSpec((None, CONV_W - 1, B_CONV), lambda b, t: (b, 0, 0)),
                  pl.BlockSpec((None, 2, LANES, LANES), lambda b, t: (b, 0, 0, 0)),
                  const(CONV_W, B_CONV), const(1, LANES), const(1, LANES), const(1, LANES),
                  const(512, 512), const(LANES, LANES)],
        out_specs=[pl.BlockSpec((c, 256), lambda b, t: (b * nt + t, 0)),
                   pl.BlockSpec((None, 2, LANES, LANES), lambda b, t: (b, 0, 0, 0)),
                   pl.BlockSpec((None, CONV_W - 1, B_CONV), lambda b, t: (b, 0, 0))],
        out_shape=[jax.ShapeDtypeStruct((m, 256), F32),
                   jax.ShapeDtypeStruct((nb, 2, LANES, LANES), F32),
                   jax.ShapeDtypeStruct((nb, CONV_W - 1, B_CONV), F32)],
        scratch_shapes=[pltpu.VMEM((8 + c, B_CONV), F32), pltpu.VMEM((2, LANES, LANES), F32)],
        compiler_params=_cparams(("arbitrary", "arbitrary")),
        name="delta",
    )(z, z, z, buf0, s0_pair, cw, al_row, dt_row, dn_row, bd512, bd128)


def _hgrn_kernel(x_ref, s0_ref, lb_ref, cn_ref, bd128_ref, o_ref, sout_ref, s_scr, *, c, t_valid, nt):
    t = pl.program_id(1)

    @pl.when(t == 0)
    def _():
        s_scr[...] = s0_ref[...]

    lb = lb_ref[...]
    cf = x_ref[:, 256:512]
    logf = jnp.log(lb + (1.0 - lb) * jax.nn.sigmoid(cf))
    kk = (1.0 - lb) * jax.nn.sigmoid(-cf)
    qq = _silu(x_ref[:, 0:256])
    vv = x_ref[:, 512:768]
    if t_valid % c != 0:
        vm = (t * c + lax.broadcasted_iota(I32, (c, 1), 0) < t_valid).astype(F32)
        logf = logf * vm
        kk = kk * vm

    r = lax.broadcasted_iota(I32, (c, c), 0)
    q = lax.broadcasted_iota(I32, (c, c), 1)
    b = _dot((r >= q).astype(F32), logf, HIGHEST)
    lane = lax.broadcasted_iota(I32, (c, LANES), 1)
    left = lane < 64
    row128 = lax.broadcasted_iota(I32, (LANES, LANES), 0)
    col128 = lax.broadcasted_iota(I32, (LANES, LANES), 1)
    bdmask = (row128 // 64) == (col128 // 64)
    eye128 = (row128 == col128).astype(F32)

    amats = [jnp.zeros((c, c), F32) for _ in range(C_HEADS)]
    half = c // 2
    while half >= 1:
        sz = 2 * half
        bnd = (r // sz) * sz + half - 1
        rl = _dot((q == bnd).astype(F32), b, HIGHEST)
        qf = qq * jnp.exp(jnp.minimum(b - rl, 0.0))
        kf = kk * jnp.exp(jnp.minimum(rl - b, 0.0))
        lmask = jnp.logical_and(r // sz == q // sz, jnp.logical_and(r % sz >= half, q % sz < half))
        for h in range(C_HEADS):
            p0 = (h // 2) * LANES
            hm = left if h % 2 == 0 else jnp.logical_not(left)
            a = _dot_nt(jnp.where(hm, qf[:, p0:p0 + LANES], 0.0), kf[:, p0:p0 + LANES], HIGHEST)
            amats[h] = amats[h] + jnp.where(lmask, a, 0.0)
        half //= 2

    eb = jnp.exp(b)
    for p in range(C_HEADS // 2):
        sl = slice(p * LANES, (p + 1) * LANES)
        s_old = s_scr[p]
        qp, kp, vp, bp = qq[:, sl], kk[:, sl], vv[:, sl], b[:, sl]
        o = _dot(qp * eb[:, sl], s_old, HIGHEST)
        o = o + _dot(amats[2 * p], jnp.where(left, vp, 0.0), HIGHEST)
        o = o + _dot(amats[2 * p + 1], jnp.where(left, 0.0, vp), HIGHEST)
        o = o + _group_sum(qp * kp, bd128_ref[...]) * vp
        blast = bp[c - 1:c, :]
        kd = kp * jnp.exp(blast - bp)
        dmat = eye128 * jnp.exp(blast)
        s_scr[p] = _dot(dmat, s_old, HIGHEST) + jnp.where(bdmask, _dot_tn(kd, vp, HIGHEST), 0.0)
        ms_o = _group_sum(o * o, bd128_ref[...]) * (1.0 / 64)
        o_ref[:, sl] = o * lax.rsqrt(ms_o + EPS) * cn_ref[...]

    @pl.when(t == nt - 1)
    def _():
        sout_ref[...] = s_scr[...]


def _hgrn(z, nb, t_pad, t_valid, s0_pair, lb_row, cn_row, bd128):
    c = CHUNK
    nt = t_pad // c
    m = nb * t_pad
    const = lambda *shape: pl.BlockSpec(shape, lambda b, t: (0,) * len(shape))
    return pl.pallas_call(
        functools.partial(_hgrn_kernel, c=c, t_valid=t_valid, nt=nt),
        grid=(nb, nt),
        in_specs=[pl.BlockSpec((c, 768), lambda b, t: (b * nt + t, Z_HG // 768)),
                  pl.BlockSpec((None, 2, LANES, LANES), lambda b, t: (b, 0, 0, 0)),
                  const(1, 256), const(1, LANES), const(LANES, LANES)],
        out_specs=[pl.BlockSpec((c, 256), lambda b, t: (b * nt + t, 0)),
                   pl.BlockSpec((None, 2, LANES, LANES), lambda b, t: (b, 0, 0, 0))],
        out_shape=[jax.ShapeDtypeStruct((m, 256), F32),
                   jax.ShapeDtypeStruct((nb, 2, LANES, LANES), F32)],
        scratch_shapes=[pltpu.VMEM((2, LANES, LANES), F32)],
        compiler_params=_cparams(("arbitrary", "arbitrary")),
        name="hgrn",
    )(z, s0_pair, lb_row, cn_row, bd128)


def _out_mlp_kernel(x_ref, oa_ref, ob_ref, oc_ref, wo_ref, g2_ref, w1_ref, w2_ref, y_ref, h_scr):
    @pl.when(pl.program_id(1) == 0)
    def _():
        x1 = (x_ref[...]
              + _dot(oa_ref[...].astype(BF16), wo_ref[0:A_WIDTH, :])
              + _dot(ob_ref[...].astype(BF16), wo_ref[A_WIDTH:A_WIDTH + 256, :])
              + _dot(oc_ref[...].astype(BF16), wo_ref[A_WIDTH + 256:A_WIDTH + 512, :]))
        y_ref[...] = x1
        ms = jnp.mean(x1 * x1, axis=-1, keepdims=True)
        h_scr[...] = (x1 * lax.rsqrt(ms + EPS) * g2_ref[...]).astype(BF16)

    u = jnp.maximum(_dot(h_scr[...], w1_ref[...]), 0.0)
    y_ref[...] += _dot((u * u).astype(BF16), w2_ref[...])


def _out_mlp(x, oa, ob, oc, wo, g2, w1, w2):
    m, d = x.shape
    tm = min(512, m)
    tf = 512
    nf = w1.shape[1] // tf
    tok = lambda w: pl.BlockSpec((tm, w), lambda i, j: (i, 0))
    return pl.pallas_call(
        _out_mlp_kernel,
        grid=(m // tm, nf),
        in_specs=[tok(d), tok(A_WIDTH), tok(256), tok(256),
                  pl.BlockSpec((d, d), lambda i, j: (0, 0)),
                  pl.BlockSpec((1, d), lambda i, j: (0, 0)),
                  pl.BlockSpec((d, tf), lambda i, j: (0, j)),
                  pl.BlockSpec((tf, d), lambda i, j: (j, 0))],
        out_specs=tok(d),
        out_shape=jax.ShapeDtypeStruct((m, d), F32),
        scratch_shapes=[pltpu.VMEM((tm, d), BF16)],
        compiler_params=_cparams(("parallel", "arbitrary")),
        name="out_mlp",
    )(x, oa, ob, oc, wo, g2, w1, w2)


def _relayout_w_in(w):
    d = w.shape[0]
    off = np.cumsum([0, 512, 512, 512, 1024, 16, 64, 256, 256, 256, 4, 4, 256, 256, 256, 256])
    (aq, ak, av, iq, iw, ik, bq, bk, bv, ba, bb, bg, cq, cf, ci, end) = [int(o) for o in off]
    zpad = lambda n: jnp.zeros((d, n), w.dtype)
    cols = [w[:, aq:iw],
            w[:, bg:cq],
            w[:, ik:bq], w[:, iw:ik], w[:, ba:bb], w[:, bb:bg], zpad(LANES - 88),
            zpad(Z_CONV - Z_MISC - LANES),
            w[:, bq:ba],
            w[:, cq:end]]
    out = jnp.concatenate(cols, axis=1)
    assert out.shape[1] == NZ
    return out.astype(BF16)


def _rope_tables(pos):
    pos = pos.astype(F32)
    half = HEAD_DIM // 2
    inv = ROPE_THETA ** (-jnp.arange(half, dtype=F32) / half)
    ang = pos[:, None] * inv[None, :]
    cos, sin = jnp.cos(ang), jnp.sin(ang)
    ca = jnp.tile(jnp.concatenate([cos, cos], -1), (1, 2))
    sa = jnp.tile(jnp.concatenate([-sin, sin], -1), (1, 2))
    half_i = IDX_ROPE // 2
    inv_i = ROPE_THETA ** (-jnp.arange(half_i, dtype=F32) / half_i)
    ang_i = pos[:, None] * inv_i[None, :]
    cos_i, sin_i = jnp.cos(ang_i), jnp.sin(ang_i)
    rest = IDX_DIM - IDX_ROPE
    one = jnp.ones((pos.shape[0], rest), F32)
    zero = jnp.zeros((pos.shape[0], rest), F32)
    ci = jnp.tile(jnp.concatenate([cos_i, cos_i, one], -1), (1, 2))
    si = jnp.tile(jnp.concatenate([-sin_i, sin_i, zero], -1), (1, 2))
    return ca, sa, ci, si


def _to_pair(s):
    nb = s.shape[0]
    s = s.reshape(nb, 2, 2, 64, 64)
    z = jnp.zeros_like(s[:, :, 0])
    top = jnp.concatenate([s[:, :, 0], z], axis=-1)
    bot = jnp.concatenate([z, s[:, :, 1]], axis=-1)
    return jnp.concatenate([top, bot], axis=-2)


def _from_pair(p):
    nb = p.shape[0]
    return jnp.stack([p[:, :, :64, :64], p[:, :, 64:, 64:]], axis=2).reshape(nb, 4, 64, 64)


def _lane_row(vals, offset):
    return jnp.zeros((1, LANES), F32).at[0, offset:offset + vals.shape[0]].set(vals.astype(F32))


def _block_diag_ones(n):
    i = np.arange(n) // 64
    return jnp.asarray((i[:, None] == i[None, :]).astype(np.float32)).astype(BF16)


def kernel(x_prompt, x_sample, cache_k, cache_v, cache_idx_k, state_delta, state_delta_conv, state_hgrn,
           page_table, norm1_g, w_in, q_norm_g, k_norm_g, idx_k_ln_g, idx_k_ln_b, conv_w, a_log, dt_bias,
           delta_norm_g, lb_param, hgrn_norm_g, w_out, norm2_g, w_mlp_in, w_mlp_out):
    depth = w_in.shape[0]
    bp, s_len, d = x_prompt.shape
    bd_, t_dec, _ = x_sample.shape
    assert t_dec == 1 and d == D_MODEL
    past = page_table.shape[1] * PAGE_SIZE

    lb_all = jnp.cumsum(jax.nn.softmax(lb_param.astype(F32), axis=0), axis=0)
    lb_all = lb_all - lb_all[0:1]
    tabs_p = _rope_tables(jnp.arange(s_len))
    tabs_s = _rope_tables(past + jnp.arange(t_dec))
    bd512 = _block_diag_ones(512)
    bd128 = _block_diag_ones(LANES)

    xp = x_prompt.reshape(bp * s_len, d)
    xs = x_sample.reshape(bd_ * t_dec, d)
    t_pad_s = CHUNK
    zeros_buf = jnp.zeros((bp, CONV_W - 1, B_CONV), F32)
    zeros_state = jnp.zeros((bp, 2, LANES, LANES), F32)

    outs_p = {k: [] for k in ("k", "v", "ik", "ds", "dc", "hs")}
    outs_s = {k: [] for k in ("k", "v", "ik", "ds", "dc", "hs")}
    for l in range(depth):
        w_in_l = _relayout_w_in(w_in[l])
        g1 = norm1_g[l].reshape(1, d)
        qg = jnp.tile(q_norm_g[l], A_HEADS).reshape(1, A_WIDTH)
        kg = jnp.tile(k_norm_g[l], A_HEADS).reshape(1, A_WIDTH)
        ikg = _lane_row(idx_k_ln_g[l], 0)
        ikb = _lane_row(idx_k_ln_b[l], 0)
        al_row = _lane_row(a_log[l], M_BA)
        dt_row = _lane_row(dt_bias[l], M_BA)
        dn_row = jnp.tile(delta_norm_g[l], 2).reshape(1, LANES).astype(F32)
        cn_row = jnp.tile(hgrn_norm_g[l], 2).reshape(1, LANES).astype(F32)
        lb_row = lb_all[l].reshape(1, 256)
        wo = w_out[l].astype(BF16)
        g2 = norm2_g[l].reshape(1, d)
        w1 = w_mlp_in[l].astype(BF16)
        w2 = w_mlp_out[l].astype(BF16)
        cw = conv_w[l]

        z = _norm_matmul(xp, g1, w_in_l, 768)
        pp = _prep(z, tabs_p, s_len, qg, kg, ikg, ikb, bd512)
        oa = _attn_prompt(pp, bp, s_len)
        ob, ds, dc = _delta(z, bp, s_len, s_len, zeros_buf, zeros_state, cw, al_row, dt_row, dn_row, bd512, bd128)
        oc, hs = _hgrn(z, bp, s_len, s_len, zeros_state, lb_row, cn_row, bd128)
        xp = _out_mlp(xp, oa, ob, oc, wo, g2, w1, w2)
        outs_p["k"].append(pp["kf"].reshape(bp, s_len, A_HEADS, HEAD_DIM))
        outs_p["v"].append(pp["vf"].reshape(bp, s_len, A_HEADS, HEAD_DIM))
        outs_p["ik"].append(pp["ikf"].reshape(bp, s_len, IDX_DIM))
        outs_p["ds"].append(_from_pair(ds))
        outs_p["dc"].append(dc)
        outs_p["hs"].append(_from_pair(hs))

        zs = _norm_matmul(xs, g1, w_in_l, 768)
        ps = _prep(zs, tabs_s, 1, qg, kg, ikg, ikb, bd512)
        oa_s = _attn_sample(ps, cache_k, cache_v, cache_idx_k, page_table, l)
        zs_pad = jnp.pad(zs.reshape(bd_, 1, NZ), ((0, 0), (0, t_pad_s - 1), (0, 0))).reshape(bd_ * t_pad_s, NZ)
        ob_s, ds_s, dc_s = _delta(zs_pad, bd_, t_pad_s, 1, state_delta_conv[l].astype(F32),
                                  _to_pair(state_delta[l].astype(F32)), cw, al_row, dt_row, dn_row, bd512, bd128)
        oc_s, hs_s = _hgrn(zs_pad, bd_, t_pad_s, 1, _to_pair(state_hgrn[l].astype(F32)), lb_row, cn_row, bd128)
        ob_s = ob_s.reshape(bd_, t_pad_s, 256)[:, 0]
        oc_s = oc_s.reshape(bd_, t_pad_s, 256)[:, 0]
        xs = _out_mlp(xs, oa_s, ob_s, oc_s, wo, g2, w1, w2)
        outs_s["k"].append(ps["kf"].reshape(bd_, t_dec, A_HEADS, HEAD_DIM))
        outs_s["v"].append(ps["vf"].reshape(bd_, t_dec, A_HEADS, HEAD_DIM))
        outs_s["ik"].append(ps["ikf"].reshape(bd_, t_dec, IDX_DIM))
        outs_s["ds"].append(_from_pair(ds_s))
        outs_s["dc"].append(dc_s)
        outs_s["hs"].append(_from_pair(hs_s))

    st = lambda o, k: jnp.stack(o[k])
    return (xp.reshape(bp, s_len, d), xs.reshape(bd_, t_dec, d),
            st(outs_p, "k"), st(outs_p, "v"), st(outs_p, "ik"), st(outs_p, "ds"), st(outs_p, "dc"), st(outs_p, "hs"),
            st(outs_s, "k"), st(outs_s, "v"), st(outs_s, "ik"), st(outs_s, "ds"), st(outs_s, "dc"), st(outs_s, "hs"))
```

```python
import functools

import numpy as np
import jax
import jax.numpy as jnp
from jax import lax
from jax.experimental import pallas as pl
from jax.experimental.pallas import tpu as pltpu

F32 = jnp.float32
BF16 = jnp.bfloat16
I32 = jnp.int32
HIGHEST = lax.Precision.HIGHEST

D_MODEL = 1024
HEAD_DIM = 64
A_HEADS = 8
A_WIDTH = A_HEADS * HEAD_DIM
IDX_HEADS = 16
IDX_DIM = 64
IDX_ROPE = 32
TOPK_MAX = 256
B_HEADS = 4
B_DK = 64
B_CONV = 768
CONV_W = 4
C_HEADS = 4
D_FF = 4 * D_MODEL
ROPE_THETA = 10000.0
EPS = 1e-6
PAGE_SIZE = 128
CHUNK = 64

LANES = 128
VMEM_LIMIT = 48 * 1024 * 1024

Z_IQ = 3 * A_WIDTH
Z_BG = Z_IQ + IDX_HEADS * IDX_DIM
Z_MISC = Z_BG + 256
Z_CONV = 3072
Z_HG = Z_CONV + B_CONV
NZ = Z_HG + 768
M_IW = 64
M_BA = 80
M_BB = 84

NEG = -1e30
INT_MIN = -2 ** 31
LOG2E = 1.4426950408889634
KSPLIT = 2


def _cparams(sem):
    return pltpu.CompilerParams(dimension_semantics=sem, vmem_limit_bytes=VMEM_LIMIT)


def _dot(a, b, precision=None):
    return jnp.dot(a, b, preferred_element_type=F32, precision=precision)


def _dot_nt(a, b, precision=None):
    return lax.dot_general(a, b, (((1,), (1,)), ((), ())), preferred_element_type=F32, precision=precision)


def _dot_tn(a, b, precision=None):
    return lax.dot_general(a, b, (((0,), (0,)), ((), ())), preferred_element_type=F32, precision=precision)


_DIMS = {"nn": (((1,), (0,)), ((), ())), "nt": (((1,), (1,)), ((), ())), "tn": (((0,), (0,)), ((), ()))}


def _bdot(a, b, form):
    return lax.dot_general(a, b, _DIMS[form], preferred_element_type=F32)


def _split2(x):
    hi = x.astype(BF16)
    return hi, (x - hi.astype(F32)).astype(BF16)


def _split3(x):
    hi = x.astype(BF16)
    r = x - hi.astype(F32)
    mid = r.astype(BF16)
    return hi, mid, (r - mid.astype(F32)).astype(BF16)


def _mm1(a, b, form="nn"):
    return _bdot(a.astype(BF16), b.astype(BF16), form)


def _mm3(a, b, form="nn"):
    ah, al = _split2(a)
    bh, bl = _split2(b)
    return _bdot(ah, bh, form) + (_bdot(ah, bl, form) + _bdot(al, bh, form))


def _mm_sel(a01, b, form="nn"):
    a = a01.astype(BF16)
    h, m, l = _split3(b)
    return _bdot(a, h, form) + (_bdot(a, m, form) + _bdot(a, l, form))


def _group_sum(y, bd):
    hi = y.astype(BF16)
    lo = (y - hi.astype(F32)).astype(BF16)
    return _dot(hi, bd) + _dot(lo, bd)


def _to_key(x):
    bits = lax.bitcast_convert_type(x, I32)
    return jnp.where(bits < 0, bits ^ jnp.int32(0x7FFFFFFF), bits)


def _silu(x):
    return x * jax.nn.sigmoid(x)


def _softplus(x):
    return jnp.maximum(x, 0.0) + jnp.log(1.0 + jnp.exp(-jnp.abs(x)))


def _norm_matmul_kernel(x_ref, g_ref, w_ref, o_ref, h_scr):
    @pl.when(pl.program_id(1) == 0)
    def _():
        x = x_ref[...]
        ms = jnp.mean(x * x, axis=-1, keepdims=True)
        h_scr[...] = (x * lax.rsqrt(ms + EPS) * g_ref[...]).astype(BF16)

    o_ref[...] = _dot(h_scr[...], w_ref[...])


def _norm_matmul(x, g, w, tn):
    m, d = x.shape
    n = w.shape[1]
    tm = min(512, m)
    return pl.pallas_call(
        _norm_matmul_kernel,
        grid=(m // tm, n // tn),
        in_specs=[pl.BlockSpec((tm, d), lambda i, j: (i, 0)),
                  pl.BlockSpec((1, d), lambda i, j: (0, 0)),
                  pl.BlockSpec((d, tn), lambda i, j: (0, j))],
        out_specs=pl.BlockSpec((tm, tn), lambda i, j: (i, j)),
        out_shape=jax.ShapeDtypeStruct((m, n), F32),
        scratch_shapes=[pltpu.VMEM((tm, d), BF16)],
        compiler_params=_cparams(("parallel", "arbitrary")),
        name="norm_matmul",
    )(x, g, w)


def _prep_values(za_ref, zm_ref, ca_ref, sa_ref, ci_ref, si_ref, qg_ref, kg_ref, ikg_ref, ikb_ref, bd_ref):
    tm = za_ref.shape[0]
    bd = bd_ref[...]
    ca = jnp.concatenate([ca_ref[...]] * 4, axis=1)
    sa = jnp.concatenate([sa_ref[...]] * 4, axis=1)
    lane_a = lax.broadcasted_iota(I32, (tm, A_WIDTH), 1)
    lo_half = (lane_a % HEAD_DIM) < (HEAD_DIM // 2)
    lane128 = lax.broadcasted_iota(I32, (tm, LANES), 1)
    left = lane128 < 64

    def norm_rope(x, g):
        ss = _group_sum(x * x, bd)
        y = x * lax.rsqrt(ss * (1.0 / HEAD_DIM) + EPS) * g
        sw = jnp.where(lo_half, pltpu.roll(y, A_WIDTH - 32, 1), pltpu.roll(y, 32, 1))
        return y * ca + sw * sa

    q = norm_rope(za_ref[:, 0:A_WIDTH], qg_ref[...]) * (HEAD_DIM ** -0.5 * LOG2E)
    k = norm_rope(za_ref[:, A_WIDTH:2 * A_WIDTH], kg_ref[...])
    v = za_ref[:, 2 * A_WIDTH:3 * A_WIDTH]

    nq = IDX_HEADS * IDX_DIM
    xi = za_ref[:, Z_IQ:Z_IQ + nq]
    lane_i = lax.broadcasted_iota(I32, (tm, nq), 1)
    ci = jnp.concatenate([ci_ref[...]] * 8, axis=1)
    si = jnp.concatenate([si_ref[...]] * 8, axis=1)
    swi = jnp.where((lane_i % IDX_DIM) < 16, pltpu.roll(xi, nq - 16, 1), pltpu.roll(xi, 16, 1))
    iq = xi * ci + swi * si

    m = zm_ref[...]
    mu = jnp.sum(jnp.where(left, m, 0.0), axis=-1, keepdims=True) * (1.0 / IDX_DIM)
    xc = jnp.where(left, m - mu, 0.0)
    var = jnp.sum(xc * xc, axis=-1, keepdims=True) * (1.0 / IDX_DIM)
    y = xc * lax.rsqrt(var + EPS) * ikg_ref[...] + ikb_ref[...]
    swk = jnp.where(lane128 < 16, pltpu.roll(y, LANES - 16, 1), pltpu.roll(y, 16, 1))
    ikr = y * ci_ref[...] + swk * si_ref[...]
    ikdup = jnp.where(left, ikr, pltpu.roll(ikr, 64, 1)).astype(BF16)
    misc = m * (IDX_HEADS ** -0.5 * IDX_DIM ** -0.5)
    return q, k, v, iq, ikr, ikdup, misc


def _prep_kernel(*refs):
    ins, (qpad_ref, kf_ref, vf_ref, iqpad_ref, ikf_ref, ikdup_ref, misc_ref) = refs[:11], refs[11:]
    q, k, v, iq, ikr, ikdup, misc = _prep_values(*ins)
    left = lax.broadcasted_iota(I32, (q.shape[0], LANES), 1) < 64
    for h in range(A_HEADS):
        keep = left if h % 2 == 0 else jnp.logical_not(left)
        qpad_ref[h] = jnp.where(keep, q[:, (h // 2) * LANES:(h // 2 + 1) * LANES], 0.0).astype(BF16)
    for h in range(IDX_HEADS):
        keep = left if h % 2 == 0 else jnp.logical_not(left)
        iqpad_ref[h] = jnp.where(keep, iq[:, (h // 2) * LANES:(h // 2 + 1) * LANES], 0.0).astype(BF16)
    kf_ref[...] = k
    vf_ref[...] = v
    ikf_ref[...] = ikr[:, 0:IDX_DIM]
    ikdup_ref[...] = ikdup
    misc_ref[...] = misc


def _prep_t_kernel(*refs):
    ins, (qt_ref, kt_ref, kb_ref, vt_ref, vaug_ref, iqt_ref, ikt_ref, ikdup_ref, iwt_ref) = refs[:11], refs[11:]
    q, k, v, iq, ikr, ikdup, misc = _prep_values(*ins)
    tm = q.shape[0]
    top = lax.broadcasted_iota(I32, (LANES, tm), 0) < 64
    qt = q.T
    vt = v.T
    iqt = iq.T
    for h in range(A_HEADS):
        keep = top if h % 2 == 0 else jnp.logical_not(top)
        rows = slice((h // 2) * LANES, (h // 2 + 1) * LANES)
        qt_ref[h] = jnp.where(keep, qt[rows, :], 0.0).astype(BF16)
        vaug_ref[h] = jnp.where(keep, vt[rows, :], 1.0).astype(BF16)
    for h in range(IDX_HEADS):
        keep = top if h % 2 == 0 else jnp.logical_not(top)
        iqt_ref[h] = jnp.where(keep, iqt[(h // 2) * LANES:(h // 2 + 1) * LANES, :], 0.0).astype(BF16)
    kt_ref[...] = k.T
    kb_ref[...] = k.astype(BF16)
    vt_ref[...] = vt
    ikt_ref[...] = ikr.T[0:IDX_DIM, :]
    ikdup_ref[...] = ikdup
    iwt_ref[...] = misc.T[M_IW:M_IW + IDX_HEADS, :]


def _prep_in_specs(tm, tab_spec):
    row = lambda w: pl.BlockSpec((1, w), lambda i: (0, 0))
    return [pl.BlockSpec((tm, Z_BG), lambda i: (i, 0)),
            pl.BlockSpec((tm, LANES), lambda i: (i, Z_MISC // LANES)),
            tab_spec, tab_spec, tab_spec, tab_spec,
            row(A_WIDTH), row(A_WIDTH), row(LANES), row(LANES),
            pl.BlockSpec((A_WIDTH, A_WIDTH), lambda i: (0, 0))]


def _prep(z, tabs, qg, kg, ikg, ikb, bd512):
    m = z.shape[0]
    tm = m
    tab_spec = pl.BlockSpec((1, LANES), lambda i: (0, 0))
    tok = lambda w: pl.BlockSpec((tm, w), lambda i: (i, 0))
    outs = pl.pallas_call(
        _prep_kernel,
        grid=(m // tm,),
        in_specs=_prep_in_specs(tm, tab_spec),
        out_specs=[pl.BlockSpec((A_HEADS, tm, LANES), lambda i: (0, i, 0)),
                   tok(A_WIDTH), tok(A_WIDTH),
                   pl.BlockSpec((IDX_HEADS, tm, LANES), lambda i: (0, i, 0)),
                   tok(IDX_DIM), tok(LANES), tok(LANES)],
        out_shape=[jax.ShapeDtypeStruct((A_HEADS, m, LANES), BF16),
                   jax.ShapeDtypeStruct((m, A_WIDTH), F32),
                   jax.ShapeDtypeStruct((m, A_WIDTH), F32),
                   jax.ShapeDtypeStruct((IDX_HEADS, m, LANES), BF16),
                   jax.ShapeDtypeStruct((m, IDX_DIM), F32),
                   jax.ShapeDtypeStruct((m, LANES), BF16),
                   jax.ShapeDtypeStruct((m, LANES), F32)],
        compiler_params=_cparams(("parallel",)),
        name="prep",
    )(z, z, *tabs, qg, kg, ikg, ikb, bd512)
    names = ("qpad", "kf", "vf", "iqpad", "ikf", "ikdup", "misc")
    return dict(zip(names, outs))


def _prep_t(z, tabs, nb, t_seq, qg, kg, ikg, ikb, bd512):
    m = z.shape[0]
    tm = min(256, t_seq)
    assert t_seq % tm == 0
    nblk = t_seq // tm
    tab_spec = pl.BlockSpec((tm, LANES), lambda i: (i % nblk, 0))
    feat = lambda f: pl.BlockSpec((None, f, tm), lambda i: (i // nblk, 0, i % nblk))
    heads = lambda nh: pl.BlockSpec((nh, LANES, tm), lambda i: (0, 0, i))
    outs = pl.pallas_call(
        _prep_t_kernel,
        grid=(m // tm,),
        in_specs=_prep_in_specs(tm, tab_spec),
        out_specs=[heads(A_HEADS), feat(A_WIDTH),
                   pl.BlockSpec((tm, A_WIDTH), lambda i: (i, 0)),
                   feat(A_WIDTH), heads(A_HEADS), heads(IDX_HEADS), feat(IDX_DIM),
                   pl.BlockSpec((tm, LANES), lambda i: (i, 0)),
                   pl.BlockSpec((IDX_HEADS, tm), lambda i: (0, i))],
        out_shape=[jax.ShapeDtypeStruct((A_HEADS, LANES, m), BF16),
                   jax.ShapeDtypeStruct((nb, A_WIDTH, t_seq), F32),
                   jax.ShapeDtypeStruct((m, A_WIDTH), BF16),
                   jax.ShapeDtypeStruct((nb, A_WIDTH, t_seq), F32),
                   jax.ShapeDtypeStruct((A_HEADS, LANES, m), BF16),
                   jax.ShapeDtypeStruct((IDX_HEADS, LANES, m), BF16),
                   jax.ShapeDtypeStruct((nb, IDX_DIM, t_seq), F32),
                   jax.ShapeDtypeStruct((m, LANES), BF16),
                   jax.ShapeDtypeStruct((IDX_HEADS, m), F32)],
        compiler_params=_cparams(("parallel",)),
        name="prep_t",
    )(z, z, *tabs, qg, kg, ikg, ikb, bd512)
    names = ("qt", "kt", "kb", "vt", "vaug", "iqt", "ikt", "ikdup", "iwt")
    return dict(zip(names, outs))


def _attn_prompt_kernel(qb_tab, kb_tab, nkb_tab, qt_ref, iqt_ref, iwt_ref, ik_ref, k_ref, vaug_ref, o_ref,
                        key_scr, thr_scr, m_scr, acc_scr, iq2_scr, *, tq, tk, topk):
    s = pl.program_id(1)
    qb = qb_tab[s]
    kb = kb_tab[s]
    nkb = nkb_tab[s]
    koff = lax.broadcasted_iota(I32, (tk, tq), 0)
    tpos = qb * tq + lax.broadcasted_iota(I32, (tk, tq), 1)

    @pl.when(kb == 0)
    def _():
        iwt = iwt_ref[...]
        for j in range(IDX_HEADS // 2):
            iq2_scr[j] = jnp.concatenate([iqt_ref[2 * j], iqt_ref[2 * j + 1]], axis=1)

        def score_chunk(c, carry):
            ikc = ik_ref[pl.ds(pl.multiple_of(c * tk, tk), tk), :]
            acc = jnp.zeros((tk, tq), F32)
            for j in range(IDX_HEADS // 2):
                s2 = _dot(ikc, iq2_scr[j])
                acc = acc + jnp.maximum(s2[:, 0:tq], 0.0) * iwt[2 * j:2 * j + 1, :]
                acc = acc + jnp.maximum(s2[:, tq:2 * tq], 0.0) * iwt[2 * j + 1:2 * j + 2, :]
            acc = jnp.where(c * tk + koff <= tpos, acc, -jnp.inf)
            key_scr[c] = _to_key(acc)
            return carry

        lax.fori_loop(0, nkb, score_chunk, 0)

        def count_ge(cand):
            def body(c, cnt):
                ge = (key_scr[c] >= cand).astype(I32)
                return cnt + jnp.sum(ge.reshape(tk // 8, 8, tq), axis=0)

            cnt = lax.fori_loop(0, nkb, body, jnp.zeros((8, tq), I32))
            return jnp.sum(cnt, axis=0, keepdims=True)

        zero = jnp.zeros((1, tq), I32)
        c0 = count_ge(zero)
        thr0 = jnp.where(c0 >= topk, zero, jnp.full((1, tq), INT_MIN, I32))
        n_adm = qb * tq + lax.broadcasted_iota(I32, (1, tq), 1) + 1
        open0 = jnp.logical_and(n_adm >= topk, c0 != topk).astype(I32)

        def cond(st):
            i, _, _, n_open = st
            return jnp.logical_and(i < 31, n_open > 0)

        def bit_step(st):
            i, thr, opn, _ = st
            cand = thr | jnp.left_shift(jnp.int32(1), 30 - i)
            cnt = count_ge(cand)
            thr = jnp.where(cnt >= topk, cand, thr)
            opn = jnp.where(cnt == topk, 0, opn)
            return i + 1, thr, opn, jnp.sum(opn)

        _, thr, _, _ = lax.while_loop(cond, bit_step, (jnp.int32(0), thr0, open0, jnp.sum(open0)))
        thr_scr[...] = thr
        m_scr[...] = jnp.full(m_scr.shape, NEG, F32)
        acc_scr[...] = jnp.zeros(acc_scr.shape, F32)

    sel = jnp.logical_and(key_scr[kb] >= thr_scr[...], kb * tk + koff <= tpos)
    bias = jnp.where(sel, 0.0, NEG)
    ts = tk // KSPLIT
    for u in range(KSPLIT):
        rows = slice(u * ts, (u + 1) * ts)
        for h in range(A_HEADS):
            p0 = (h // 2) * LANES
            sc = _dot(k_ref[rows, p0:p0 + LANES], qt_ref[h]) + bias[rows, :]
            m_old = m_scr[h:h + 1, :]
            m_new = jnp.maximum(m_old, jnp.max(sc, axis=0, keepdims=True))
            alpha = jnp.exp2(m_old - m_new)
            p = jnp.exp2(sc - m_new).astype(BF16)
            acc_scr[h] = alpha * acc_scr[h] + _dot(vaug_ref[h, :, rows], p)
            m_scr[h:h + 1, :] = m_new

    @pl.when(kb == nkb - 1)
    def _():
        for p in range(A_HEADS // 2):
            a = acc_scr[2 * p]
            b = acc_scr[2 * p + 1]
            ot = jnp.concatenate([a[0:64, :] / a[64:65, :], b[64:128, :] / b[0:1, :]], axis=0)
            o_ref[:, p * LANES:(p + 1) * LANES] = ot.T


def _attn_prompt(pp, nb, s_len):
    tq, tk = 128, 512
    tk = min(tk, s_len)
    topk = min(TOPK_MAX, s_len // 4)
    nqb = s_len // tq
    nkblk = s_len // tk
    qb_l, kb_l, nkb_l = [], [], []
    for qb in range(nqb):
        nkb = ((qb + 1) * tq + tk - 1) // tk
        for kb in range(nkb):
            qb_l.append(qb)
            kb_l.append(kb)
            nkb_l.append(nkb)
    tabs = [jnp.asarray(np.array(a, np.int32)) for a in (qb_l, kb_l, nkb_l)]
    nsteps = len(qb_l)
    m = nb * s_len
    grid_spec = pltpu.PrefetchScalarGridSpec(
        num_scalar_prefetch=3,
        grid=(nb, nsteps),
        in_specs=[
            pl.BlockSpec((A_HEADS, LANES, tq), lambda b, s, qt, kt, nt: (0, 0, b * nqb + qt[s])),
            pl.BlockSpec((IDX_HEADS, LANES, tq), lambda b, s, qt, kt, nt: (0, 0, b * nqb + qt[s])),
            pl.BlockSpec((IDX_HEADS, tq), lambda b, s, qt, kt, nt: (0, b * nqb + qt[s])),
            pl.BlockSpec((s_len, LANES), lambda b, s, qt, kt, nt: (b, 0)),
            pl.BlockSpec((tk, A_WIDTH), lambda b, s, qt, kt, nt: (b * nkblk + kt[s], 0)),
            pl.BlockSpec((A_HEADS, LANES, tk), lambda b, s, qt, kt, nt: (0, 0, b * nkblk + kt[s])),
        ],
        out_specs=pl.BlockSpec((tq, A_WIDTH), lambda b, s, qt, kt, nt: (b * nqb + qt[s], 0)),
        scratch_shapes=[pltpu.VMEM((nkblk, tk, tq), I32),
                        pltpu.VMEM((1, tq), I32),
                        pltpu.VMEM((A_HEADS, tq), F32),
                        pltpu.VMEM((A_HEADS, LANES, tq), F32),
                        pltpu.VMEM((IDX_HEADS // 2, LANES, 2 * tq), BF16)],
    )
    return pl.pallas_call(
        functools.partial(_attn_prompt_kernel, tq=tq, tk=tk, topk=topk),
        grid_spec=grid_spec,
        out_shape=jax.ShapeDtypeStruct((m, A_WIDTH), F32),
        compiler_params=_cparams(("arbitrary", "arbitrary")),
        name="attn_prompt",
    )(*tabs, pp["qt"], pp["iqt"], pp["iwt"], pp["ikdup"], pp["kb"], pp["vaug"])


def _sample_score_kernel(pt_ref, iq_ref, iw_ref, *rest, npg):
    page_refs = rest[:npg]
    o_ref = rest[npg]
    iq = iq_ref[...]
    w = iw_ref[...]
    for g in range(npg):
        pg = page_refs[g][...].astype(BF16)
        sc = _dot(iq, pg)
        o_ref[g:g + 1, :] = jnp.sum(jnp.maximum(sc, 0.0) * w, axis=0, keepdims=True)


def _sample_attn_kernel(pt_ref, sc_ref, iq_ref, iw_ref, ikn_ref, qbd_ref, kn_ref, vn_ref, *rest, npg, topk):
    k_refs = rest[:npg]
    v_refs = rest[npg:2 * npg]
    o_ref = rest[2 * npg]
    key_scr, thr_scr, nk_scr, m_scr, l_scr, acc_scr = rest[2 * npg + 1:]
    j = pl.program_id(1)
    nj = pl.num_programs(1)

    @pl.when(j == 0)
    def _():
        keys = _to_key(sc_ref[...])
        key_scr[...] = keys
        sn = jnp.sum(iq_ref[...].astype(F32) * ikn_ref[...].astype(F32), axis=1, keepdims=True)
        snew = jnp.sum(jnp.maximum(sn, 0.0) * iw_ref[...], axis=0, keepdims=True)
        knew = _to_key(snew)
        nk_scr[...] = knew

        def count_ge(cand):
            c = jnp.sum((keys >= cand).astype(I32), axis=1, keepdims=True)
            return jnp.sum(c, axis=0, keepdims=True) + (knew >= cand).astype(I32)

        thr = jnp.full((1, 1), INT_MIN, I32)
        zero = jnp.zeros((1, 1), I32)
        thr = jnp.where(count_ge(zero) >= topk, zero, thr)

        def bit_step(i, thr):
            cand = thr | jnp.left_shift(jnp.int32(1), 30 - i)
            return jnp.where(count_ge(cand) >= topk, cand, thr)

        thr_scr[...] = lax.fori_loop(0, 31, bit_step, thr)
        m_scr[...] = jnp.full(m_scr.shape, NEG, F32)
        l_scr[...] = jnp.zeros(l_scr.shape, F32)
        acc_scr[...] = jnp.zeros(acc_scr.shape, F32)

    qbd = qbd_ref[...]
    thr = thr_scr[...]
    sel = jnp.concatenate([key_scr[pl.ds(j * npg + g, 1), :] >= thr for g in range(npg)], axis=1)
    sc = jnp.concatenate([_dot(qbd, k_refs[g][...].astype(BF16)) for g in range(npg)], axis=1)
    sc = jnp.where(sel, sc, NEG)
    m_old = m_scr[...]
    m_new = jnp.maximum(m_old, jnp.max(sc, axis=1, keepdims=True))
    alpha = jnp.exp2(m_old - m_new)
    p = jnp.where(sel, jnp.exp2(sc - m_new), 0.0)
    l_scr[...] = alpha * l_scr[...] + jnp.sum(p, axis=1, keepdims=True)
    pv = jnp.zeros(acc_scr.shape, F32)
    for g in range(npg):
        pv = pv + _dot_nt(p[:, g * PAGE_SIZE:(g + 1) * PAGE_SIZE].astype(BF16), v_refs[g][...].astype(BF16))
    acc_scr[...] = alpha * acc_scr[...] + pv
    m_scr[...] = m_new

    @pl.when(j == nj - 1)
    def _():
        sel = nk_scr[...] >= thr
        sc = jnp.sum(qbd.astype(F32) * kn_ref[...].astype(BF16).astype(F32), axis=1, keepdims=True)
        sc = jnp.where(sel, sc, NEG)
        m_old = m_scr[...]
        m_new = jnp.maximum(m_old, sc)
        alpha = jnp.exp2(m_old - m_new)
        p = jnp.where(sel, jnp.exp2(sc - m_new), 0.0)
        l_new = alpha * l_scr[...] + p
        acc = alpha * acc_scr[...] + p.astype(BF16).astype(F32) * vn_ref[...].astype(BF16).astype(F32)
        o = acc / l_new
        hrow = lax.broadcasted_iota(I32, (A_HEADS, A_WIDTH), 0)
        hcol = lax.broadcasted_iota(I32, (A_HEADS, A_WIDTH), 1) // HEAD_DIM
        o_ref[...] = jnp.sum(jnp.where(hrow == hcol, o, 0.0), axis=0, keepdims=True)


def _attn_sample(pp, cache_k, cache_v, cache_ik, page_table, layer):
    nb, npages = page_table.shape
    npg = 8
    assert npages % npg == 0
    topk = min(TOPK_MAX, (npages * PAGE_SIZE + 1) // 4)
    pt = page_table.reshape(-1).astype(I32)
    n_pool = cache_k.shape[1]
    ck = jnp.transpose(cache_k, (0, 1, 3, 4, 2)).reshape(cache_k.shape[0], n_pool, A_WIDTH, PAGE_SIZE)
    cv = jnp.transpose(cache_v, (0, 1, 3, 4, 2)).reshape(cache_v.shape[0], n_pool, A_WIDTH, PAGE_SIZE)
    cache_ik = jnp.transpose(cache_ik, (0, 1, 3, 2))

    iq = pp["iqpad"]
    iq = jnp.transpose(iq[:, :, :64] + iq[:, :, 64:], (1, 0, 2))
    iw = pp["misc"][:, M_IW:M_IW + IDX_HEADS].reshape(nb, IDX_HEADS, 1)
    ikn = pp["ikdup"][:, :IDX_DIM].reshape(nb, 1, IDX_DIM)
    qp = jnp.transpose(pp["qpad"], (1, 0, 2))
    qbd = jnp.zeros((nb, A_HEADS, A_HEADS // 2, LANES), BF16)
    for h in range(A_HEADS):
        qbd = qbd.at[:, h, h // 2, :].set(qp[:, h, :])
    qbd = qbd.reshape(nb, A_HEADS, A_WIDTH)
    kn = pp["kf"].reshape(nb, 1, A_WIDTH)
    vn = pp["vf"].reshape(nb, 1, A_WIDTH)

    def page_spec(width, g):
        return pl.BlockSpec((None, None, width, PAGE_SIZE),
                            lambda b, j, ptr: (layer, ptr[b * npages + j * npg + g], 0, 0))

    per_b = lambda *shape: pl.BlockSpec((None,) + shape, lambda b, j, ptr: (b,) + (0,) * len(shape))

    scores = pl.pallas_call(
        functools.partial(_sample_score_kernel, npg=npg),
        grid_spec=pltpu.PrefetchScalarGridSpec(
            num_scalar_prefetch=1,
            grid=(nb, npages // npg),
            in_specs=[per_b(IDX_HEADS, IDX_DIM), per_b(IDX_HEADS, 1)]
                     + [page_spec(IDX_DIM, g) for g in range(npg)],
            out_specs=pl.BlockSpec((None, npg, PAGE_SIZE), lambda b, j, ptr: (b, j, 0)),
        ),
        out_shape=jax.ShapeDtypeStruct((nb, npages, PAGE_SIZE), F32),
        compiler_params=_cparams(("arbitrary", "arbitrary")),
        name="sample_score",
    )(pt, iq, iw, *([cache_ik] * npg))

    out = pl.pallas_call(
        functools.partial(_sample_attn_kernel, npg=npg, topk=topk),
        grid_spec=pltpu.PrefetchScalarGridSpec(
            num_scalar_prefetch=1,
            grid=(nb, npages // npg),
            in_specs=[per_b(npages, PAGE_SIZE), per_b(IDX_HEADS, IDX_DIM), per_b(IDX_HEADS, 1),
                      per_b(1, IDX_DIM), per_b(A_HEADS, A_WIDTH), per_b(1, A_WIDTH), per_b(1, A_WIDTH)]
                     + [page_spec(A_WIDTH, g) for g in range(npg)]
                     + [page_spec(A_WIDTH, g) for g in range(npg)],
            out_specs=pl.BlockSpec((None, 1, A_WIDTH), lambda b, j, ptr: (b, 0, 0)),
            scratch_shapes=[pltpu.VMEM((npages, PAGE_SIZE), I32),
                            pltpu.VMEM((1, 1), I32),
                            pltpu.VMEM((1, 1), I32),
                            pltpu.VMEM((A_HEADS, 1), F32),
                            pltpu.VMEM((A_HEADS, 1), F32),
                            pltpu.VMEM((A_HEADS, A_WIDTH), F32)],
        ),
        out_shape=jax.ShapeDtypeStruct((nb, 1, A_WIDTH), F32),
        compiler_params=_cparams(("arbitrary", "arbitrary")),
        name="sample_attn",
    )(pt, scores, iq, iw, ikn, qbd, kn, vn, *([ck] * npg), *([cv] * npg))
    return out.reshape(nb, A_WIDTH)


def _unit_lower_inverse(mm, c):
    r = lax.broadcasted_iota(I32, (c, c), 0)
    q = lax.broadcasted_iota(I32, (c, c), 1)
    eye = (r == q).astype(F32)
    blk = min(16, c)
    pw = jnp.where(r // blk == q // blk, -mm, 0.0)
    inv = eye + pw
    k = 1
    while 2 * k < blk:
        pw = _mm3(pw, pw)
        inv = inv + _mm3(inv, pw)
        k *= 2
    sz = 2 * blk
    while sz <= c:
        half = sz // 2
        e = jnp.where(jnp.logical_and(r // sz == q // sz, r // half != q // half), mm, 0.0)
        inv = inv - _mm3(inv, _mm3(e, inv))
        sz *= 2
    return inv


def _delta_kernel(xc_ref, bg_ref, ms_ref, buf_ref, s0_ref, cw_ref, al_ref, dt_ref, dn_ref, bd512_ref, bd128_ref,
                  o_ref, sout_ref, nbuf_ref, xp_scr, s_scr, *, c, t_valid, nt):
    t = pl.program_id(1)

    @pl.when(t == 0)
    def _():
        xp_scr[5:8, :] = buf_ref[...]
        s_scr[...] = s0_ref[...]

    xp_scr[8:8 + c, :] = xc_ref[...]
    cw = cw_ref[...]
    y = (cw[0:1] * xp_scr[5:5 + c, :] + cw[1:2] * xp_scr[6:6 + c, :]
         + cw[2:3] * xp_scr[7:7 + c, :] + cw[3:4] * xp_scr[8:8 + c, :])

    tb, lt = (t_valid - 1) // c, (t_valid - 1) % c

    @pl.when(t == tb)
    def _():
        nbuf_ref[...] = xp_scr[8 + lt - 2:8 + lt + 1, :]

    hist = xp_scr[c + 5:c + 8, :]
    xp_scr[5:8, :] = hist

    y = _silu(y)
    ss = _group_sum(y[:, 0:512] * y[:, 0:512], bd512_ref[...])
    inv_n = lax.rsqrt(ss + EPS)
    qn = y[:, 0:256] * inv_n[:, 0:256] * (B_DK ** -0.5)
    kn = y[:, 256:512] * inv_n[:, 256:512]
    vv = y[:, 512:768]
    ms = ms_ref[...]
    g_all = -jnp.exp(al_ref[...]) * _softplus(ms + dt_ref[...])
    beta_all = jax.nn.sigmoid(ms)
    if t_valid % c != 0:
        vm = (t * c + lax.broadcasted_iota(I32, (c, 1), 0) < t_valid).astype(F32)
        g_all = g_all * vm
        beta_all = beta_all * vm
        kn = kn * vm
        vv = vv * vm

    r = lax.broadcasted_iota(I32, (c, c), 0)
    q = lax.broadcasted_iota(I32, (c, c), 1)
    incl = r >= q
    strict = r > q
    gcum_all = _mm_sel(incl, g_all)
    lane = lax.broadcasted_iota(I32, (c, LANES), 1)
    left = lane < 64
    ones = jnp.ones((c, LANES), BF16)
    row128 = lax.broadcasted_iota(I32, (LANES, LANES), 0)
    col128 = lax.broadcasted_iota(I32, (LANES, LANES), 1)
    bdmask = (row128 // 64) == (col128 // 64)

    for p in range(B_HEADS // 2):
        kp = kn[:, p * LANES:(p + 1) * LANES]
        qp = qn[:, p * LANES:(p + 1) * LANES]
        vp = vv[:, p * LANES:(p + 1) * LANES]
        s_old = s_scr[p]
        u_pair = jnp.zeros((c, LANES), F32)
        w_pair = jnp.zeros((c, LANES), F32)
        pmats, gcs = [], []
        for e in range(2):
            h = 2 * p + e
            hm = left if e == 0 else jnp.logical_not(left)
            gc = gcum_all[:, M_BA + h:M_BA + h + 1]
            bt = beta_all[:, M_BB + h:M_BB + h + 1]
            grow = _mm_sel(ones, jnp.where(lane == M_BA + h, gcum_all, 0.0), "nt")
            dec = jnp.exp(jnp.where(incl, gc - grow, 0.0))
            km = jnp.where(hm, kp, 0.0)
            kk = _mm1(km, kp, "nt")
            mm = jnp.where(strict, bt * kk * dec, 0.0)
            inv = _unit_lower_inverse(mm, c)
            rhs = jnp.concatenate([bt * jnp.where(hm, vp, 0.0), (bt * jnp.exp(gc)) * km], axis=1)
            uw = _mm3(inv, rhs)
            u_pair = u_pair + uw[:, 0:LANES]
            w_pair = w_pair + uw[:, LANES:2 * LANES]
            qk = _mm1(jnp.where(hm, qp, 0.0), kp, "nt")
            pmats.append(jnp.where(incl, qk * dec, 0.0))
            gcs.append(gc)
        gc_pair = jnp.where(left, gcs[0], gcs[1])
        delta = u_pair - _mm3(w_pair, s_old)
        o = _mm1(qp * jnp.exp(gc_pair), s_old)
        o = o + _mm1(pmats[0], jnp.where(left, delta, 0.0))
        o = o + _mm1(pmats[1], jnp.where(left, 0.0, delta))
        glast = gc_pair[c - 1:c, :]
        kd = kp * jnp.exp(glast - gc_pair)
        gl_rows = jnp.where(row128 < 64, jnp.exp(gcs[0][c - 1:c, :]), jnp.exp(gcs[1][c - 1:c, :]))
        s_scr[p] = gl_rows * s_old + jnp.where(bdmask, _mm3(kd, delta, "tn"), 0.0)
        ms_o = _group_sum(o * o, bd128_ref[...]) * (1.0 / 64)
        o_ref[:, p * LANES:(p + 1) * LANES] = (o * lax.rsqrt(ms_o + EPS) * dn_ref[...]
                                               * _silu(bg_ref[:, p * LANES:(p + 1) * LANES]))

    @pl.when(t == nt - 1)
    def _():
        sout_ref[...] = s_scr[...]


def _delta(z, nb, t_pad, t_valid, buf0, s0_pair, cw, al_row, dt_row, dn_row, bd512, bd128):
    c = CHUNK
    nt = t_pad // c
    m = nb * t_pad
    const = lambda *shape: pl.BlockSpec(shape, lambda b, t: (0,) * len(shape))
    return pl.pallas_call(
        functools.partial(_delta_kernel, c=c, t_valid=t_valid, nt=nt),
        grid=(nb, nt),
        in_specs=[pl.BlockSpec((c, B_CONV), lambda b, t: (b * nt + t, Z_CONV // B_CONV)),
                  pl.BlockSpec((c, 256), lambda b, t: (b * nt + t, Z_BG // 256)),
                  pl.BlockSpec((c, LANES), lambda b, t: (b * nt + t, Z_MISC // LANES)),
                  pl.BlockSpec((None, CONV_W - 1, B_CONV), lambda b, t: (b, 0, 0)),
                  pl.BlockSpec((None, 2, LANES, LANES), lambda b, t: (b, 0, 0, 0)),
                  const(CONV_W, B_CONV), const(1, LANES), const(1, LANES), const(1, LANES),
                  const(512, 512), const(LANES, LANES)],
        out_specs=[pl.BlockSpec((c, 256), lambda b, t: (b * nt + t, 0)),
                   pl.BlockSpec((None, 2, LANES, LANES), lambda b, t: (b, 0, 0, 0)),
                   pl.BlockSpec((None, CONV_W - 1, B_CONV), lambda b, t: (b, 0, 0))],
        out_shape=[jax.ShapeDtypeStruct((m, 256), F32),
                   jax.ShapeDtypeStruct((nb, 2, LANES, LANES), F32),
                   jax.ShapeDtypeStruct((nb, CONV_W - 1, B_CONV), F32)],
        scratch_shapes=[pltpu.VMEM((8 + c, B_CONV), F32), pltpu.VMEM((2, LANES, LANES), F32)],
        compiler_params=_cparams(("arbitrary", "arbitrary")),
        name="delta",
    )(z, z, z, buf0, s0_pair, cw, al_row, dt_row, dn_row, bd512, bd128)


def _hgrn_kernel(x_ref, s0_ref, lb_ref, cn_ref, bd128_ref, o_ref, sout_ref, s_scr, *, c, t_valid, nt):
    t = pl.program_id(1)

    @pl.when(t == 0)
    def _():
        s_scr[...] = s0_ref[...]

    lb = lb_ref[...]
    cf = x_ref[:, 256:512]
    logf = jnp.log(lb + (1.0 - lb) * jax.nn.sigmoid(cf))
    kk = (1.0 - lb) * jax.nn.sigmoid(-cf)
    qq = _silu(x_ref[:, 0:256])
    vv = x_ref[:, 512:768]
    if t_valid % c != 0:
        vm = (t * c + lax.broadcasted_iota(I32, (c, 1), 0) < t_valid).astype(F32)
        logf = logf * vm
        kk = kk * vm

    r = lax.broadcasted_iota(I32, (c, c), 0)
    q = lax.broadcasted_iota(I32, (c, c), 1)
    b = _mm_sel(r >= q, logf)
    lane = lax.broadcasted_iota(I32, (c, LANES), 1)
    left = lane < 64
    row128 = lax.broadcasted_iota(I32, (LANES, LANES), 0)
    col128 = lax.broadcasted_iota(I32, (LANES, LANES), 1)
    bdmask = (row128 // 64) == (col128 // 64)

    amats = [jnp.zeros((c, c), F32) for _ in range(C_HEADS)]
    half = c // 2
    while half >= 1:
        sz = 2 * half
        bnd = (r // sz) * sz + half - 1
        rl = _mm_sel(q == bnd, b)
        qf = qq * jnp.exp(jnp.minimum(b - rl, 0.0))
        kf = kk * jnp.exp(jnp.minimum(rl - b, 0.0))
        lmask = jnp.logical_and(r // sz == q // sz, jnp.logical_and(r % sz >= half, q % sz < half))
        for h in range(C_HEADS):
            p0 = (h // 2) * LANES
            hm = left if h % 2 == 0 else jnp.logical_not(left)
            a = _mm1(jnp.where(hm, qf[:, p0:p0 + LANES], 0.0), kf[:, p0:p0 + LANES], "nt")
            amats[h] = amats[h] + jnp.where(lmask, a, 0.0)
        half //= 2

    eb = jnp.exp(b)
    for p in range(C_HEADS // 2):
        sl = slice(p * LANES, (p + 1) * LANES)
        st_old = s_scr[p]
        qp, kp, vp, bp = qq[:, sl], kk[:, sl], vv[:, sl], b[:, sl]
        o = _mm1(qp * eb[:, sl], st_old, "nt")
        o = o + _mm1(amats[2 * p], jnp.where(left, vp, 0.0))
        o = o + _mm1(amats[2 * p + 1], jnp.where(left, 0.0, vp))
        o = o + _group_sum(qp * kp, bd128_ref[...]) * vp
        blast = bp[c - 1:c, :]
        kd = kp * jnp.exp(blast - bp)
        s_scr[p] = st_old * jnp.exp(blast) + jnp.where(bdmask, _mm3(vp, kd, "tn"), 0.0)
        ms_o = _group_sum(o * o, bd128_ref[...]) * (1.0 / 64)
        o_ref[:, sl] = o * lax.rsqrt(ms_o + EPS) * cn_ref[...]

    @pl.when(t == nt - 1)
    def _():
        sout_ref[...] = s_scr[...]


def _hgrn(z, nb, t_pad, t_valid, s0_pair, lb_row, cn_row, bd128):
    c = CHUNK
    nt = t_pad // c
    m = nb * t_pad
    const = lambda *shape: pl.BlockSpec(shape, lambda b, t: (0,) * len(shape))
    return pl.pallas_call(
        functools.partial(_hgrn_kernel, c=c, t_valid=t_valid, nt=nt),
        grid=(nb, nt),
        in_specs=[pl.BlockSpec((c, 768), lambda b, t: (b * nt + t, Z_HG // 768)),
                  pl.BlockSpec((None, 2, LANES, LANES), lambda b, t: (b, 0, 0, 0)),
                  const(1, 256), const(1, LANES), const(LANES, LANES)],
        out_specs=[pl.BlockSpec((c, 256), lambda b, t: (b * nt + t, 0)),
                   pl.BlockSpec((None, 2, LANES, LANES), lambda b, t: (b, 0, 0, 0))],
        out_shape=[jax.ShapeDtypeStruct((m, 256), F32),
                   jax.ShapeDtypeStruct((nb, 2, LANES, LANES), F32)],
        scratch_shapes=[pltpu.VMEM((2, LANES, LANES), F32)],
        compiler_params=_cparams(("arbitrary", "arbitrary")),
        name="hgrn",
    )(z, s0_pair, lb_row, cn_row, bd128)


def _out_mlp_kernel(x_ref, oa_ref, ob_ref, oc_ref, wo_ref, g2_ref, w1_ref, w2_ref, y_ref, h_scr):
    @pl.when(pl.program_id(1) == 0)
    def _():
        x1 = (x_ref[...]
              + _dot(oa_ref[...].astype(BF16), wo_ref[0:A_WIDTH, :])
              + _dot(ob_ref[...].astype(BF16), wo_ref[A_WIDTH:A_WIDTH + 256, :])
              + _dot(oc_ref[...].astype(BF16), wo_ref[A_WIDTH + 256:A_WIDTH + 512, :]))
        y_ref[...] = x1
        ms = jnp.mean(x1 * x1, axis=-1, keepdims=True)
        h_scr[...] = (x1 * lax.rsqrt(ms + EPS) * g2_ref[...]).astype(BF16)

    u = jnp.maximum(_dot(h_scr[...], w1_ref[...]), 0.0)
    y_ref[...] += _dot((u * u).astype(BF16), w2_ref[...])


def _out_mlp(x, oa, ob, oc, wo, g2, w1, w2):
    m, d = x.shape
    tm = min(512, m)
    tf = 512
    nf = w1.shape[1] // tf
    tok = lambda w: pl.BlockSpec((tm, w), lambda i, j: (i, 0))
    return pl.pallas_call(
        _out_mlp_kernel,
        grid=(m // tm, nf),
        in_specs=[tok(d), tok(A_WIDTH), tok(256), tok(256),
                  pl.BlockSpec((d, d), lambda i, j: (0, 0)),
                  pl.BlockSpec((1, d), lambda i, j: (0, 0)),
                  pl.BlockSpec((d, tf), lambda i, j: (0, j)),
                  pl.BlockSpec((tf, d), lambda i, j: (j, 0))],
        out_specs=tok(d),
        out_shape=jax.ShapeDtypeStruct((m, d), F32),
        scratch_shapes=[pltpu.VMEM((tm, d), BF16)],
        compiler_params=_cparams(("parallel", "arbitrary")),
        name="out_mlp",
    )(x, oa, ob, oc, wo, g2, w1, w2)


def _relayout_w_in(w):
    d = w.shape[0]
    off = np.cumsum([0, 512, 512, 512, 1024, 16, 64, 256, 256, 256, 4, 4, 256, 256, 256, 256])
    (aq, ak, av, iq, iw, ik, bq, bk, bv, ba, bb, bg, cq, cf, ci, end) = [int(o) for o in off]
    zpad = lambda n: jnp.zeros((d, n), w.dtype)
    cols = [w[:, aq:iw],
            w[:, bg:cq],
            w[:, ik:bq], w[:, iw:ik], w[:, ba:bb], w[:, bb:bg], zpad(LANES - 88),
            zpad(Z_CONV - Z_MISC - LANES),
            w[:, bq:ba],
            w[:, cq:end]]
    out = jnp.concatenate(cols, axis=1)
    assert out.shape[1] == NZ
    return out.astype(BF16)


def _rope_tables(pos):
    pos = pos.astype(F32)
    half = HEAD_DIM // 2
    inv = ROPE_THETA ** (-jnp.arange(half, dtype=F32) / half)
    ang = pos[:, None] * inv[None, :]
    cos, sin = jnp.cos(ang), jnp.sin(ang)
    ca = jnp.tile(jnp.concatenate([cos, cos], -1), (1, 2))
    sa = jnp.tile(jnp.concatenate([-sin, sin], -1), (1, 2))
    half_i = IDX_ROPE // 2
    inv_i = ROPE_THETA ** (-jnp.arange(half_i, dtype=F32) / half_i)
    ang_i = pos[:, None] * inv_i[None, :]
    cos_i, sin_i = jnp.cos(ang_i), jnp.sin(ang_i)
    rest = IDX_DIM - IDX_ROPE
    one = jnp.ones((pos.shape[0], rest), F32)
    zero = jnp.zeros((pos.shape[0], rest), F32)
    ci = jnp.tile(jnp.concatenate([cos_i, cos_i, one], -1), (1, 2))
    si = jnp.tile(jnp.concatenate([-sin_i, sin_i, zero], -1), (1, 2))
    return ca, sa, ci, si


def _to_pair(s):
    nb = s.shape[0]
    s = s.reshape(nb, 2, 2, 64, 64)
    z = jnp.zeros_like(s[:, :, 0])
    top = jnp.concatenate([s[:, :, 0], z], axis=-1)
    bot = jnp.concatenate([z, s[:, :, 1]], axis=-1)
    return jnp.concatenate([top, bot], axis=-2)


def _from_pair(p):
    nb = p.shape[0]
    return jnp.stack([p[:, :, :64, :64], p[:, :, 64:, 64:]], axis=2).reshape(nb, 4, 64, 64)


def _lane_row(vals, offset):
    return jnp.zeros((1, LANES), F32).at[0, offset:offset + vals.shape[0]].set(vals.astype(F32))


def _block_diag_ones(n):
    i = np.arange(n) // 64
    return jnp.asarray((i[:, None] == i[None, :]).astype(np.float32)).astype(BF16)


def kernel(x_prompt, x_sample, cache_k, cache_v, cache_idx_k, state_delta, state_delta_conv, state_hgrn,
           page_table, norm1_g, w_in, q_norm_g, k_norm_g, idx_k_ln_g, idx_k_ln_b, conv_w, a_log, dt_bias,
           delta_norm_g, lb_param, hgrn_norm_g, w_out, norm2_g, w_mlp_in, w_mlp_out):
    depth = w_in.shape[0]
    bp, s_len, d = x_prompt.shape
    bd_, t_dec, _ = x_sample.shape
    assert t_dec == 1 and d == D_MODEL
    past = page_table.shape[1] * PAGE_SIZE

    lb_all = jnp.cumsum(jax.nn.softmax(lb_param.astype(F32), axis=0), axis=0)
    lb_all = lb_all - lb_all[0:1]
    tabs_p = _rope_tables(jnp.arange(s_len))
    tabs_s = _rope_tables(past + jnp.arange(t_dec))
    bd512 = _block_diag_ones(512)
    bd128 = _block_diag_ones(LANES)

    xp = x_prompt.reshape(bp * s_len, d)
    xs = x_sample.reshape(bd_ * t_dec, d)
    t_pad_s = CHUNK
    zeros_buf = jnp.zeros((bp, CONV_W - 1, B_CONV), F32)
    zeros_state = jnp.zeros((bp, 2, LANES, LANES), F32)

    outs_p = {k: [] for k in ("k", "v", "ik", "ds", "dc", "hs")}
    outs_s = {k: [] for k in ("k", "v", "ik", "ds", "dc", "hs")}
    for l in range(depth):
        w_in_l = _relayout_w_in(w_in[l])
        g1 = norm1_g[l].reshape(1, d)
        qg = jnp.tile(q_norm_g[l], A_HEADS).reshape(1, A_WIDTH)
        kg = jnp.tile(k_norm_g[l], A_HEADS).reshape(1, A_WIDTH)
        ikg = _lane_row(idx_k_ln_g[l], 0)
        ikb = _lane_row(idx_k_ln_b[l], 0)
        al_row = _lane_row(a_log[l], M_BA)
        dt_row = _lane_row(dt_bias[l], M_BA)
        dn_row = jnp.tile(delta_norm_g[l], 2).reshape(1, LANES).astype(F32)
        cn_row = jnp.tile(hgrn_norm_g[l], 2).reshape(1, LANES).astype(F32)
        lb_row = lb_all[l].reshape(1, 256)
        wo = w_out[l].astype(BF16)
        g2 = norm2_g[l].reshape(1, d)
        w1 = w_mlp_in[l].astype(BF16)
        w2 = w_mlp_out[l].astype(BF16)
        cw = conv_w[l]

        z = _norm_matmul(xp, g1, w_in_l, 768)
        pp = _prep_t(z, tabs_p, bp, s_len, qg, kg, ikg, ikb, bd512)
        oa = _attn_prompt(pp, bp, s_len)
        ob, ds, dc = _delta(z, bp, s_len, s_len, zeros_buf, zeros_state, cw, al_row, dt_row, dn_row, bd512, bd128)
        oc, hs = _hgrn(z, bp, s_len, s_len, zeros_state, lb_row, cn_row, bd128)
        xp = _out_mlp(xp, oa, ob, oc, wo, g2, w1, w2)
        heads_last = lambda a: jnp.transpose(a.reshape(bp, A_HEADS, HEAD_DIM, s_len), (0, 3, 1, 2))
        outs_p["k"].append(heads_last(pp["kt"]))
        outs_p["v"].append(heads_last(pp["vt"]))
        outs_p["ik"].append(jnp.transpose(pp["ikt"], (0, 2, 1)))
        outs_p["ds"].append(_from_pair(ds))
        outs_p["dc"].append(dc)
        outs_p["hs"].append(jnp.swapaxes(_from_pair(hs), -1, -2))

        zs = _norm_matmul(xs, g1, w_in_l, 768)
        ps = _prep(zs, tabs_s, qg, kg, ikg, ikb, bd512)
        oa_s = _attn_sample(ps, cache_k, cache_v, cache_idx_k, page_table, l)
        zs_pad = jnp.pad(zs.reshape(bd_, 1, NZ), ((0, 0), (0, t_pad_s - 1), (0, 0))).reshape(bd_ * t_pad_s, NZ)
        ob_s, ds_s, dc_s = _delta(zs_pad, bd_, t_pad_s, 1, state_delta_conv[l].astype(F32),
                                  _to_pair(state_delta[l].astype(F32)), cw, al_row, dt_row, dn_row, bd512, bd128)
        oc_s, hs_s = _hgrn(zs_pad, bd_, t_pad_s, 1, _to_pair(jnp.swapaxes(state_hgrn[l].astype(F32), -1, -2)),
                           lb_row, cn_row, bd128)
        ob_s = ob_s.reshape(bd_, t_pad_s, 256)[:, 0]
        oc_s = oc_s.reshape(bd_, t_pad_s, 256)[:, 0]
        xs = _out_mlp(xs, oa_s, ob_s, oc_s, wo, g2, w1, w2)
        outs_s["k"].append(ps["kf"].reshape(bd_, t_dec, A_HEADS, HEAD_DIM))
        outs_s["v"].append(ps["vf"].reshape(bd_, t_dec, A_HEADS, HEAD_DIM))
        outs_s["ik"].append(ps["ikf"].reshape(bd_, t_dec, IDX_DIM))
        outs_s["ds"].append(_from_pair(ds_s))
        outs_s["dc"].append(dc_s)
        outs_s["hs"].append(jnp.swapaxes(_from_pair(hs_s), -1, -2))

    st = lambda o, k: jnp.stack(o[k])
    return (xp.reshape(bp, s_len, d), xs.reshape(bd_, t_dec, d),
            st(outs_p, "k"), st(outs_p, "v"), st(outs_p, "ik"), st(outs_p, "ds"), st(outs_p, "dc"), st(outs_p, "hs"),
            st(outs_s, "k"), st(outs_s, "v"), st(outs_s, "ik"), st(outs_s, "ds"), st(outs_s, "dc"), st(outs_s, "hs"))
```

```python
import functools

import numpy as np
import jax
import jax.numpy as jnp
from jax import lax
from jax.experimental import pallas as pl
from jax.experimental.pallas import tpu as pltpu

F32 = jnp.float32
BF16 = jnp.bfloat16
I32 = jnp.int32
HIGHEST = lax.Precision.HIGHEST

D_MODEL = 1024
HEAD_DIM = 64
A_HEADS = 8
A_WIDTH = A_HEADS * HEAD_DIM
IDX_HEADS = 16
IDX_DIM = 64
IDX_ROPE = 32
TOPK_MAX = 256
B_HEADS = 4
B_DK = 64
B_CONV = 768
CONV_W = 4
C_HEADS = 4
D_FF = 4 * D_MODEL
ROPE_THETA = 10000.0
EPS = 1e-6
PAGE_SIZE = 128
CHUNK = 64

LANES = 128
VMEM_LIMIT = 48 * 1024 * 1024

Z_IQ = 3 * A_WIDTH
Z_BG = Z_IQ + IDX_HEADS * IDX_DIM
Z_MISC = Z_BG + 256
Z_CONV = 3072
Z_HG = Z_CONV + B_CONV
NZ = Z_HG + 768
M_IW = 64
M_BA = 80
M_BB = 84

NEG = -1e30
INT_MIN = -2 ** 31
LOG2E = 1.4426950408889634
KSPLIT = 4


def _cparams(sem):
    return pltpu.CompilerParams(dimension_semantics=sem, vmem_limit_bytes=VMEM_LIMIT)


def _dot(a, b, precision=None):
    return jnp.dot(a, b, preferred_element_type=F32, precision=precision)


def _dot_nt(a, b, precision=None):
    return lax.dot_general(a, b, (((1,), (1,)), ((), ())), preferred_element_type=F32, precision=precision)


def _dot_tn(a, b, precision=None):
    return lax.dot_general(a, b, (((0,), (0,)), ((), ())), preferred_element_type=F32, precision=precision)


_DIMS = {"nn": (((1,), (0,)), ((), ())), "nt": (((1,), (1,)), ((), ())), "tn": (((0,), (0,)), ((), ()))}


def _bdot(a, b, form):
    return lax.dot_general(a, b, _DIMS[form], preferred_element_type=F32)


def _split2(x):
    hi = x.astype(BF16)
    return hi, (x - hi.astype(F32)).astype(BF16)


def _split3(x):
    hi = x.astype(BF16)
    r = x - hi.astype(F32)
    mid = r.astype(BF16)
    return hi, mid, (r - mid.astype(F32)).astype(BF16)


def _mm1(a, b, form="nn"):
    return _bdot(a.astype(BF16), b.astype(BF16), form)


def _mm3(a, b, form="nn"):
    ah, al = _split2(a)
    bh, bl = _split2(b)
    return _bdot(ah, bh, form) + (_bdot(ah, bl, form) + _bdot(al, bh, form))


def _mm_sel(a01, b, form="nn"):
    a = a01.astype(BF16)
    h, m, l = _split3(b)
    return _bdot(a, h, form) + (_bdot(a, m, form) + _bdot(a, l, form))


def _group_sum(y, bd):
    hi = y.astype(BF16)
    lo = (y - hi.astype(F32)).astype(BF16)
    return _dot(hi, bd) + _dot(lo, bd)


def _to_key(x):
    bits = lax.bitcast_convert_type(x, I32)
    return jnp.where(bits < 0, bits ^ jnp.int32(0x7FFFFFFF), bits)


def _silu(x):
    return x * jax.nn.sigmoid(x)


def _softplus(x):
    return jnp.maximum(x, 0.0) + jnp.log(1.0 + jnp.exp(-jnp.abs(x)))


def _norm_matmul_kernel(x_ref, g_ref, w_ref, o_ref, h_scr):
    @pl.when(pl.program_id(1) == 0)
    def _():
        x = x_ref[...]
        ms = jnp.mean(x * x, axis=-1, keepdims=True)
        h_scr[...] = (x * lax.rsqrt(ms + EPS) * g_ref[...]).astype(BF16)

    o_ref[...] = _dot(h_scr[...], w_ref[...])


def _norm_matmul(x, g, w, tn):
    m, d = x.shape
    n = w.shape[1]
    tm = min(512, m)
    return pl.pallas_call(
        _norm_matmul_kernel,
        grid=(m // tm, n // tn),
        in_specs=[pl.BlockSpec((tm, d), lambda i, j: (i, 0)),
                  pl.BlockSpec((1, d), lambda i, j: (0, 0)),
                  pl.BlockSpec((d, tn), lambda i, j: (0, j))],
        out_specs=pl.BlockSpec((tm, tn), lambda i, j: (i, j)),
        out_shape=jax.ShapeDtypeStruct((m, n), F32),
        scratch_shapes=[pltpu.VMEM((tm, d), BF16)],
        compiler_params=_cparams(("parallel", "arbitrary")),
        name="norm_matmul",
    )(x, g, w)


def _prep_values(za_ref, zm_ref, ca_ref, sa_ref, ci_ref, si_ref, qg_ref, kg_ref, ikg_ref, ikb_ref, bd_ref):
    tm = za_ref.shape[0]
    bd = bd_ref[...]
    ca = jnp.concatenate([ca_ref[...]] * 4, axis=1)
    sa = jnp.concatenate([sa_ref[...]] * 4, axis=1)
    lane_a = lax.broadcasted_iota(I32, (tm, A_WIDTH), 1)
    lo_half = (lane_a % HEAD_DIM) < (HEAD_DIM // 2)
    lane128 = lax.broadcasted_iota(I32, (tm, LANES), 1)
    left = lane128 < 64

    def norm_rope(x, g):
        ss = _group_sum(x * x, bd)
        y = x * lax.rsqrt(ss * (1.0 / HEAD_DIM) + EPS) * g
        sw = jnp.where(lo_half, pltpu.roll(y, A_WIDTH - 32, 1), pltpu.roll(y, 32, 1))
        return y * ca + sw * sa

    q = norm_rope(za_ref[:, 0:A_WIDTH], qg_ref[...]) * (HEAD_DIM ** -0.5 * LOG2E)
    k = norm_rope(za_ref[:, A_WIDTH:2 * A_WIDTH], kg_ref[...])
    v = za_ref[:, 2 * A_WIDTH:3 * A_WIDTH]

    nq = IDX_HEADS * IDX_DIM
    xi = za_ref[:, Z_IQ:Z_IQ + nq]
    lane_i = lax.broadcasted_iota(I32, (tm, nq), 1)
    ci = jnp.concatenate([ci_ref[...]] * 8, axis=1)
    si = jnp.concatenate([si_ref[...]] * 8, axis=1)
    swi = jnp.where((lane_i % IDX_DIM) < 16, pltpu.roll(xi, nq - 16, 1), pltpu.roll(xi, 16, 1))
    iq = xi * ci + swi * si

    m = zm_ref[...]
    mu = jnp.sum(jnp.where(left, m, 0.0), axis=-1, keepdims=True) * (1.0 / IDX_DIM)
    xc = jnp.where(left, m - mu, 0.0)
    var = jnp.sum(xc * xc, axis=-1, keepdims=True) * (1.0 / IDX_DIM)
    y = xc * lax.rsqrt(var + EPS) * ikg_ref[...] + ikb_ref[...]
    swk = jnp.where(lane128 < 16, pltpu.roll(y, LANES - 16, 1), pltpu.roll(y, 16, 1))
    ikr = y * ci_ref[...] + swk * si_ref[...]
    ikdup = jnp.where(left, ikr, pltpu.roll(ikr, 64, 1)).astype(BF16)
    misc = m * (IDX_HEADS ** -0.5 * IDX_DIM ** -0.5)
    return q, k, v, iq, ikr, ikdup, misc


def _prep_kernel(*refs):
    ins, (qpad_ref, kf_ref, vf_ref, iqpad_ref, ikf_ref, ikdup_ref, misc_ref) = refs[:11], refs[11:]
    q, k, v, iq, ikr, ikdup, misc = _prep_values(*ins)
    left = lax.broadcasted_iota(I32, (q.shape[0], LANES), 1) < 64
    for h in range(A_HEADS):
        keep = left if h % 2 == 0 else jnp.logical_not(left)
        qpad_ref[h] = jnp.where(keep, q[:, (h // 2) * LANES:(h // 2 + 1) * LANES], 0.0).astype(BF16)
    for h in range(IDX_HEADS):
        keep = left if h % 2 == 0 else jnp.logical_not(left)
        iqpad_ref[h] = jnp.where(keep, iq[:, (h // 2) * LANES:(h // 2 + 1) * LANES], 0.0).astype(BF16)
    kf_ref[...] = k
    vf_ref[...] = v
    ikf_ref[...] = ikr[:, 0:IDX_DIM]
    ikdup_ref[...] = ikdup
    misc_ref[...] = misc


def _prep_t_kernel(*refs):
    ins, (qt_ref, kt_ref, kb_ref, vt_ref, vaug_ref, iqt_ref, ikt_ref, ikdup_ref, iwt_ref) = refs[:11], refs[11:]
    q, k, v, iq, ikr, ikdup, misc = _prep_values(*ins)
    tm = q.shape[0]
    top = lax.broadcasted_iota(I32, (LANES, tm), 0) < 64
    qt = q.T
    vt = v.T
    iqt = iq.T
    for h in range(A_HEADS):
        keep = top if h % 2 == 0 else jnp.logical_not(top)
        rows = slice((h // 2) * LANES, (h // 2 + 1) * LANES)
        qt_ref[h] = jnp.where(keep, qt[rows, :], 0.0).astype(BF16)
        vaug_ref[h] = jnp.where(keep, vt[rows, :], 1.0).astype(BF16)
    for h in range(IDX_HEADS):
        keep = top if h % 2 == 0 else jnp.logical_not(top)
        iqt_ref[h] = jnp.where(keep, iqt[(h // 2) * LANES:(h // 2 + 1) * LANES, :], 0.0).astype(BF16)
    kt_ref[...] = k.T
    kb_ref[...] = k.astype(BF16)
    vt_ref[...] = vt
    ikt_ref[...] = ikr.T[0:IDX_DIM, :]
    ikdup_ref[...] = ikdup
    iwt_ref[...] = misc.T[M_IW:M_IW + IDX_HEADS, :]


def _prep_in_specs(tm, tab_spec):
    row = lambda w: pl.BlockSpec((1, w), lambda i: (0, 0))
    return [pl.BlockSpec((tm, Z_BG), lambda i: (i, 0)),
            pl.BlockSpec((tm, LANES), lambda i: (i, Z_MISC // LANES)),
            tab_spec, tab_spec, tab_spec, tab_spec,
            row(A_WIDTH), row(A_WIDTH), row(LANES), row(LANES),
            pl.BlockSpec((A_WIDTH, A_WIDTH), lambda i: (0, 0))]


def _prep(z, tabs, qg, kg, ikg, ikb, bd512):
    m = z.shape[0]
    tm = m
    tab_spec = pl.BlockSpec((1, LANES), lambda i: (0, 0))
    tok = lambda w: pl.BlockSpec((tm, w), lambda i: (i, 0))
    outs = pl.pallas_call(
        _prep_kernel,
        grid=(m // tm,),
        in_specs=_prep_in_specs(tm, tab_spec),
        out_specs=[pl.BlockSpec((A_HEADS, tm, LANES), lambda i: (0, i, 0)),
                   tok(A_WIDTH), tok(A_WIDTH),
                   pl.BlockSpec((IDX_HEADS, tm, LANES), lambda i: (0, i, 0)),
                   tok(IDX_DIM), tok(LANES), tok(LANES)],
        out_shape=[jax.ShapeDtypeStruct((A_HEADS, m, LANES), BF16),
                   jax.ShapeDtypeStruct((m, A_WIDTH), F32),
                   jax.ShapeDtypeStruct((m, A_WIDTH), F32),
                   jax.ShapeDtypeStruct((IDX_HEADS, m, LANES), BF16),
                   jax.ShapeDtypeStruct((m, IDX_DIM), F32),
                   jax.ShapeDtypeStruct((m, LANES), BF16),
                   jax.ShapeDtypeStruct((m, LANES), F32)],
        compiler_params=_cparams(("parallel",)),
        name="prep",
    )(z, z, *tabs, qg, kg, ikg, ikb, bd512)
    names = ("qpad", "kf", "vf", "iqpad", "ikf", "ikdup", "misc")
    return dict(zip(names, outs))


def _prep_t(z, tabs, nb, t_seq, qg, kg, ikg, ikb, bd512):
    m = z.shape[0]
    tm = min(256, t_seq)
    assert t_seq % tm == 0
    nblk = t_seq // tm
    tab_spec = pl.BlockSpec((tm, LANES), lambda i: (i % nblk, 0))
    feat = lambda f: pl.BlockSpec((None, f, tm), lambda i: (i // nblk, 0, i % nblk))
    heads = lambda nh: pl.BlockSpec((nh, LANES, tm), lambda i: (0, 0, i))
    outs = pl.pallas_call(
        _prep_t_kernel,
        grid=(m // tm,),
        in_specs=_prep_in_specs(tm, tab_spec),
        out_specs=[heads(A_HEADS), feat(A_WIDTH),
                   pl.BlockSpec((tm, A_WIDTH), lambda i: (i, 0)),
                   feat(A_WIDTH), heads(A_HEADS), heads(IDX_HEADS), feat(IDX_DIM),
                   pl.BlockSpec((tm, LANES), lambda i: (i, 0)),
                   pl.BlockSpec((IDX_HEADS, tm), lambda i: (0, i))],
        out_shape=[jax.ShapeDtypeStruct((A_HEADS, LANES, m), BF16),
                   jax.ShapeDtypeStruct((nb, A_WIDTH, t_seq), F32),
                   jax.ShapeDtypeStruct((m, A_WIDTH), BF16),
                   jax.ShapeDtypeStruct((nb, A_WIDTH, t_seq), F32),
                   jax.ShapeDtypeStruct((A_HEADS, LANES, m), BF16),
                   jax.ShapeDtypeStruct((IDX_HEADS, LANES, m), BF16),
                   jax.ShapeDtypeStruct((nb, IDX_DIM, t_seq), F32),
                   jax.ShapeDtypeStruct((m, LANES), BF16),
                   jax.ShapeDtypeStruct((IDX_HEADS, m), F32)],
        compiler_params=_cparams(("parallel",)),
        name="prep_t",
    )(z, z, *tabs, qg, kg, ikg, ikb, bd512)
    names = ("qt", "kt", "kb", "vt", "vaug", "iqt", "ikt", "ikdup", "iwt")
    return dict(zip(names, outs))


def _attn_prompt_kernel(qb_tab, kb_tab, nkb_tab, qt_ref, iqt_ref, iwt_ref, ik_ref, k_ref, vaug_ref, o_ref,
                        key_scr, thr_scr, m_scr, acc_scr, iq2_scr, *, tq, tk, topk):
    s = pl.program_id(1)
    qb = qb_tab[s]
    kb = kb_tab[s]
    nkb = nkb_tab[s]
    koff = lax.broadcasted_iota(I32, (tk, tq), 0)
    tpos = qb * tq + lax.broadcasted_iota(I32, (tk, tq), 1)

    @pl.when(kb == 0)
    def _():
        iwt = iwt_ref[...]
        for j in range(IDX_HEADS // 2):
            iq2_scr[j] = jnp.concatenate([iqt_ref[2 * j], iqt_ref[2 * j + 1]], axis=1)

        def score_chunk(c, carry):
            ikc = ik_ref[pl.ds(pl.multiple_of(c * tk, tk), tk), :]
            acc = jnp.zeros((tk, tq), F32)
            for j in range(IDX_HEADS // 2):
                s2 = _dot(ikc, iq2_scr[j])
                acc = acc + jnp.maximum(s2[:, 0:tq], 0.0) * iwt[2 * j:2 * j + 1, :]
                acc = acc + jnp.maximum(s2[:, tq:2 * tq], 0.0) * iwt[2 * j + 1:2 * j + 2, :]
            acc = jnp.where(c * tk + koff <= tpos, acc, -jnp.inf)
            key_scr[c] = _to_key(acc)
            return carry

        lax.fori_loop(0, nkb, score_chunk, 0)

        def count_ge(cand):
            def body(c, cnt):
                ge = (key_scr[c] >= cand).astype(I32)
                return cnt + jnp.sum(ge.reshape(tk // 8, 8, tq), axis=0)

            cnt = lax.fori_loop(0, nkb, body, jnp.zeros((8, tq), I32))
            return jnp.sum(cnt, axis=0, keepdims=True)

        zero = jnp.zeros((1, tq), I32)
        c0 = count_ge(zero)
        thr0 = jnp.where(c0 >= topk, zero, jnp.full((1, tq), INT_MIN, I32))
        n_adm = qb * tq + lax.broadcasted_iota(I32, (1, tq), 1) + 1
        open0 = jnp.logical_and(n_adm >= topk, c0 != topk).astype(I32)

        def cond(st):
            i, _, _, n_open = st
            return jnp.logical_and(i < 31, n_open > 0)

        def bit_step(st):
            i, thr, opn, _ = st
            cand = thr | jnp.left_shift(jnp.int32(1), 30 - i)
            cnt = count_ge(cand)
            thr = jnp.where(cnt >= topk, cand, thr)
            opn = jnp.where(cnt == topk, 0, opn)
            return i + 1, thr, opn, jnp.sum(opn)

        _, thr, _, _ = lax.while_loop(cond, bit_step, (jnp.int32(0), thr0, open0, jnp.sum(open0)))
        thr_scr[...] = thr
        m_scr[...] = jnp.full(m_scr.shape, NEG, F32)
        acc_scr[...] = jnp.zeros(acc_scr.shape, F32)

    sel = jnp.logical_and(key_scr[kb] >= thr_scr[...], kb * tk + koff <= tpos)
    bias = jnp.where(sel, 0.0, NEG)
    ts = tk // KSPLIT
    for u in range(KSPLIT):
        rows = slice(u * ts, (u + 1) * ts)
        for h in range(A_HEADS):
            p0 = (h // 2) * LANES
            sc = _dot(k_ref[rows, p0:p0 + LANES], qt_ref[h]) + bias[rows, :]
            m_old = m_scr[h:h + 1, :]
            m_new = jnp.maximum(m_old, jnp.max(sc, axis=0, keepdims=True))
            alpha = jnp.exp2(m_old - m_new)
            p = jnp.exp2(sc - m_new).astype(BF16)
            acc_scr[h] = alpha * acc_scr[h] + _dot(vaug_ref[h, :, rows], p)
            m_scr[h:h + 1, :] = m_new

    @pl.when(kb == nkb - 1)
    def _():
        for p in range(A_HEADS // 2):
            a = acc_scr[2 * p]
            b = acc_scr[2 * p + 1]
            ot = jnp.concatenate([a[0:64, :] / a[64:65, :], b[64:128, :] / b[0:1, :]], axis=0)
            o_ref[:, p * LANES:(p + 1) * LANES] = ot.T


def _attn_prompt(pp, nb, s_len):
    tq, tk = 256, 512
    tk = min(tk, s_len)
    topk = min(TOPK_MAX, s_len // 4)
    nqb = s_len // tq
    nkblk = s_len // tk
    qb_l, kb_l, nkb_l = [], [], []
    for qb in range(nqb):
        nkb = ((qb + 1) * tq + tk - 1) // tk
        for kb in range(nkb):
            qb_l.append(qb)
            kb_l.append(kb)
            nkb_l.append(nkb)
    tabs = [jnp.asarray(np.array(a, np.int32)) for a in (qb_l, kb_l, nkb_l)]
    nsteps = len(qb_l)
    m = nb * s_len
    grid_spec = pltpu.PrefetchScalarGridSpec(
        num_scalar_prefetch=3,
        grid=(nb, nsteps),
        in_specs=[
            pl.BlockSpec((A_HEADS, LANES, tq), lambda b, s, qt, kt, nt: (0, 0, b * nqb + qt[s])),
            pl.BlockSpec((IDX_HEADS, LANES, tq), lambda b, s, qt, kt, nt: (0, 0, b * nqb + qt[s])),
            pl.BlockSpec((IDX_HEADS, tq), lambda b, s, qt, kt, nt: (0, b * nqb + qt[s])),
            pl.BlockSpec((s_len, LANES), lambda b, s, qt, kt, nt: (b, 0)),
            pl.BlockSpec((tk, A_WIDTH), lambda b, s, qt, kt, nt: (b * nkblk + kt[s], 0)),
            pl.BlockSpec((A_HEADS, LANES, tk), lambda b, s, qt, kt, nt: (0, 0, b * nkblk + kt[s])),
        ],
        out_specs=pl.BlockSpec((tq, A_WIDTH), lambda b, s, qt, kt, nt: (b * nqb + qt[s], 0)),
        scratch_shapes=[pltpu.VMEM((nkblk, tk, tq), I32),
                        pltpu.VMEM((1, tq), I32),
                        pltpu.VMEM((A_HEADS, tq), F32),
                        pltpu.VMEM((A_HEADS, LANES, tq), F32),
                        pltpu.VMEM((IDX_HEADS // 2, LANES, 2 * tq), BF16)],
    )
    return pl.pallas_call(
        functools.partial(_attn_prompt_kernel, tq=tq, tk=tk, topk=topk),
        grid_spec=grid_spec,
        out_shape=jax.ShapeDtypeStruct((m, A_WIDTH), F32),
        compiler_params=_cparams(("arbitrary", "arbitrary")),
        name="attn_prompt",
    )(*tabs, pp["qt"], pp["iqt"], pp["iwt"], pp["ikdup"], pp["kb"], pp["vaug"])


def _sample_score_kernel(pt_ref, iq_ref, iw_ref, *rest, npg):
    page_refs = rest[:npg]
    o_ref = rest[npg]
    iq = iq_ref[...]
    w = iw_ref[...]
    for g in range(npg):
        pg = page_refs[g][...].astype(BF16)
        sc = _dot(iq, pg)
        o_ref[g:g + 1, :] = jnp.sum(jnp.maximum(sc, 0.0) * w, axis=0, keepdims=True)


def _sample_attn_kernel(pt_ref, sc_ref, iq_ref, iw_ref, ikn_ref, qbd_ref, kn_ref, vn_ref, *rest, npg, topk):
    k_refs = rest[:npg]
    v_refs = rest[npg:2 * npg]
    o_ref = rest[2 * npg]
    key_scr, thr_scr, nk_scr, m_scr, l_scr, acc_scr = rest[2 * npg + 1:]
    j = pl.program_id(1)
    nj = pl.num_programs(1)

    @pl.when(j == 0)
    def _():
        keys = _to_key(sc_ref[...])
        key_scr[...] = keys
        sn = jnp.sum(iq_ref[...].astype(F32) * ikn_ref[...].astype(F32), axis=1, keepdims=True)
        snew = jnp.sum(jnp.maximum(sn, 0.0) * iw_ref[...], axis=0, keepdims=True)
        knew = _to_key(snew)
        nk_scr[...] = knew

        def count_ge(cand):
            c = jnp.sum((keys >= cand).astype(I32), axis=1, keepdims=True)
            return jnp.sum(c, axis=0, keepdims=True) + (knew >= cand).astype(I32)

        thr = jnp.full((1, 1), INT_MIN, I32)
        zero = jnp.zeros((1, 1), I32)
        thr = jnp.where(count_ge(zero) >= topk, zero, thr)

        def bit_step(i, thr):
            cand = thr | jnp.left_shift(jnp.int32(1), 30 - i)
            return jnp.where(count_ge(cand) >= topk, cand, thr)

        thr_scr[...] = lax.fori_loop(0, 31, bit_step, thr)
        m_scr[...] = jnp.full(m_scr.shape, NEG, F32)
        l_scr[...] = jnp.zeros(l_scr.shape, F32)
        acc_scr[...] = jnp.zeros(acc_scr.shape, F32)

    qbd = qbd_ref[...]
    thr = thr_scr[...]
    sel = jnp.concatenate([key_scr[pl.ds(j * npg + g, 1), :] >= thr for g in range(npg)], axis=1)
    sc = jnp.concatenate([_dot(qbd, k_refs[g][...].astype(BF16)) for g in range(npg)], axis=1)
    sc = jnp.where(sel, sc, NEG)
    m_old = m_scr[...]
    m_new = jnp.maximum(m_old, jnp.max(sc, axis=1, keepdims=True))
    alpha = jnp.exp2(m_old - m_new)
    p = jnp.where(sel, jnp.exp2(sc - m_new), 0.0)
    l_scr[...] = alpha * l_scr[...] + jnp.sum(p, axis=1, keepdims=True)
    pv = jnp.zeros(acc_scr.shape, F32)
    for g in range(npg):
        pv = pv + _dot_nt(p[:, g * PAGE_SIZE:(g + 1) * PAGE_SIZE].astype(BF16), v_refs[g][...].astype(BF16))
    acc_scr[...] = alpha * acc_scr[...] + pv
    m_scr[...] = m_new

    @pl.when(j == nj - 1)
    def _():
        sel = nk_scr[...] >= thr
        sc = jnp.sum(qbd.astype(F32) * kn_ref[...].astype(BF16).astype(F32), axis=1, keepdims=True)
        sc = jnp.where(sel, sc, NEG)
        m_old = m_scr[...]
        m_new = jnp.maximum(m_old, sc)
        alpha = jnp.exp2(m_old - m_new)
        p = jnp.where(sel, jnp.exp2(sc - m_new), 0.0)
        l_new = alpha * l_scr[...] + p
        acc = alpha * acc_scr[...] + p.astype(BF16).astype(F32) * vn_ref[...].astype(BF16).astype(F32)
        o = acc / l_new
        hrow = lax.broadcasted_iota(I32, (A_HEADS, A_WIDTH), 0)
        hcol = lax.broadcasted_iota(I32, (A_HEADS, A_WIDTH), 1) // HEAD_DIM
        o_ref[...] = jnp.sum(jnp.where(hrow == hcol, o, 0.0), axis=0, keepdims=True)


def _attn_sample(pp, cache_k, cache_v, cache_ik, page_table, layer):
    nb, npages = page_table.shape
    npg = 8
    assert npages % npg == 0
    topk = min(TOPK_MAX, (npages * PAGE_SIZE + 1) // 4)
    pt = page_table.reshape(-1).astype(I32)
    n_pool = cache_k.shape[1]
    ck = jnp.transpose(cache_k, (0, 1, 3, 4, 2)).reshape(cache_k.shape[0], n_pool, A_WIDTH, PAGE_SIZE)
    cv = jnp.transpose(cache_v, (0, 1, 3, 4, 2)).reshape(cache_v.shape[0], n_pool, A_WIDTH, PAGE_SIZE)
    cache_ik = jnp.transpose(cache_ik, (0, 1, 3, 2))

    iq = pp["iqpad"]
    iq = jnp.transpose(iq[:, :, :64] + iq[:, :, 64:], (1, 0, 2))
    iw = pp["misc"][:, M_IW:M_IW + IDX_HEADS].reshape(nb, IDX_HEADS, 1)
    ikn = pp["ikdup"][:, :IDX_DIM].reshape(nb, 1, IDX_DIM)
    qp = jnp.transpose(pp["qpad"], (1, 0, 2))
    qbd = jnp.zeros((nb, A_HEADS, A_HEADS // 2, LANES), BF16)
    for h in range(A_HEADS):
        qbd = qbd.at[:, h, h // 2, :].set(qp[:, h, :])
    qbd = qbd.reshape(nb, A_HEADS, A_WIDTH)
    kn = pp["kf"].reshape(nb, 1, A_WIDTH)
    vn = pp["vf"].reshape(nb, 1, A_WIDTH)

    def page_spec(width, g):
        return pl.BlockSpec((None, None, width, PAGE_SIZE),
                            lambda b, j, ptr: (layer, ptr[b * npages + j * npg + g], 0, 0))

    per_b = lambda *shape: pl.BlockSpec((None,) + shape, lambda b, j, ptr: (b,) + (0,) * len(shape))

    scores = pl.pallas_call(
        functools.partial(_sample_score_kernel, npg=npg),
        grid_spec=pltpu.PrefetchScalarGridSpec(
            num_scalar_prefetch=1,
            grid=(nb, npages // npg),
            in_specs=[per_b(IDX_HEADS, IDX_DIM), per_b(IDX_HEADS, 1)]
                     + [page_spec(IDX_DIM, g) for g in range(npg)],
            out_specs=pl.BlockSpec((None, npg, PAGE_SIZE), lambda b, j, ptr: (b, j, 0)),
        ),
        out_shape=jax.ShapeDtypeStruct((nb, npages, PAGE_SIZE), F32),
        compiler_params=_cparams(("arbitrary", "arbitrary")),
        name="sample_score",
    )(pt, iq, iw, *([cache_ik] * npg))

    out = pl.pallas_call(
        functools.partial(_sample_attn_kernel, npg=npg, topk=topk),
        grid_spec=pltpu.PrefetchScalarGridSpec(
            num_scalar_prefetch=1,
            grid=(nb, npages // npg),
            in_specs=[per_b(npages, PAGE_SIZE), per_b(IDX_HEADS, IDX_DIM), per_b(IDX_HEADS, 1),
                      per_b(1, IDX_DIM), per_b(A_HEADS, A_WIDTH), per_b(1, A_WIDTH), per_b(1, A_WIDTH)]
                     + [page_spec(A_WIDTH, g) for g in range(npg)]
                     + [page_spec(A_WIDTH, g) for g in range(npg)],
            out_specs=pl.BlockSpec((None, 1, A_WIDTH), lambda b, j, ptr: (b, 0, 0)),
            scratch_shapes=[pltpu.VMEM((npages, PAGE_SIZE), I32),
                            pltpu.VMEM((1, 1), I32),
                            pltpu.VMEM((1, 1), I32),
                            pltpu.VMEM((A_HEADS, 1), F32),
                            pltpu.VMEM((A_HEADS, 1), F32),
                            pltpu.VMEM((A_HEADS, A_WIDTH), F32)],
        ),
        out_shape=jax.ShapeDtypeStruct((nb, 1, A_WIDTH), F32),
        compiler_params=_cparams(("arbitrary", "arbitrary")),
        name="sample_attn",
    )(pt, scores, iq, iw, ikn, qbd, kn, vn, *([ck] * npg), *([cv] * npg))
    return out.reshape(nb, A_WIDTH)


def _unit_lower_inverse(mm, c):
    r = lax.broadcasted_iota(I32, (c, c), 0)
    q = lax.broadcasted_iota(I32, (c, c), 1)
    eye = (r == q).astype(F32)
    blk = min(16, c)
    pw = jnp.where(r // blk == q // blk, -mm, 0.0)
    inv = eye + pw
    k = 1
    while 2 * k < blk:
        pw = _mm3(pw, pw)
        inv = inv + _mm3(inv, pw)
        k *= 2
    sz = 2 * blk
    while sz <= c:
        half = sz // 2
        e = jnp.where(jnp.logical_and(r // sz == q // sz, r // half != q // half), mm, 0.0)
        inv = inv - _mm3(inv, _mm3(e, inv))
        sz *= 2
    return inv


def _delta_kernel(xc_ref, bg_ref, ms_ref, buf_ref, s0_ref, cw_ref, al_ref, dt_ref, dn_ref, bd512_ref, bd128_ref,
                  o_ref, sout_ref, nbuf_ref, xp_scr, s_scr, *, c, t_valid, nt):
    t = pl.program_id(1)

    @pl.when(t == 0)
    def _():
        xp_scr[5:8, :] = buf_ref[...]
        s_scr[...] = s0_ref[...]

    xp_scr[8:8 + c, :] = xc_ref[...]
    cw = cw_ref[...]
    y = (cw[0:1] * xp_scr[5:5 + c, :] + cw[1:2] * xp_scr[6:6 + c, :]
         + cw[2:3] * xp_scr[7:7 + c, :] + cw[3:4] * xp_scr[8:8 + c, :])

    tb, lt = (t_valid - 1) // c, (t_valid - 1) % c

    @pl.when(t == tb)
    def _():
        nbuf_ref[...] = xp_scr[8 + lt - 2:8 + lt + 1, :]

    hist = xp_scr[c + 5:c + 8, :]
    xp_scr[5:8, :] = hist

    y = _silu(y)
    ss = _group_sum(y[:, 0:512] * y[:, 0:512], bd512_ref[...])
    inv_n = lax.rsqrt(ss + EPS)
    qn = y[:, 0:256] * inv_n[:, 0:256] * (B_DK ** -0.5)
    kn = y[:, 256:512] * inv_n[:, 256:512]
    vv = y[:, 512:768]
    ms = ms_ref[...]
    g_all = -jnp.exp(al_ref[...]) * _softplus(ms + dt_ref[...])
    beta_all = jax.nn.sigmoid(ms)
    if t_valid % c != 0:
        vm = (t * c + lax.broadcasted_iota(I32, (c, 1), 0) < t_valid).astype(F32)
        g_all = g_all * vm
        beta_all = beta_all * vm
        kn = kn * vm
        vv = vv * vm

    r = lax.broadcasted_iota(I32, (c, c), 0)
    q = lax.broadcasted_iota(I32, (c, c), 1)
    incl = r >= q
    strict = r > q
    gcum_all = _mm_sel(incl, g_all)
    lane = lax.broadcasted_iota(I32, (c, LANES), 1)
    left = lane < 64
    ones = jnp.ones((c, LANES), BF16)
    row128 = lax.broadcasted_iota(I32, (LANES, LANES), 0)
    col128 = lax.broadcasted_iota(I32, (LANES, LANES), 1)
    bdmask = (row128 // 64) == (col128 // 64)

    for p in range(B_HEADS // 2):
        kp = kn[:, p * LANES:(p + 1) * LANES]
        qp = qn[:, p * LANES:(p + 1) * LANES]
        vp = vv[:, p * LANES:(p + 1) * LANES]
        s_old = s_scr[p]
        u_pair = jnp.zeros((c, LANES), F32)
        w_pair = jnp.zeros((c, LANES), F32)
        pmats, gcs = [], []
        for e in range(2):
            h = 2 * p + e
            hm = left if e == 0 else jnp.logical_not(left)
            gc = gcum_all[:, M_BA + h:M_BA + h + 1]
            bt = beta_all[:, M_BB + h:M_BB + h + 1]
            grow = _mm_sel(ones, jnp.where(lane == M_BA + h, gcum_all, 0.0), "nt")
            dec = jnp.exp(jnp.where(incl, gc - grow, 0.0))
            km = jnp.where(hm, kp, 0.0)
            kk = _mm1(km, kp, "nt")
            mm = jnp.where(strict, bt * kk * dec, 0.0)
            inv = _unit_lower_inverse(mm, c)
            rhs = jnp.concatenate([bt * jnp.where(hm, vp, 0.0), (bt * jnp.exp(gc)) * km], axis=1)
            uw = _mm3(inv, rhs)
            u_pair = u_pair + uw[:, 0:LANES]
            w_pair = w_pair + uw[:, LANES:2 * LANES]
            qk = _mm1(jnp.where(hm, qp, 0.0), kp, "nt")
            pmats.append(jnp.where(incl, qk * dec, 0.0))
            gcs.append(gc)
        gc_pair = jnp.where(left, gcs[0], gcs[1])
        delta = u_pair - _mm3(w_pair, s_old)
        o = _mm1(qp * jnp.exp(gc_pair), s_old)
        o = o + _mm1(pmats[0], jnp.where(left, delta, 0.0))
        o = o + _mm1(pmats[1], jnp.where(left, 0.0, delta))
        glast = gc_pair[c - 1:c, :]
        kd = kp * jnp.exp(glast - gc_pair)
        gl_rows = jnp.where(row128 < 64, jnp.exp(gcs[0][c - 1:c, :]), jnp.exp(gcs[1][c - 1:c, :]))
        s_scr[p] = gl_rows * s_old + jnp.where(bdmask, _mm3(kd, delta, "tn"), 0.0)
        ms_o = _group_sum(o * o, bd128_ref[...]) * (1.0 / 64)
        o_ref[:, p * LANES:(p + 1) * LANES] = (o * lax.rsqrt(ms_o + EPS) * dn_ref[...]
                                               * _silu(bg_ref[:, p * LANES:(p + 1) * LANES]))

    @pl.when(t == nt - 1)
    def _():
        sout_ref[...] = s_scr[...]


def _delta(z, nb, t_pad, t_valid, buf0, s0_pair, cw, al_row, dt_row, dn_row, bd512, bd128):
    c = CHUNK
    nt = t_pad // c
    m = nb * t_pad
    const = lambda *shape: pl.BlockSpec(shape, lambda b, t: (0,) * len(shape))
    return pl.pallas_call(
        functools.partial(_delta_kernel, c=c, t_valid=t_valid, nt=nt),
        grid=(nb, nt),
        in_specs=[pl.BlockSpec((c, B_CONV), lambda b, t: (b * nt + t, Z_CONV // B_CONV)),
                  pl.BlockSpec((c, 256), lambda b, t: (b * nt + t, Z_BG // 256)),
                  pl.BlockSpec((c, LANES), lambda b, t: (b * nt + t, Z_MISC // LANES)),
                  pl.BlockSpec((None, CONV_W - 1, B_CONV), lambda b, t: (b, 0, 0)),
                  pl.BlockSpec((None, 2, LANES, LANES), lambda b, t: (b, 0, 0, 0)),
                  const(CONV_W, B_CONV), const(1, LANES), const(1, LANES), const(1, LANES),
                  const(512, 512), const(LANES, LANES)],
        out_specs=[pl.BlockSpec((c, 256), lambda b, t: (b * nt + t, 0)),
                   pl.BlockSpec((None, 2, LANES, LANES), lambda b, t: (b, 0, 0, 0)),
                   pl.BlockSpec((None, CONV_W - 1, B_CONV), lambda b, t: (b, 0, 0))],
        out_shape=[jax.ShapeDtypeStruct((m, 256), F32),
                   jax.ShapeDtypeStruct((nb, 2, LANES, LANES), F32),
                   jax.ShapeDtypeStruct((nb, CONV_W - 1, B_CONV), F32)],
        scratch_shapes=[pltpu.VMEM((8 + c, B_CONV), F32), pltpu.VMEM((2, LANES, LANES), F32)],
        compiler_params=_cparams(("arbitrary", "arbitrary")),
        name="delta",
    )(z, z, z, buf0, s0_pair, cw, al_row, dt_row, dn_row, bd512, bd128)


def _hgrn_kernel(x_ref, s0_ref, lb_ref, cn_ref, bd128_ref, o_ref, sout_ref, s_scr, *, c, t_valid, nt):
    t = pl.program_id(1)

    @pl.when(t == 0)
    def _():
        s_scr[...] = s0_ref[...]

    lb = lb_ref[...]
    cf = x_ref[:, 256:512]
    logf = jnp.log(lb + (1.0 - lb) * jax.nn.sigmoid(cf))
    kk = (1.0 - lb) * jax.nn.sigmoid(-cf)
    qq = _silu(x_ref[:, 0:256])
    vv = x_ref[:, 512:768]
    if t_valid % c != 0:
        vm = (t * c + lax.broadcasted_iota(I32, (c, 1), 0) < t_valid).astype(F32)
        logf = logf * vm
        kk = kk * vm

    r = lax.broadcasted_iota(I32, (c, c), 0)
    q = lax.broadcasted_iota(I32, (c, c), 1)
    b = _mm_sel(r >= q, logf)
    lane = lax.broadcasted_iota(I32, (c, LANES), 1)
    left = lane < 64
    row128 = lax.broadcasted_iota(I32, (LANES, LANES), 0)
    col128 = lax.broadcasted_iota(I32, (LANES, LANES), 1)
    bdmask = (row128 // 64) == (col128 // 64)

    amats = [jnp.zeros((c, c), F32) for _ in range(C_HEADS)]
    half = c // 2
    while half >= 1:
        sz = 2 * half
        bnd = (r // sz) * sz + half - 1
        rl = _mm_sel(q == bnd, b)
        qf = qq * jnp.exp(jnp.minimum(b - rl, 0.0))
        kf = kk * jnp.exp(jnp.minimum(rl - b, 0.0))
        lmask = jnp.logical_and(r // sz == q // sz, jnp.logical_and(r % sz >= half, q % sz < half))
        for h in range(C_HEADS):
            p0 = (h // 2) * LANES
            hm = left if h % 2 == 0 else jnp.logical_not(left)
            a = _mm1(jnp.where(hm, qf[:, p0:p0 + LANES], 0.0), kf[:, p0:p0 + LANES], "nt")
            amats[h] = amats[h] + jnp.where(lmask, a, 0.0)
        half //= 2

    eb = jnp.exp(b)
    for p in range(C_HEADS // 2):
        sl = slice(p * LANES, (p + 1) * LANES)
        st_old = s_scr[p]
        qp, kp, vp, bp = qq[:, sl], kk[:, sl], vv[:, sl], b[:, sl]
        o = _mm1(qp * eb[:, sl], st_old, "nt")
        o = o + _mm1(amats[2 * p], jnp.where(left, vp, 0.0))
        o = o + _mm1(amats[2 * p + 1], jnp.where(left, 0.0, vp))
        o = o + _group_sum(qp * kp, bd128_ref[...]) * vp
        blast = bp[c - 1:c, :]
        kd = kp * jnp.exp(blast - bp)
        s_scr[p] = st_old * jnp.exp(blast) + jnp.where(bdmask, _mm3(vp, kd, "tn"), 0.0)
        ms_o = _group_sum(o * o, bd128_ref[...]) * (1.0 / 64)
        o_ref[:, sl] = o * lax.rsqrt(ms_o + EPS) * cn_ref[...]

    @pl.when(t == nt - 1)
    def _():
        sout_ref[...] = s_scr[...]


def _hgrn(z, nb, t_pad, t_valid, s0_pair, lb_row, cn_row, bd128):
    c = CHUNK
    nt = t_pad // c
    m = nb * t_pad
    const = lambda *shape: pl.BlockSpec(shape, lambda b, t: (0,) * len(shape))
    return pl.pallas_call(
        functools.partial(_hgrn_kernel, c=c, t_valid=t_valid, nt=nt),
        grid=(nb, nt),
        in_specs=[pl.BlockSpec((c, 768), lambda b, t: (b * nt + t, Z_HG // 768)),
                  pl.BlockSpec((None, 2, LANES, LANES), lambda b, t: (b, 0, 0, 0)),
                  const(1, 256), const(1, LANES), const(LANES, LANES)],
        out_specs=[pl.BlockSpec((c, 256), lambda b, t: (b * nt + t, 0)),
                   pl.BlockSpec((None, 2, LANES, LANES), lambda b, t: (b, 0, 0, 0))],
        out_shape=[jax.ShapeDtypeStruct((m, 256), F32),
                   jax.ShapeDtypeStruct((nb, 2, LANES, LANES), F32)],
        scratch_shapes=[pltpu.VMEM((2, LANES, LANES), F32)],
        compiler_params=_cparams(("arbitrary", "arbitrary")),
        name="hgrn",
    )(z, s0_pair, lb_row, cn_row, bd128)


NH = 4
HW = NH * 64


def _bd(y):
    hb = lax.broadcasted_iota(I32, y.shape, 1) // 64
    zero = jnp.zeros_like(y)
    return jnp.concatenate([jnp.where(hb == h, y, zero) for h in range(NH)], axis=0)


def _st1(x, y, form="nn"):
    return _bdot(x.astype(BF16), _bd(y.astype(BF16)), form)


def _st3(x, y):
    xh, xl = _split2(x)
    yh, yl = _split2(y)
    bh, bl = _bd(yh), _bd(yl)
    return _bdot(xh, bh, "nn") + (_bdot(xh, bl, "nn") + _bdot(xl, bh, "nn"))


def _st_unit_lower_inverse(mm, r, q):
    pw = jnp.where(r // 16 == q // 16, -mm, 0.0)
    inv = (r == q).astype(F32) + pw
    k = 1
    while 2 * k < 16:
        pw = _st3(pw, pw)
        inv = inv + _st3(inv, pw)
        k *= 2
    for sz in (32, 64):
        half = sz // 2
        e = jnp.where(jnp.logical_and(r // sz == q // sz, r // half != q // half), mm, 0.0)
        inv = inv - _st3(inv, _st3(e, inv))
    return inv


def _delta_st_kernel(xc_ref, bg_ref, ms_ref, buf_ref, s0_ref, cw_ref, al_ref, dt_ref, dn_ref, bd512_ref,
                     o_ref, sout_ref, nbuf_ref, xp_scr, s_scr, *, c, t_valid, nt):
    t = pl.program_id(1)

    @pl.when(t == 0)
    def _():
        xp_scr[5:8, :] = buf_ref[...]
        s_scr[...] = s0_ref[...]

    xp_scr[8:8 + c, :] = xc_ref[...]
    cw = cw_ref[...]
    y = (cw[0:1] * xp_scr[5:5 + c, :] + cw[1:2] * xp_scr[6:6 + c, :]
         + cw[2:3] * xp_scr[7:7 + c, :] + cw[3:4] * xp_scr[8:8 + c, :])

    tb, lt = (t_valid - 1) // c, (t_valid - 1) % c

    @pl.when(t == tb)
    def _():
        nbuf_ref[...] = xp_scr[8 + lt - 2:8 + lt + 1, :]

    hist = xp_scr[c + 5:c + 8, :]
    xp_scr[5:8, :] = hist

    y = _silu(y)
    ss = _group_sum(y[:, 0:512] * y[:, 0:512], bd512_ref[...])
    inv_n = lax.rsqrt(ss + EPS)
    qn = y[:, 0:HW] * inv_n[:, 0:HW] * (B_DK ** -0.5)
    kn = y[:, HW:2 * HW] * inv_n[:, HW:2 * HW]
    vv = y[:, 2 * HW:3 * HW]
    ms = ms_ref[...]
    g_all = -jnp.exp(al_ref[...]) * _softplus(ms + dt_ref[...])
    beta_all = jax.nn.sigmoid(ms)
    if t_valid % c != 0:
        vm = (t * c + lax.broadcasted_iota(I32, (c, 1), 0) < t_valid).astype(F32)
        g_all = g_all * vm
        beta_all = beta_all * vm
        kn = kn * vm
        vv = vv * vm

    r = lax.broadcasted_iota(I32, (c, HW), 0)
    lane = lax.broadcasted_iota(I32, (c, HW), 1)
    q = lane % 64
    hb = lane // 64
    incl = r >= q
    strict = r > q
    r64 = lax.broadcasted_iota(I32, (c, c), 0)
    q64 = lax.broadcasted_iota(I32, (c, c), 1)
    gcum_all = _mm_sel(r64 >= q64, g_all)
    lane128 = lax.broadcasted_iota(I32, (c, LANES), 1)
    gc = jnp.zeros((c, HW), F32)
    bt = jnp.zeros((c, HW), F32)
    for h in range(NH):
        gc = jnp.where(hb == h, gcum_all[:, M_BA + h:M_BA + h + 1], gc)
        bt = jnp.where(hb == h, beta_all[:, M_BB + h:M_BB + h + 1], bt)
    gsel = jnp.concatenate([jnp.where(lane128 == M_BA + h, gcum_all, 0.0) for h in range(NH)], axis=0)
    grow = _mm_sel(jnp.ones((c, LANES), BF16), gsel, "nt")
    dec = jnp.exp(jnp.where(incl, gc - grow, 0.0))
    kbd = _bd(kn.astype(BF16))
    kk = _bdot(kn.astype(BF16), kbd, "nt")
    qk = _bdot(qn.astype(BF16), kbd, "nt")
    mm = jnp.where(strict, bt * kk * dec, 0.0)
    inv = _st_unit_lower_inverse(mm, r, q)
    egc = jnp.exp(gc)
    uu = _st3(inv, bt * vv)
    ww = _st3(inv, (bt * egc) * kn)
    s_old = s_scr[...]
    delta = uu - _mm3(ww, s_old)
    o = _mm1(qn * egc, s_old) + _st1(jnp.where(incl, qk * dec, 0.0), delta)
    glast = gc[c - 1:c, :]
    kd = kn * jnp.exp(glast - gc)
    rb = lax.broadcasted_iota(I32, (HW, HW), 0) // 64
    cb = lax.broadcasted_iota(I32, (HW, HW), 1) // 64
    gl = jnp.zeros((HW, HW), F32)
    for h in range(NH):
        gl = jnp.where(rb == h, jnp.exp(gcum_all[c - 1:c, M_BA + h:M_BA + h + 1]), gl)
    s_scr[...] = gl * s_old + jnp.where(rb == cb, _mm3(kd, delta, "tn"), 0.0)
    ms_o = _group_sum(o * o, bd512_ref[0:HW, 0:HW]) * (1.0 / 64)
    o_ref[...] = o * lax.rsqrt(ms_o + EPS) * dn_ref[...] * _silu(bg_ref[...])

    @pl.when(t == nt - 1)
    def _():
        sout_ref[...] = s_scr[...]


def _delta_st(z, nb, t_pad, t_valid, buf0, s0_bd, cw, al_row, dt_row, dn_row, bd512):
    c = CHUNK
    assert c == 64
    nt = t_pad // c
    m = nb * t_pad
    const = lambda *shape: pl.BlockSpec(shape, lambda b, t: (0,) * len(shape))
    return pl.pallas_call(
        functools.partial(_delta_st_kernel, c=c, t_valid=t_valid, nt=nt),
        grid=(nb, nt),
        in_specs=[pl.BlockSpec((c, B_CONV), lambda b, t: (b * nt + t, Z_CONV // B_CONV)),
                  pl.BlockSpec((c, HW), lambda b, t: (b * nt + t, Z_BG // HW)),
                  pl.BlockSpec((c, LANES), lambda b, t: (b * nt + t, Z_MISC // LANES)),
                  pl.BlockSpec((None, CONV_W - 1, B_CONV), lambda b, t: (b, 0, 0)),
                  pl.BlockSpec((None, HW, HW), lambda b, t: (b, 0, 0)),
                  const(CONV_W, B_CONV), const(1, LANES), const(1, LANES), const(1, HW),
                  const(512, 512)],
        out_specs=[pl.BlockSpec((c, HW), lambda b, t: (b * nt + t, 0)),
                   pl.BlockSpec((None, HW, HW), lambda b, t: (b, 0, 0)),
                   pl.BlockSpec((None, CONV_W - 1, B_CONV), lambda b, t: (b, 0, 0))],
        out_shape=[jax.ShapeDtypeStruct((m, HW), F32),
                   jax.ShapeDtypeStruct((nb, HW, HW), F32),
                   jax.ShapeDtypeStruct((nb, CONV_W - 1, B_CONV), F32)],
        scratch_shapes=[pltpu.VMEM((8 + c, B_CONV), F32), pltpu.VMEM((HW, HW), F32)],
        compiler_params=_cparams(("arbitrary", "arbitrary")),
        name="delta",
    )(z, z, z, buf0, s0_bd, cw, al_row, dt_row, dn_row, bd512)


def _hgrn_st_kernel(x_ref, s0_ref, lb_ref, cn_ref, bd512_ref, o_ref, sout_ref, s_scr, *, c, t_valid, nt):
    t = pl.program_id(1)

    @pl.when(t == 0)
    def _():
        s_scr[...] = s0_ref[...]

    lb = lb_ref[...]
    cf = x_ref[:, HW:2 * HW]
    logf = jnp.log(lb + (1.0 - lb) * jax.nn.sigmoid(cf))
    kk = (1.0 - lb) * jax.nn.sigmoid(-cf)
    qq = _silu(x_ref[:, 0:HW])
    vv = x_ref[:, 2 * HW:3 * HW]
    if t_valid % c != 0:
        vm = (t * c + lax.broadcasted_iota(I32, (c, 1), 0) < t_valid).astype(F32)
        logf = logf * vm
        kk = kk * vm

    r64 = lax.broadcasted_iota(I32, (c, c), 0)
    q64 = lax.broadcasted_iota(I32, (c, c), 1)
    b = _mm_sel(r64 >= q64, logf)
    r = lax.broadcasted_iota(I32, (c, HW), 0)
    q = lax.broadcasted_iota(I32, (c, HW), 1) % 64

    amat = jnp.zeros((c, HW), F32)
    half = c // 2
    while half >= 1:
        sz = 2 * half
        bnd = (r64 // sz) * sz + half - 1
        rl = _mm_sel(q64 == bnd, b)
        qf = qq * jnp.exp(jnp.minimum(b - rl, 0.0))
        kf = kk * jnp.exp(jnp.minimum(rl - b, 0.0))
        lmask = jnp.logical_and(r // sz == q // sz, jnp.logical_and(r % sz >= half, q % sz < half))
        amat = amat + jnp.where(lmask, _st1(qf, kf, "nt"), 0.0)
        half //= 2

    st_old = s_scr[...]
    o = _mm1(qq * jnp.exp(b), st_old, "nt") + _st1(amat, vv)
    o = o + _group_sum(qq * kk, bd512_ref[0:HW, 0:HW]) * vv
    blast = b[c - 1:c, :]
    kd = kk * jnp.exp(blast - b)
    rb = lax.broadcasted_iota(I32, (HW, HW), 0) // 64
    cb = lax.broadcasted_iota(I32, (HW, HW), 1) // 64
    s_scr[...] = st_old * jnp.exp(blast) + jnp.where(rb == cb, _mm3(vv, kd, "tn"), 0.0)
    ms_o = _group_sum(o * o, bd512_ref[0:HW, 0:HW]) * (1.0 / 64)
    o_ref[...] = o * lax.rsqrt(ms_o + EPS) * cn_ref[...]

    @pl.when(t == nt - 1)
    def _():
        sout_ref[...] = s_scr[...]


def _hgrn_st(z, nb, t_pad, t_valid, s0_bd, lb_row, cn_row, bd512):
    c = CHUNK
    assert c == 64
    nt = t_pad // c
    m = nb * t_pad
    const = lambda *shape: pl.BlockSpec(shape, lambda b, t: (0,) * len(shape))
    return pl.pallas_call(
        functools.partial(_hgrn_st_kernel, c=c, t_valid=t_valid, nt=nt),
        grid=(nb, nt),
        in_specs=[pl.BlockSpec((c, 768), lambda b, t: (b * nt + t, Z_HG // 768)),
                  pl.BlockSpec((None, HW, HW), lambda b, t: (b, 0, 0)),
                  const(1, HW), const(1, HW), const(512, 512)],
        out_specs=[pl.BlockSpec((c, HW), lambda b, t: (b * nt + t, 0)),
                   pl.BlockSpec((None, HW, HW), lambda b, t: (b, 0, 0))],
        out_shape=[jax.ShapeDtypeStruct((m, HW), F32),
                   jax.ShapeDtypeStruct((nb, HW, HW), F32)],
        scratch_shapes=[pltpu.VMEM((HW, HW), F32)],
        compiler_params=_cparams(("arbitrary", "arbitrary")),
        name="hgrn",
    )(z, s0_bd, lb_row, cn_row, bd512)


def _out_mlp_kernel(x_ref, oa_ref, ob_ref, oc_ref, wo_ref, g2_ref, w1_ref, w2_ref, y_ref, h_scr):
    @pl.when(pl.program_id(1) == 0)
    def _():
        x1 = (x_ref[...]
              + _dot(oa_ref[...].astype(BF16), wo_ref[0:A_WIDTH, :])
              + _dot(ob_ref[...].astype(BF16), wo_ref[A_WIDTH:A_WIDTH + 256, :])
              + _dot(oc_ref[...].astype(BF16), wo_ref[A_WIDTH + 256:A_WIDTH + 512, :]))
        y_ref[...] = x1
        ms = jnp.mean(x1 * x1, axis=-1, keepdims=True)
        h_scr[...] = (x1 * lax.rsqrt(ms + EPS) * g2_ref[...]).astype(BF16)

    u = jnp.maximum(_dot(h_scr[...], w1_ref[...]), 0.0)
    y_ref[...] += _dot((u * u).astype(BF16), w2_ref[...])


def _out_mlp(x, oa, ob, oc, wo, g2, w1, w2):
    m, d = x.shape
    tm = min(512, m)
    tf = 512
    nf = w1.shape[1] // tf
    tok = lambda w: pl.BlockSpec((tm, w), lambda i, j: (i, 0))
    return pl.pallas_call(
        _out_mlp_kernel,
        grid=(m // tm, nf),
        in_specs=[tok(d), tok(A_WIDTH), tok(256), tok(256),
                  pl.BlockSpec((d, d), lambda i, j: (0, 0)),
                  pl.BlockSpec((1, d), lambda i, j: (0, 0)),
                  pl.BlockSpec((d, tf), lambda i, j: (0, j)),
                  pl.BlockSpec((tf, d), lambda i, j: (j, 0))],
        out_specs=tok(d),
        out_shape=jax.ShapeDtypeStruct((m, d), F32),
        scratch_shapes=[pltpu.VMEM((tm, d), BF16)],
        compiler_params=_cparams(("parallel", "arbitrary")),
        name="out_mlp",
    )(x, oa, ob, oc, wo, g2, w1, w2)


def _relayout_w_in(w):
    d = w.shape[0]
    off = np.cumsum([0, 512, 512, 512, 1024, 16, 64, 256, 256, 256, 4, 4, 256, 256, 256, 256])
    (aq, ak, av, iq, iw, ik, bq, bk, bv, ba, bb, bg, cq, cf, ci, end) = [int(o) for o in off]
    zpad = lambda n: jnp.zeros((d, n), w.dtype)
    cols = [w[:, aq:iw],
            w[:, bg:cq],
            w[:, ik:bq], w[:, iw:ik], w[:, ba:bb], w[:, bb:bg], zpad(LANES - 88),
            zpad(Z_CONV - Z_MISC - LANES),
            w[:, bq:ba],
            w[:, cq:end]]
    out = jnp.concatenate(cols, axis=1)
    assert out.shape[1] == NZ
    return out.astype(BF16)


def _rope_tables(pos):
    pos = pos.astype(F32)
    half = HEAD_DIM // 2
    inv = ROPE_THETA ** (-jnp.arange(half, dtype=F32) / half)
    ang = pos[:, None] * inv[None, :]
    cos, sin = jnp.cos(ang), jnp.sin(ang)
    ca = jnp.tile(jnp.concatenate([cos, cos], -1), (1, 2))
    sa = jnp.tile(jnp.concatenate([-sin, sin], -1), (1, 2))
    half_i = IDX_ROPE // 2
    inv_i = ROPE_THETA ** (-jnp.arange(half_i, dtype=F32) / half_i)
    ang_i = pos[:, None] * inv_i[None, :]
    cos_i, sin_i = jnp.cos(ang_i), jnp.sin(ang_i)
    rest = IDX_DIM - IDX_ROPE
    one = jnp.ones((pos.shape[0], rest), F32)
    zero = jnp.zeros((pos.shape[0], rest), F32)
    ci = jnp.tile(jnp.concatenate([cos_i, cos_i, one], -1), (1, 2))
    si = jnp.tile(jnp.concatenate([-sin_i, sin_i, zero], -1), (1, 2))
    return ca, sa, ci, si


def _to_bd4(s):
    z = jnp.zeros_like(s[:, 0])
    rows = [jnp.concatenate([s[:, h] if g == h else z for g in range(NH)], axis=-1) for h in range(NH)]
    return jnp.concatenate(rows, axis=-2)


def _from_bd4(p):
    return jnp.stack([p[:, 64 * h:64 * (h + 1), 64 * h:64 * (h + 1)] for h in range(NH)], axis=1)


def _lane_row(vals, offset):
    return jnp.zeros((1, LANES), F32).at[0, offset:offset + vals.shape[0]].set(vals.astype(F32))


def _block_diag_ones(n):
    i = np.arange(n) // 64
    return jnp.asarray((i[:, None] == i[None, :]).astype(np.float32)).astype(BF16)


def kernel(x_prompt, x_sample, cache_k, cache_v, cache_idx_k, state_delta, state_delta_conv, state_hgrn,
           page_table, norm1_g, w_in, q_norm_g, k_norm_g, idx_k_ln_g, idx_k_ln_b, conv_w, a_log, dt_bias,
           delta_norm_g, lb_param, hgrn_norm_g, w_out, norm2_g, w_mlp_in, w_mlp_out):
    depth = w_in.shape[0]
    bp, s_len, d = x_prompt.shape
    bd_, t_dec, _ = x_sample.shape
    assert t_dec == 1 and d == D_MODEL
    past = page_table.shape[1] * PAGE_SIZE

    lb_all = jnp.cumsum(jax.nn.softmax(lb_param.astype(F32), axis=0), axis=0)
    lb_all = lb_all - lb_all[0:1]
    tabs_p = _rope_tables(jnp.arange(s_len))
    tabs_s = _rope_tables(past + jnp.arange(t_dec))
    bd512 = _block_diag_ones(512)
    bd128 = _block_diag_ones(LANES)

    xp = x_prompt.reshape(bp * s_len, d)
    xs = x_sample.reshape(bd_ * t_dec, d)
    t_pad_s = CHUNK
    zeros_buf = jnp.zeros((bp, CONV_W - 1, B_CONV), F32)
    zeros_state = jnp.zeros((bp, HW, HW), F32)

    outs_p = {k: [] for k in ("k", "v", "ik", "ds", "dc", "hs")}
    outs_s = {k: [] for k in ("k", "v", "ik", "ds", "dc", "hs")}
    for l in range(depth):
        w_in_l = _relayout_w_in(w_in[l])
        g1 = norm1_g[l].reshape(1, d)
        qg = jnp.tile(q_norm_g[l], A_HEADS).reshape(1, A_WIDTH)
        kg = jnp.tile(k_norm_g[l], A_HEADS).reshape(1, A_WIDTH)
        ikg = _lane_row(idx_k_ln_g[l], 0)
        ikb = _lane_row(idx_k_ln_b[l], 0)
        al_row = _lane_row(a_log[l], M_BA)
        dt_row = _lane_row(dt_bias[l], M_BA)
        dn_row = jnp.tile(delta_norm_g[l], NH).reshape(1, HW).astype(F32)
        cn_row = jnp.tile(hgrn_norm_g[l], NH).reshape(1, HW).astype(F32)
        lb_row = lb_all[l].reshape(1, 256)
        wo = w_out[l].astype(BF16)
        g2 = norm2_g[l].reshape(1, d)
        w1 = w_mlp_in[l].astype(BF16)
        w2 = w_mlp_out[l].astype(BF16)
        cw = conv_w[l]

        z = _norm_matmul(xp, g1, w_in_l, 768)
        pp = _prep_t(z, tabs_p, bp, s_len, qg, kg, ikg, ikb, bd512)
        oa = _attn_prompt(pp, bp, s_len)
        ob, ds, dc = _delta_st(z, bp, s_len, s_len, zeros_buf, zeros_state, cw, al_row, dt_row, dn_row, bd512)
        oc, hs = _hgrn_st(z, bp, s_len, s_len, zeros_state, lb_row, cn_row, bd512)
        xp = _out_mlp(xp, oa, ob, oc, wo, g2, w1, w2)
        heads_last = lambda a: jnp.transpose(a.reshape(bp, A_HEADS, HEAD_DIM, s_len), (0, 3, 1, 2))
        outs_p["k"].append(heads_last(pp["kt"]))
        outs_p["v"].append(heads_last(pp["vt"]))
        outs_p["ik"].append(jnp.transpose(pp["ikt"], (0, 2, 1)))
        outs_p["ds"].append(_from_bd4(ds))
        outs_p["dc"].append(dc)
        outs_p["hs"].append(jnp.swapaxes(_from_bd4(hs), -1, -2))

        zs = _norm_matmul(xs, g1, w_in_l, 768)
        ps = _prep(zs, tabs_s, qg, kg, ikg, ikb, bd512)
        oa_s = _attn_sample(ps, cache_k, cache_v, cache_idx_k, page_table, l)
        zs_pad = jnp.pad(zs.reshape(bd_, 1, NZ), ((0, 0), (0, t_pad_s - 1), (0, 0))).reshape(bd_ * t_pad_s, NZ)
        ob_s, ds_s, dc_s = _delta_st(zs_pad, bd_, t_pad_s, 1, state_delta_conv[l].astype(F32),
                                     _to_bd4(state_delta[l].astype(F32)), cw, al_row, dt_row, dn_row, bd512)
        oc_s, hs_s = _hgrn_st(zs_pad, bd_, t_pad_s, 1, _to_bd4(jnp.swapaxes(state_hgrn[l].astype(F32), -1, -2)),
                              lb_row, cn_row, bd512)
        ob_s = ob_s.reshape(bd_, t_pad_s, 256)[:, 0]
        oc_s = oc_s.reshape(bd_, t_pad_s, 256)[:, 0]
        xs = _out_mlp(xs, oa_s, ob_s, oc_s, wo, g2, w1, w2)
        outs_s["k"].append(ps["kf"].reshape(bd_, t_dec, A_HEADS, HEAD_DIM))
        outs_s["v"].append(ps["vf"].reshape(bd_, t_dec, A_HEADS, HEAD_DIM))
        outs_s["ik"].append(ps["ikf"].reshape(bd_, t_dec, IDX_DIM))
        outs_s["ds"].append(_from_bd4(ds_s))
        outs_s["dc"].append(dc_s)
        outs_s["hs"].append(jnp.swapaxes(_from_bd4(hs_s), -1, -2))

    st = lambda o, k: jnp.stack(o[k])
    return (xp.reshape(bp, s_len, d), xs.reshape(bd_, t_dec, d),
            st(outs_p, "k"), st(outs_p, "v"), st(outs_p, "ik"), st(outs_p, "ds"), st(outs_p, "dc"), st(outs_p, "hs"),
            st(outs_s, "k"), st(outs_s, "v"), st(outs_s, "ik"), st(outs_s, "ds"), st(outs_s, "dc"), st(outs_s, "hs"))
```

```python
import functools

import numpy as np
import jax
import jax.numpy as jnp
from jax import lax
from jax.experimental import pallas as pl
from jax.experimental.pallas import tpu as pltpu

F32 = jnp.float32
BF16 = jnp.bfloat16
I32 = jnp.int32
HIGHEST = lax.Precision.HIGHEST

D_MODEL = 1024
HEAD_DIM = 64
A_HEADS = 8
A_WIDTH = A_HEADS * HEAD_DIM
IDX_HEADS = 16
IDX_DIM = 64
IDX_ROPE = 32
TOPK_MAX = 256
B_HEADS = 4
B_DK = 64
B_CONV = 768
CONV_W = 4
C_HEADS = 4
D_FF = 4 * D_MODEL
ROPE_THETA = 10000.0
EPS = 1e-6
PAGE_SIZE = 128
CHUNK = 64

LANES = 128
VMEM_LIMIT = 48 * 1024 * 1024

Z_IQ = 3 * A_WIDTH
Z_BG = Z_IQ + IDX_HEADS * IDX_DIM
Z_MISC = Z_BG + 256
Z_CONV = 3072
Z_HG = Z_CONV + B_CONV
NZ = Z_HG + 768
M_IW = 64
M_BA = 80
M_BB = 84

NEG = -1e30
INT_MIN = -2 ** 31
LOG2E = 1.4426950408889634
KSPLIT = 4


def _cparams(sem):
    return pltpu.CompilerParams(dimension_semantics=sem, vmem_limit_bytes=VMEM_LIMIT)


def _dot(a, b, precision=None):
    return jnp.dot(a, b, preferred_element_type=F32, precision=precision)


def _dot_nt(a, b, precision=None):
    return lax.dot_general(a, b, (((1,), (1,)), ((), ())), preferred_element_type=F32, precision=precision)


def _dot_tn(a, b, precision=None):
    return lax.dot_general(a, b, (((0,), (0,)), ((), ())), preferred_element_type=F32, precision=precision)


_DIMS = {"nn": (((1,), (0,)), ((), ())), "nt": (((1,), (1,)), ((), ())), "tn": (((0,), (0,)), ((), ()))}


def _bdot(a, b, form):
    return lax.dot_general(a, b, _DIMS[form], preferred_element_type=F32)


def _split2(x):
    hi = x.astype(BF16)
    return hi, (x - hi.astype(F32)).astype(BF16)


def _split3(x):
    hi = x.astype(BF16)
    r = x - hi.astype(F32)
    mid = r.astype(BF16)
    return hi, mid, (r - mid.astype(F32)).astype(BF16)


def _mm1(a, b, form="nn"):
    return _bdot(a.astype(BF16), b.astype(BF16), form)


def _mm3(a, b, form="nn"):
    ah, al = _split2(a)
    bh, bl = _split2(b)
    return _bdot(ah, bh, form) + (_bdot(ah, bl, form) + _bdot(al, bh, form))


def _mm_sel(a01, b, form="nn"):
    a = a01.astype(BF16)
    h, m, l = _split3(b)
    return _bdot(a, h, form) + (_bdot(a, m, form) + _bdot(a, l, form))


def _group_sum(y, bd):
    hi = y.astype(BF16)
    lo = (y - hi.astype(F32)).astype(BF16)
    return _dot(hi, bd) + _dot(lo, bd)


def _to_key(x):
    bits = lax.bitcast_convert_type(x, I32)
    return jnp.where(bits < 0, bits ^ jnp.int32(0x7FFFFFFF), bits)


def _silu(x):
    return x * jax.nn.sigmoid(x)


def _softplus(x):
    return jnp.maximum(x, 0.0) + jnp.log(1.0 + jnp.exp(-jnp.abs(x)))


def _norm_matmul_kernel(x_ref, g_ref, w_ref, o_ref, h_scr):
    @pl.when(pl.program_id(1) == 0)
    def _():
        x = x_ref[...]
        ms = jnp.mean(x * x, axis=-1, keepdims=True)
        h_scr[...] = (x * lax.rsqrt(ms + EPS) * g_ref[...]).astype(BF16)

    o_ref[...] = _dot(h_scr[...], w_ref[...])


def _norm_matmul(x, g, w, tn):
    m, d = x.shape
    n = w.shape[1]
    tm = min(512, m)
    return pl.pallas_call(
        _norm_matmul_kernel,
        grid=(m // tm, n // tn),
        in_specs=[pl.BlockSpec((tm, d), lambda i, j: (i, 0)),
                  pl.BlockSpec((1, d), lambda i, j: (0, 0)),
                  pl.BlockSpec((d, tn), lambda i, j: (0, j))],
        out_specs=pl.BlockSpec((tm, tn), lambda i, j: (i, j)),
        out_shape=jax.ShapeDtypeStruct((m, n), F32),
        scratch_shapes=[pltpu.VMEM((tm, d), BF16)],
        compiler_params=_cparams(("parallel", "arbitrary")),
        name="norm_matmul",
    )(x, g, w)


def _prep_values(za_ref, zm_ref, ca_ref, sa_ref, ci_ref, si_ref, qg_ref, kg_ref, ikg_ref, ikb_ref, bd_ref):
    tm = za_ref.shape[0]
    bd = bd_ref[...]
    ca = jnp.concatenate([ca_ref[...]] * 4, axis=1)
    sa = jnp.concatenate([sa_ref[...]] * 4, axis=1)
    lane_a = lax.broadcasted_iota(I32, (tm, A_WIDTH), 1)
    lo_half = (lane_a % HEAD_DIM) < (HEAD_DIM // 2)
    lane128 = lax.broadcasted_iota(I32, (tm, LANES), 1)
    left = lane128 < 64

    def norm_rope(x, g):
        ss = _group_sum(x * x, bd)
        y = x * lax.rsqrt(ss * (1.0 / HEAD_DIM) + EPS) * g
        sw = jnp.where(lo_half, pltpu.roll(y, A_WIDTH - 32, 1), pltpu.roll(y, 32, 1))
        return y * ca + sw * sa

    q = norm_rope(za_ref[:, 0:A_WIDTH], qg_ref[...]) * (HEAD_DIM ** -0.5 * LOG2E)
    k = norm_rope(za_ref[:, A_WIDTH:2 * A_WIDTH], kg_ref[...])
    v = za_ref[:, 2 * A_WIDTH:3 * A_WIDTH]

    nq = IDX_HEADS * IDX_DIM
    xi = za_ref[:, Z_IQ:Z_IQ + nq]
    lane_i = lax.broadcasted_iota(I32, (tm, nq), 1)
    ci = jnp.concatenate([ci_ref[...]] * 8, axis=1)
    si = jnp.concatenate([si_ref[...]] * 8, axis=1)
    swi = jnp.where((lane_i % IDX_DIM) < 16, pltpu.roll(xi, nq - 16, 1), pltpu.roll(xi, 16, 1))
    iq = xi * ci + swi * si

    m = zm_ref[...]
    mu = jnp.sum(jnp.where(left, m, 0.0), axis=-1, keepdims=True) * (1.0 / IDX_DIM)
    xc = jnp.where(left, m - mu, 0.0)
    var = jnp.sum(xc * xc, axis=-1, keepdims=True) * (1.0 / IDX_DIM)
    y = xc * lax.rsqrt(var + EPS) * ikg_ref[...] + ikb_ref[...]
    swk = jnp.where(lane128 < 16, pltpu.roll(y, LANES - 16, 1), pltpu.roll(y, 16, 1))
    ikr = y * ci_ref[...] + swk * si_ref[...]
    ikdup = jnp.where(left, ikr, pltpu.roll(ikr, 64, 1)).astype(BF16)
    misc = m * (IDX_HEADS ** -0.5 * IDX_DIM ** -0.5)
    return q, k, v, iq, ikr, ikdup, misc


def _prep_kernel(*refs):
    ins, (qpad_ref, kf_ref, vf_ref, iqpad_ref, ikf_ref, ikdup_ref, misc_ref) = refs[:11], refs[11:]
    q, k, v, iq, ikr, ikdup, misc = _prep_values(*ins)
    left = lax.broadcasted_iota(I32, (q.shape[0], LANES), 1) < 64
    for h in range(A_HEADS):
        keep = left if h % 2 == 0 else jnp.logical_not(left)
        qpad_ref[h] = jnp.where(keep, q[:, (h // 2) * LANES:(h // 2 + 1) * LANES], 0.0).astype(BF16)
    for h in range(IDX_HEADS):
        keep = left if h % 2 == 0 else jnp.logical_not(left)
        iqpad_ref[h] = jnp.where(keep, iq[:, (h // 2) * LANES:(h // 2 + 1) * LANES], 0.0).astype(BF16)
    kf_ref[...] = k
    vf_ref[...] = v
    ikf_ref[...] = ikr[:, 0:IDX_DIM]
    ikdup_ref[...] = ikdup
    misc_ref[...] = misc


def _prep_t_kernel(*refs):
    ins, (qt_ref, kt_ref, kb_ref, vt_ref, vaug_ref, iqt_ref, ikt_ref, ikdup_ref, iwt_ref) = refs[:11], refs[11:]
    q, k, v, iq, ikr, ikdup, misc = _prep_values(*ins)
    tm = q.shape[0]
    top = lax.broadcasted_iota(I32, (LANES, tm), 0) < 64
    qt = q.T
    vt = v.T
    iqt = iq.T
    for h in range(A_HEADS):
        keep = top if h % 2 == 0 else jnp.logical_not(top)
        rows = slice((h // 2) * LANES, (h // 2 + 1) * LANES)
        qt_ref[h] = jnp.where(keep, qt[rows, :], 0.0).astype(BF16)
        vaug_ref[h] = jnp.where(keep, vt[rows, :], 1.0).astype(BF16)
    for h in range(IDX_HEADS):
        keep = top if h % 2 == 0 else jnp.logical_not(top)
        iqt_ref[h] = jnp.where(keep, iqt[(h // 2) * LANES:(h // 2 + 1) * LANES, :], 0.0).astype(BF16)
    kt_ref[...] = k.T
    kb_ref[...] = k.astype(BF16)
    vt_ref[...] = vt
    ikt_ref[...] = ikr.T[0:IDX_DIM, :]
    ikdup_ref[...] = ikdup
    iwt_ref[...] = misc.T[M_IW:M_IW + IDX_HEADS, :]


def _prep_in_specs(tm, tab_spec):
    row = lambda w: pl.BlockSpec((1, w), lambda i: (0, 0))
    return [pl.BlockSpec((tm, Z_BG), lambda i: (i, 0)),
            pl.BlockSpec((tm, LANES), lambda i: (i, Z_MISC // LANES)),
            tab_spec, tab_spec, tab_spec, tab_spec,
            row(A_WIDTH), row(A_WIDTH), row(LANES), row(LANES),
            pl.BlockSpec((A_WIDTH, A_WIDTH), lambda i: (0, 0))]


def _prep(z, tabs, qg, kg, ikg, ikb, bd512):
    m = z.shape[0]
    tm = m
    tab_spec = pl.BlockSpec((1, LANES), lambda i: (0, 0))
    tok = lambda w: pl.BlockSpec((tm, w), lambda i: (i, 0))
    outs = pl.pallas_call(
        _prep_kernel,
        grid=(m // tm,),
        in_specs=_prep_in_specs(tm, tab_spec),
        out_specs=[pl.BlockSpec((A_HEADS, tm, LANES), lambda i: (0, i, 0)),
                   tok(A_WIDTH), tok(A_WIDTH),
                   pl.BlockSpec((IDX_HEADS, tm, LANES), lambda i: (0, i, 0)),
                   tok(IDX_DIM), tok(LANES), tok(LANES)],
        out_shape=[jax.ShapeDtypeStruct((A_HEADS, m, LANES), BF16),
                   jax.ShapeDtypeStruct((m, A_WIDTH), F32),
                   jax.ShapeDtypeStruct((m, A_WIDTH), F32),
                   jax.ShapeDtypeStruct((IDX_HEADS, m, LANES), BF16),
                   jax.ShapeDtypeStruct((m, IDX_DIM), F32),
                   jax.ShapeDtypeStruct((m, LANES), BF16),
                   jax.ShapeDtypeStruct((m, LANES), F32)],
        compiler_params=_cparams(("parallel",)),
        name="prep",
    )(z, z, *tabs, qg, kg, ikg, ikb, bd512)
    names = ("qpad", "kf", "vf", "iqpad", "ikf", "ikdup", "misc")
    return dict(zip(names, outs))


def _prep_t(z, tabs, nb, t_seq, qg, kg, ikg, ikb, bd512):
    m = z.shape[0]
    tm = min(256, t_seq)
    assert t_seq % tm == 0
    nblk = t_seq // tm
    tab_spec = pl.BlockSpec((tm, LANES), lambda i: (i % nblk, 0))
    feat = lambda f: pl.BlockSpec((None, f, tm), lambda i: (i // nblk, 0, i % nblk))
    heads = lambda nh: pl.BlockSpec((nh, LANES, tm), lambda i: (0, 0, i))
    outs = pl.pallas_call(
        _prep_t_kernel,
        grid=(m // tm,),
        in_specs=_prep_in_specs(tm, tab_spec),
        out_specs=[heads(A_HEADS), feat(A_WIDTH),
                   pl.BlockSpec((tm, A_WIDTH), lambda i: (i, 0)),
                   feat(A_WIDTH), heads(A_HEADS), heads(IDX_HEADS), feat(IDX_DIM),
                   pl.BlockSpec((tm, LANES), lambda i: (i, 0)),
                   pl.BlockSpec((IDX_HEADS, tm), lambda i: (0, i))],
        out_shape=[jax.ShapeDtypeStruct((A_HEADS, LANES, m), BF16),
                   jax.ShapeDtypeStruct((nb, A_WIDTH, t_seq), F32),
                   jax.ShapeDtypeStruct((m, A_WIDTH), BF16),
                   jax.ShapeDtypeStruct((nb, A_WIDTH, t_seq), F32),
                   jax.ShapeDtypeStruct((A_HEADS, LANES, m), BF16),
                   jax.ShapeDtypeStruct((IDX_HEADS, LANES, m), BF16),
                   jax.ShapeDtypeStruct((nb, IDX_DIM, t_seq), F32),
                   jax.ShapeDtypeStruct((m, LANES), BF16),
                   jax.ShapeDtypeStruct((IDX_HEADS, m), F32)],
        compiler_params=_cparams(("parallel",)),
        name="prep_t",
    )(z, z, *tabs, qg, kg, ikg, ikb, bd512)
    names = ("qt", "kt", "kb", "vt", "vaug", "iqt", "ikt", "ikdup", "iwt")
    return dict(zip(names, outs))


def _attn_prompt_kernel(qb_tab, kb_tab, nkb_tab, qt_ref, iqt_ref, iwt_ref, ik_ref, k_ref, vaug_ref, o_ref,
                        key_scr, thr_scr, m_scr, acc_scr, iq2_scr, *, tq, tk, topk):
    s = pl.program_id(1)
    qb = qb_tab[s]
    kb = kb_tab[s]
    nkb = nkb_tab[s]
    koff = lax.broadcasted_iota(I32, (tk, tq), 0)
    tpos = qb * tq + lax.broadcasted_iota(I32, (tk, tq), 1)

    @pl.when(kb == 0)
    def _():
        iwt = iwt_ref[...]
        for j in range(IDX_HEADS // 2):
            iq2_scr[j] = jnp.concatenate([iqt_ref[2 * j], iqt_ref[2 * j + 1]], axis=1)

        def score_chunk(c, carry):
            ikc = ik_ref[pl.ds(pl.multiple_of(c * tk, tk), tk), :]
            acc = jnp.zeros((tk, tq), F32)
            for j in range(IDX_HEADS // 2):
                s2 = _dot(ikc, iq2_scr[j])
                acc = acc + jnp.maximum(s2[:, 0:tq], 0.0) * iwt[2 * j:2 * j + 1, :]
                acc = acc + jnp.maximum(s2[:, tq:2 * tq], 0.0) * iwt[2 * j + 1:2 * j + 2, :]
            acc = jnp.where(c * tk + koff <= tpos, acc, -jnp.inf)
            key_scr[c] = _to_key(acc)
            return carry

        lax.fori_loop(0, nkb, score_chunk, 0)

        def count_ge(cand):
            def body(c, cnt):
                ge = (key_scr[c] >= cand).astype(I32)
                return cnt + jnp.sum(ge.reshape(tk // 8, 8, tq), axis=0)

            cnt = lax.fori_loop(0, nkb, body, jnp.zeros((8, tq), I32))
            return jnp.sum(cnt, axis=0, keepdims=True)

        zero = jnp.zeros((1, tq), I32)
        c0 = count_ge(zero)
        thr0 = jnp.where(c0 >= topk, zero, jnp.full((1, tq), INT_MIN, I32))
        n_adm = qb * tq + lax.broadcasted_iota(I32, (1, tq), 1) + 1
        open0 = jnp.logical_and(n_adm >= topk, c0 != topk).astype(I32)

        def cond(st):
            i, _, _, n_open = st
            return jnp.logical_and(i < 31, n_open > 0)

        def bit_step(st):
            i, thr, opn, _ = st
            cand = thr | jnp.left_shift(jnp.int32(1), 30 - i)
            cnt = count_ge(cand)
            thr = jnp.where(cnt >= topk, cand, thr)
            opn = jnp.where(cnt == topk, 0, opn)
            return i + 1, thr, opn, jnp.sum(opn)

        _, thr, _, _ = lax.while_loop(cond, bit_step, (jnp.int32(0), thr0, open0, jnp.sum(open0)))
        thr_scr[...] = thr
        m_scr[...] = jnp.full(m_scr.shape, NEG, F32)
        acc_scr[...] = jnp.zeros(acc_scr.shape, F32)

    sel = jnp.logical_and(key_scr[kb] >= thr_scr[...], kb * tk + koff <= tpos)
    bias = jnp.where(sel, 0.0, NEG)
    ts = tk // KSPLIT
    for u in range(KSPLIT):
        rows = slice(u * ts, (u + 1) * ts)
        for h in range(A_HEADS):
            p0 = (h // 2) * LANES
            sc = _dot(k_ref[rows, p0:p0 + LANES], qt_ref[h]) + bias[rows, :]
            m_old = m_scr[h:h + 1, :]
            m_new = jnp.maximum(m_old, jnp.max(sc, axis=0, keepdims=True))
            alpha = jnp.exp2(m_old - m_new)
            p = jnp.exp2(sc - m_new).astype(BF16)
            acc_scr[h] = alpha * acc_scr[h] + _dot(vaug_ref[h, :, rows], p)
            m_scr[h:h + 1, :] = m_new

    @pl.when(kb == nkb - 1)
    def _():
        for p in range(A_HEADS // 2):
            a = acc_scr[2 * p]
            b = acc_scr[2 * p + 1]
            ot = jnp.concatenate([a[0:64, :] / a[64:65, :], b[64:128, :] / b[0:1, :]], axis=0)
            o_ref[:, p * LANES:(p + 1) * LANES] = ot.T


def _attn_prompt(pp, nb, s_len):
    tq, tk = 256, 512
    tk = min(tk, s_len)
    topk = min(TOPK_MAX, s_len // 4)
    nqb = s_len // tq
    nkblk = s_len // tk
    qb_l, kb_l, nkb_l = [], [], []
    for qb in range(nqb):
        nkb = ((qb + 1) * tq + tk - 1) // tk
        for kb in range(nkb):
            qb_l.append(qb)
            kb_l.append(kb)
            nkb_l.append(nkb)
    tabs = [jnp.asarray(np.array(a, np.int32)) for a in (qb_l, kb_l, nkb_l)]
    nsteps = len(qb_l)
    m = nb * s_len
    grid_spec = pltpu.PrefetchScalarGridSpec(
        num_scalar_prefetch=3,
        grid=(nb, nsteps),
        in_specs=[
            pl.BlockSpec((A_HEADS, LANES, tq), lambda b, s, qt, kt, nt: (0, 0, b * nqb + qt[s])),
            pl.BlockSpec((IDX_HEADS, LANES, tq), lambda b, s, qt, kt, nt: (0, 0, b * nqb + qt[s])),
            pl.BlockSpec((IDX_HEADS, tq), lambda b, s, qt, kt, nt: (0, b * nqb + qt[s])),
            pl.BlockSpec((s_len, LANES), lambda b, s, qt, kt, nt: (b, 0)),
            pl.BlockSpec((tk, A_WIDTH), lambda b, s, qt, kt, nt: (b * nkblk + kt[s], 0)),
            pl.BlockSpec((A_HEADS, LANES, tk), lambda b, s, qt, kt, nt: (0, 0, b * nkblk + kt[s])),
        ],
        out_specs=pl.BlockSpec((tq, A_WIDTH), lambda b, s, qt, kt, nt: (b * nqb + qt[s], 0)),
        scratch_shapes=[pltpu.VMEM((nkblk, tk, tq), I32),
                        pltpu.VMEM((1, tq), I32),
                        pltpu.VMEM((A_HEADS, tq), F32),
                        pltpu.VMEM((A_HEADS, LANES, tq), F32),
                        pltpu.VMEM((IDX_HEADS // 2, LANES, 2 * tq), BF16)],
    )
    return pl.pallas_call(
        functools.partial(_attn_prompt_kernel, tq=tq, tk=tk, topk=topk),
        grid_spec=grid_spec,
        out_shape=jax.ShapeDtypeStruct((m, A_WIDTH), F32),
        compiler_params=_cparams(("arbitrary", "arbitrary")),
        name="attn_prompt",
    )(*tabs, pp["qt"], pp["iqt"], pp["iwt"], pp["ikdup"], pp["kb"], pp["vaug"])


def _sample_score_kernel(pt_ref, iq_ref, iw_ref, *rest, npg):
    page_refs = rest[:npg]
    o_ref = rest[npg]
    iq = iq_ref[...]
    w = iw_ref[...]
    for g in range(npg):
        pg = page_refs[g][...].astype(BF16)
        sc = _dot(iq, pg)
        o_ref[g:g + 1, :] = jnp.sum(jnp.maximum(sc, 0.0) * w, axis=0, keepdims=True)


def _sample_attn_kernel(pt_ref, sc_ref, iq_ref, iw_ref, ikn_ref, qbd_ref, kn_ref, vn_ref, *rest, npg, topk):
    k_refs = rest[:npg]
    v_refs = rest[npg:2 * npg]
    o_ref = rest[2 * npg]
    key_scr, thr_scr, nk_scr, m_scr, l_scr, acc_scr = rest[2 * npg + 1:]
    j = pl.program_id(1)
    nj = pl.num_programs(1)

    @pl.when(j == 0)
    def _():
        keys = _to_key(sc_ref[...])
        key_scr[...] = keys
        sn = jnp.sum(iq_ref[...].astype(F32) * ikn_ref[...].astype(F32), axis=1, keepdims=True)
        snew = jnp.sum(jnp.maximum(sn, 0.0) * iw_ref[...], axis=0, keepdims=True)
        knew = _to_key(snew)
        nk_scr[...] = knew

        def count_ge(cand):
            c = jnp.sum((keys >= cand).astype(I32), axis=1, keepdims=True)
            return jnp.sum(c, axis=0, keepdims=True) + (knew >= cand).astype(I32)

        thr = jnp.full((1, 1), INT_MIN, I32)
        zero = jnp.zeros((1, 1), I32)
        thr = jnp.where(count_ge(zero) >= topk, zero, thr)

        def bit_step(i, thr):
            cand = thr | jnp.left_shift(jnp.int32(1), 30 - i)
            return jnp.where(count_ge(cand) >= topk, cand, thr)

        thr_scr[...] = lax.fori_loop(0, 31, bit_step, thr)
        m_scr[...] = jnp.full(m_scr.shape, NEG, F32)
        l_scr[...] = jnp.zeros(l_scr.shape, F32)
        acc_scr[...] = jnp.zeros(acc_scr.shape, F32)

    qbd = qbd_ref[...]
    thr = thr_scr[...]
    sel = jnp.concatenate([key_scr[pl.ds(j * npg + g, 1), :] >= thr for g in range(npg)], axis=1)
    sc = jnp.concatenate([_dot(qbd, k_refs[g][...].astype(BF16)) for g in range(npg)], axis=1)
    sc = jnp.where(sel, sc, NEG)
    m_old = m_scr[...]
    m_new = jnp.maximum(m_old, jnp.max(sc, axis=1, keepdims=True))
    alpha = jnp.exp2(m_old - m_new)
    p = jnp.where(sel, jnp.exp2(sc - m_new), 0.0)
    l_scr[...] = alpha * l_scr[...] + jnp.sum(p, axis=1, keepdims=True)
    pv = jnp.zeros(acc_scr.shape, F32)
    for g in range(npg):
        pv = pv + _dot_nt(p[:, g * PAGE_SIZE:(g + 1) * PAGE_SIZE].astype(BF16), v_refs[g][...].astype(BF16))
    acc_scr[...] = alpha * acc_scr[...] + pv
    m_scr[...] = m_new

    @pl.when(j == nj - 1)
    def _():
        sel = nk_scr[...] >= thr
        sc = jnp.sum(qbd.astype(F32) * kn_ref[...].astype(BF16).astype(F32), axis=1, keepdims=True)
        sc = jnp.where(sel, sc, NEG)
        m_old = m_scr[...]
        m_new = jnp.maximum(m_old, sc)
        alpha = jnp.exp2(m_old - m_new)
        p = jnp.where(sel, jnp.exp2(sc - m_new), 0.0)
        l_new = alpha * l_scr[...] + p
        acc = alpha * acc_scr[...] + p.astype(BF16).astype(F32) * vn_ref[...].astype(BF16).astype(F32)
        o = acc / l_new
        hrow = lax.broadcasted_iota(I32, (A_HEADS, A_WIDTH), 0)
        hcol = lax.broadcasted_iota(I32, (A_HEADS, A_WIDTH), 1) // HEAD_DIM
        o_ref[...] = jnp.sum(jnp.where(hrow == hcol, o, 0.0), axis=0, keepdims=True)


def _attn_sample(pp, cache_k, cache_v, cache_ik, page_table, layer):
    nb, npages = page_table.shape
    npg_s = int(np.gcd(npages, 32))
    npg = int(np.gcd(npages, 16))
    topk = min(TOPK_MAX, (npages * PAGE_SIZE + 1) // 4)
    pt = page_table.reshape(-1).astype(I32)
    n_pool = cache_k.shape[1]
    ck = jnp.transpose(cache_k, (0, 1, 3, 4, 2)).reshape(cache_k.shape[0], n_pool, A_WIDTH, PAGE_SIZE)
    cv = jnp.transpose(cache_v, (0, 1, 3, 4, 2)).reshape(cache_v.shape[0], n_pool, A_WIDTH, PAGE_SIZE)
    cache_ik = jnp.transpose(cache_ik, (0, 1, 3, 2))

    iq = pp["iqpad"]
    iq = jnp.transpose(iq[:, :, :64] + iq[:, :, 64:], (1, 0, 2))
    iw = pp["misc"][:, M_IW:M_IW + IDX_HEADS].reshape(nb, IDX_HEADS, 1)
    ikn = pp["ikdup"][:, :IDX_DIM].reshape(nb, 1, IDX_DIM)
    qp = jnp.transpose(pp["qpad"], (1, 0, 2))
    qbd = jnp.zeros((nb, A_HEADS, A_HEADS // 2, LANES), BF16)
    for h in range(A_HEADS):
        qbd = qbd.at[:, h, h // 2, :].set(qp[:, h, :])
    qbd = qbd.reshape(nb, A_HEADS, A_WIDTH)
    kn = pp["kf"].reshape(nb, 1, A_WIDTH)
    vn = pp["vf"].reshape(nb, 1, A_WIDTH)

    def page_spec(width, g, per_step):
        return pl.BlockSpec((None, None, width, PAGE_SIZE),
                            lambda b, j, ptr: (layer, ptr[b * npages + j * per_step + g], 0, 0))

    per_b = lambda *shape: pl.BlockSpec((None,) + shape, lambda b, j, ptr: (b,) + (0,) * len(shape))

    scores = pl.pallas_call(
        functools.partial(_sample_score_kernel, npg=npg_s),
        grid_spec=pltpu.PrefetchScalarGridSpec(
            num_scalar_prefetch=1,
            grid=(nb, npages // npg_s),
            in_specs=[per_b(IDX_HEADS, IDX_DIM), per_b(IDX_HEADS, 1)]
                     + [page_spec(IDX_DIM, g, npg_s) for g in range(npg_s)],
            out_specs=pl.BlockSpec((None, npg_s, PAGE_SIZE), lambda b, j, ptr: (b, j, 0)),
        ),
        out_shape=jax.ShapeDtypeStruct((nb, npages, PAGE_SIZE), F32),
        compiler_params=_cparams(("arbitrary", "arbitrary")),
        name="sample_score",
    )(pt, iq, iw, *([cache_ik] * npg_s))

    out = pl.pallas_call(
        functools.partial(_sample_attn_kernel, npg=npg, topk=topk),
        grid_spec=pltpu.PrefetchScalarGridSpec(
            num_scalar_prefetch=1,
            grid=(nb, npages // npg),
            in_specs=[per_b(npages, PAGE_SIZE), per_b(IDX_HEADS, IDX_DIM), per_b(IDX_HEADS, 1),
                      per_b(1, IDX_DIM), per_b(A_HEADS, A_WIDTH), per_b(1, A_WIDTH), per_b(1, A_WIDTH)]
                     + [page_spec(A_WIDTH, g, npg) for g in range(npg)]
                     + [page_spec(A_WIDTH, g, npg) for g in range(npg)],
            out_specs=pl.BlockSpec((None, 1, A_WIDTH), lambda b, j, ptr: (b, 0, 0)),
            scratch_shapes=[pltpu.VMEM((npages, PAGE_SIZE), I32),
                            pltpu.VMEM((1, 1), I32),
                            pltpu.VMEM((1, 1), I32),
                            pltpu.VMEM((A_HEADS, 1), F32),
                            pltpu.VMEM((A_HEADS, 1), F32),
                            pltpu.VMEM((A_HEADS, A_WIDTH), F32)],
        ),
        out_shape=jax.ShapeDtypeStruct((nb, 1, A_WIDTH), F32),
        compiler_params=_cparams(("arbitrary", "arbitrary")),
        name="sample_attn",
    )(pt, scores, iq, iw, ikn, qbd, kn, vn, *([ck] * npg), *([cv] * npg))
    return out.reshape(nb, A_WIDTH)


def _unit_lower_inverse(mm, c):
    r = lax.broadcasted_iota(I32, (c, c), 0)
    q = lax.broadcasted_iota(I32, (c, c), 1)
    eye = (r == q).astype(F32)
    blk = min(16, c)
    pw = jnp.where(r // blk == q // blk, -mm, 0.0)
    inv = eye + pw
    k = 1
    while 2 * k < blk:
        pw = _mm3(pw, pw)
        inv = inv + _mm3(inv, pw)
        k *= 2
    sz = 2 * blk
    while sz <= c:
        half = sz // 2
        e = jnp.where(jnp.logical_and(r // sz == q // sz, r // half != q // half), mm, 0.0)
        inv = inv - _mm3(inv, _mm3(e, inv))
        sz *= 2
    return inv


def _delta_kernel(xc_ref, bg_ref, ms_ref, buf_ref, s0_ref, cw_ref, al_ref, dt_ref, dn_ref, bd512_ref, bd128_ref,
                  o_ref, sout_ref, nbuf_ref, xp_scr, s_scr, *, c, t_valid, nt):
    t = pl.program_id(1)

    @pl.when(t == 0)
    def _():
        xp_scr[5:8, :] = buf_ref[...]
        s_scr[...] = s0_ref[...]

    xp_scr[8:8 + c, :] = xc_ref[...]
    cw = cw_ref[...]
    y = (cw[0:1] * xp_scr[5:5 + c, :] + cw[1:2] * xp_scr[6:6 + c, :]
         + cw[2:3] * xp_scr[7:7 + c, :] + cw[3:4] * xp_scr[8:8 + c, :])

    tb, lt = (t_valid - 1) // c, (t_valid - 1) % c

    @pl.when(t == tb)
    def _():
        nbuf_ref[...] = xp_scr[8 + lt - 2:8 + lt + 1, :]

    hist = xp_scr[c + 5:c + 8, :]
    xp_scr[5:8, :] = hist

    y = _silu(y)
    ss = _group_sum(y[:, 0:512] * y[:, 0:512], bd512_ref[...])
    inv_n = lax.rsqrt(ss + EPS)
    qn = y[:, 0:256] * inv_n[:, 0:256] * (B_DK ** -0.5)
    kn = y[:, 256:512] * inv_n[:, 256:512]
    vv = y[:, 512:768]
    ms = ms_ref[...]
    g_all = -jnp.exp(al_ref[...]) * _softplus(ms + dt_ref[...])
    beta_all = jax.nn.sigmoid(ms)
    if t_valid % c != 0:
        vm = (t * c + lax.broadcasted_iota(I32, (c, 1), 0) < t_valid).astype(F32)
        g_all = g_all * vm
        beta_all = beta_all * vm
        kn = kn * vm
        vv = vv * vm

    r = lax.broadcasted_iota(I32, (c, c), 0)
    q = lax.broadcasted_iota(I32, (c, c), 1)
    incl = r >= q
    strict = r > q
    gcum_all = _mm_sel(incl, g_all)
    lane = lax.broadcasted_iota(I32, (c, LANES), 1)
    left = lane < 64
    ones = jnp.ones((c, LANES), BF16)
    row128 = lax.broadcasted_iota(I32, (LANES, LANES), 0)
    col128 = lax.broadcasted_iota(I32, (LANES, LANES), 1)
    bdmask = (row128 // 64) == (col128 // 64)

    for p in range(B_HEADS // 2):
        kp = kn[:, p * LANES:(p + 1) * LANES]
        qp = qn[:, p * LANES:(p + 1) * LANES]
        vp = vv[:, p * LANES:(p + 1) * LANES]
        s_old = s_scr[p]
        u_pair = jnp.zeros((c, LANES), F32)
        w_pair = jnp.zeros((c, LANES), F32)
        pmats, gcs = [], []
        for e in range(2):
            h = 2 * p + e
            hm = left if e == 0 else jnp.logical_not(left)
            gc = gcum_all[:, M_BA + h:M_BA + h + 1]
            bt = beta_all[:, M_BB + h:M_BB + h + 1]
            grow = _mm_sel(ones, jnp.where(lane == M_BA + h, gcum_all, 0.0), "nt")
            dec = jnp.exp(jnp.where(incl, gc - grow, 0.0))
            km = jnp.where(hm, kp, 0.0)
            kk = _mm1(km, kp, "nt")
            mm = jnp.where(strict, bt * kk * dec, 0.0)
            inv = _unit_lower_inverse(mm, c)
            rhs = jnp.concatenate([bt * jnp.where(hm, vp, 0.0), (bt * jnp.exp(gc)) * km], axis=1)
            uw = _mm3(inv, rhs)
            u_pair = u_pair + uw[:, 0:LANES]
            w_pair = w_pair + uw[:, LANES:2 * LANES]
            qk = _mm1(jnp.where(hm, qp, 0.0), kp, "nt")
            pmats.append(jnp.where(incl, qk * dec, 0.0))
            gcs.append(gc)
        gc_pair = jnp.where(left, gcs[0], gcs[1])
        delta = u_pair - _mm3(w_pair, s_old)
        o = _mm1(qp * jnp.exp(gc_pair), s_old)
        o = o + _mm1(pmats[0], jnp.where(left, delta, 0.0))
        o = o + _mm1(pmats[1], jnp.where(left, 0.0, delta))
        glast = gc_pair[c - 1:c, :]
        kd = kp * jnp.exp(glast - gc_pair)
        gl_rows = jnp.where(row128 < 64, jnp.exp(gcs[0][c - 1:c, :]), jnp.exp(gcs[1][c - 1:c, :]))
        s_scr[p] = gl_rows * s_old + jnp.where(bdmask, _mm3(kd, delta, "tn"), 0.0)
        ms_o = _group_sum(o * o, bd128_ref[...]) * (1.0 / 64)
        o_ref[:, p * LANES:(p + 1) * LANES] = (o * lax.rsqrt(ms_o + EPS) * dn_ref[...]
                                               * _silu(bg_ref[:, p * LANES:(p + 1) * LANES]))

    @pl.when(t == nt - 1)
    def _():
        sout_ref[...] = s_scr[...]


def _delta(z, nb, t_pad, t_valid, buf0, s0_pair, cw, al_row, dt_row, dn_row, bd512, bd128):
    c = CHUNK
    nt = t_pad // c
    m = nb * t_pad
    const = lambda *shape: pl.BlockSpec(shape, lambda b, t: (0,) * len(shape))
    return pl.pallas_call(
        functools.partial(_delta_kernel, c=c, t_valid=t_valid, nt=nt),
        grid=(nb, nt),
        in_specs=[pl.BlockSpec((c, B_CONV), lambda b, t: (b * nt + t, Z_CONV // B_CONV)),
                  pl.BlockSpec((c, 256), lambda b, t: (b * nt + t, Z_BG // 256)),
                  pl.BlockSpec((c, LANES), lambda b, t: (b * nt + t, Z_MISC // LANES)),
                  pl.BlockSpec((None, CONV_W - 1, B_CONV), lambda b, t: (b, 0, 0)),
                  pl.BlockSpec((None, 2, LANES, LANES), lambda b, t: (b, 0, 0, 0)),
                  const(CONV_W, B_CONV), const(1, LANES), const(1, LANES), const(1, LANES),
                  const(512, 512), const(LANES, LANES)],
        out_specs=[pl.BlockSpec((c, 256), lambda b, t: (b * nt + t, 0)),
                   pl.BlockSpec((None, 2, LANES, LANES), lambda b, t: (b, 0, 0, 0)),
                   pl.BlockSpec((None, CONV_W - 1, B_CONV), lambda b, t: (b, 0, 0))],
        out_shape=[jax.ShapeDtypeStruct((m, 256), F32),
                   jax.ShapeDtypeStruct((nb, 2, LANES, LANES), F32),
                   jax.ShapeDtypeStruct((nb, CONV_W - 1, B_CONV), F32)],
        scratch_shapes=[pltpu.VMEM((8 + c, B_CONV), F32), pltpu.VMEM((2, LANES, LANES), F32)],
        compiler_params=_cparams(("arbitrary", "arbitrary")),
        name="delta",
    )(z, z, z, buf0, s0_pair, cw, al_row, dt_row, dn_row, bd512, bd128)


def _hgrn_kernel(x_ref, s0_ref, lb_ref, cn_ref, bd128_ref, o_ref, sout_ref, s_scr, *, c, t_valid, nt):
    t = pl.program_id(1)

    @pl.when(t == 0)
    def _():
        s_scr[...] = s0_ref[...]

    lb = lb_ref[...]
    cf = x_ref[:, 256:512]
    logf = jnp.log(lb + (1.0 - lb) * jax.nn.sigmoid(cf))
    kk = (1.0 - lb) * jax.nn.sigmoid(-cf)
    qq = _silu(x_ref[:, 0:256])
    vv = x_ref[:, 512:768]
    if t_valid % c != 0:
        vm = (t * c + lax.broadcasted_iota(I32, (c, 1), 0) < t_valid).astype(F32)
        logf = logf * vm
        kk = kk * vm

    r = lax.broadcasted_iota(I32, (c, c), 0)
    q = lax.broadcasted_iota(I32, (c, c), 1)
    b = _mm_sel(r >= q, logf)
    lane = lax.broadcasted_iota(I32, (c, LANES), 1)
    left = lane < 64
    row128 = lax.broadcasted_iota(I32, (LANES, LANES), 0)
    col128 = lax.broadcasted_iota(I32, (LANES, LANES), 1)
    bdmask = (row128 // 64) == (col128 // 64)

    amats = [jnp.zeros((c, c), F32) for _ in range(C_HEADS)]
    half = c // 2
    while half >= 1:
        sz = 2 * half
        bnd = (r // sz) * sz + half - 1
        rl = _mm_sel(q == bnd, b)
        qf = qq * jnp.exp(jnp.minimum(b - rl, 0.0))
        kf = kk * jnp.exp(jnp.minimum(rl - b, 0.0))
        lmask = jnp.logical_and(r // sz == q // sz, jnp.logical_and(r % sz >= half, q % sz < half))
        for h in range(C_HEADS):
            p0 = (h // 2) * LANES
            hm = left if h % 2 == 0 else jnp.logical_not(left)
            a = _mm1(jnp.where(hm, qf[:, p0:p0 + LANES], 0.0), kf[:, p0:p0 + LANES], "nt")
            amats[h] = amats[h] + jnp.where(lmask, a, 0.0)
        half //= 2

    eb = jnp.exp(b)
    for p in range(C_HEADS // 2):
        sl = slice(p * LANES, (p + 1) * LANES)
        st_old = s_scr[p]
        qp, kp, vp, bp = qq[:, sl], kk[:, sl], vv[:, sl], b[:, sl]
        o = _mm1(qp * eb[:, sl], st_old, "nt")
        o = o + _mm1(amats[2 * p], jnp.where(left, vp, 0.0))
        o = o + _mm1(amats[2 * p + 1], jnp.where(left, 0.0, vp))
        o = o + _group_sum(qp * kp, bd128_ref[...]) * vp
        blast = bp[c - 1:c, :]
        kd = kp * jnp.exp(blast - bp)
        s_scr[p] = st_old * jnp.exp(blast) + jnp.where(bdmask, _mm3(vp, kd, "tn"), 0.0)
        ms_o = _group_sum(o * o, bd128_ref[...]) * (1.0 / 64)
        o_ref[:, sl] = o * lax.rsqrt(ms_o + EPS) * cn_ref[...]

    @pl.when(t == nt - 1)
    def _():
        sout_ref[...] = s_scr[...]


def _hgrn(z, nb, t_pad, t_valid, s0_pair, lb_row, cn_row, bd128):
    c = CHUNK
    nt = t_pad // c
    m = nb * t_pad
    const = lambda *shape: pl.BlockSpec(shape, lambda b, t: (0,) * len(shape))
    return pl.pallas_call(
        functools.partial(_hgrn_kernel, c=c, t_valid=t_valid, nt=nt),
        grid=(nb, nt),
        in_specs=[pl.BlockSpec((c, 768), lambda b, t: (b * nt + t, Z_HG // 768)),
                  pl.BlockSpec((None, 2, LANES, LANES), lambda b, t: (b, 0, 0, 0)),
                  const(1, 256), const(1, LANES), const(LANES, LANES)],
        out_specs=[pl.BlockSpec((c, 256), lambda b, t: (b * nt + t, 0)),
                   pl.BlockSpec((None, 2, LANES, LANES), lambda b, t: (b, 0, 0, 0))],
        out_shape=[jax.ShapeDtypeStruct((m, 256), F32),
                   jax.ShapeDtypeStruct((nb, 2, LANES, LANES), F32)],
        scratch_shapes=[pltpu.VMEM((2, LANES, LANES), F32)],
        compiler_params=_cparams(("arbitrary", "arbitrary")),
        name="hgrn",
    )(z, s0_pair, lb_row, cn_row, bd128)


NH = 4
HW = NH * 64


def _bd(y):
    hb = lax.broadcasted_iota(I32, y.shape, 1) // 64
    zero = jnp.zeros_like(y)
    return jnp.concatenate([jnp.where(hb == h, y, zero) for h in range(NH)], axis=0)


def _st1(x, y, form="nn"):
    return _bdot(x.astype(BF16), _bd(y.astype(BF16)), form)


def _st3(x, y):
    xh, xl = _split2(x)
    yh, yl = _split2(y)
    bh, bl = _bd(yh), _bd(yl)
    return _bdot(xh, bh, "nn") + (_bdot(xh, bl, "nn") + _bdot(xl, bh, "nn"))


def _st_unit_lower_inverse(mm, r, q):
    pw = jnp.where(r // 16 == q // 16, -mm, 0.0)
    inv = (r == q).astype(F32) + pw
    k = 1
    while 2 * k < 16:
        pw = _st3(pw, pw)
        inv = inv + _st3(inv, pw)
        k *= 2
    for sz in (32, 64):
        half = sz // 2
        e = jnp.where(jnp.logical_and(r // sz == q // sz, r // half != q // half), mm, 0.0)
        inv = inv - _st3(inv, _st3(e, inv))
    return inv


SEQ_PER_STEP = 4


class _Lock:
    def __init__(self, vals):
        self.v = list(vals)

    def _bin(self, other, f):
        ov = other.v if isinstance(other, _Lock) else [other] * len(self.v)
        return _Lock([f(a, b) for a, b in zip(self.v, ov)])

    def __add__(self, o):
        return self._bin(o, lambda a, b: a + b)

    def __radd__(self, o):
        return self._bin(o, lambda a, b: b + a)

    def __sub__(self, o):
        return self._bin(o, lambda a, b: a - b)

    def __rsub__(self, o):
        return self._bin(o, lambda a, b: b - a)

    def __mul__(self, o):
        return self._bin(o, lambda a, b: a * b)

    def __rmul__(self, o):
        return self._bin(o, lambda a, b: b * a)

    def __neg__(self):
        return _Lock([-a for a in self.v])

    def __getitem__(self, idx):
        return _Lock([a[idx] for a in self.v])


def _lift(f):
    def g(*args, **kw):
        n = max(len(a.v) for a in args if isinstance(a, _Lock))
        pick = lambda a, i: a.v[i] if isinstance(a, _Lock) else a
        return _Lock([f(*[pick(a, i) for a in args], **kw) for i in range(n)])
    return g


def _st_unit_lower_inverse_l(mm, r, q):
    st3, where = _lift(_st3), _lift(jnp.where)
    pw = where(r // 16 == q // 16, -mm, 0.0)
    inv = pw + (r == q).astype(F32)
    k = 1
    while 2 * k < 16:
        pw = st3(pw, pw)
        inv = inv + st3(inv, pw)
        k *= 2
    for sz in (32, 64):
        half = sz // 2
        e = where(jnp.logical_and(r // sz == q // sz, r // half != q // half), mm, 0.0)
        inv = inv - st3(inv, st3(e, inv))
    return inv


def _delta_st_kernel(xc_ref, bg_ref, ms_ref, buf_ref, s0_ref, cw_ref, al_ref, dt_ref, dn_ref, bd512_ref,
                     o_ref, sout_ref, nbuf_ref, xp_scr, s_scr, *, c, t_valid, nt):
    t = pl.program_id(1)
    ns = SEQ_PER_STEP
    seqs = range(ns)
    where, exp = _lift(jnp.where), _lift(jnp.exp)
    st1, st3, mm1, mm3, msel = _lift(_st1), _lift(_st3), _lift(_mm1), _lift(_mm3), _lift(_mm_sel)

    @pl.when(t == 0)
    def _():
        xp_scr[:, 5:8, :] = buf_ref[...]
        s_scr[...] = s0_ref[...]

    xp_scr[:, 8:8 + c, :] = xc_ref[...]
    cw = cw_ref[...]
    y = _Lock([cw[0:1] * xp_scr[j, 5:5 + c, :] + cw[1:2] * xp_scr[j, 6:6 + c, :]
               + cw[2:3] * xp_scr[j, 7:7 + c, :] + cw[3:4] * xp_scr[j, 8:8 + c, :] for j in seqs])

    tb, lt = (t_valid - 1) // c, (t_valid - 1) % c

    @pl.when(t == tb)
    def _():
        nbuf_ref[...] = xp_scr[:, 8 + lt - 2:8 + lt + 1, :]

    hist = xp_scr[:, c + 5:c + 8, :]
    xp_scr[:, 5:8, :] = hist

    y = _lift(_silu)(y)
    ss = _lift(_group_sum)(y[:, 0:512] * y[:, 0:512], bd512_ref[...])
    inv_n = _lift(lax.rsqrt)(ss + EPS)
    qn = y[:, 0:HW] * inv_n[:, 0:HW] * (B_DK ** -0.5)
    kn = y[:, HW:2 * HW] * inv_n[:, HW:2 * HW]
    vv = y[:, 2 * HW:3 * HW]
    ms = _Lock([ms_ref[j] for j in seqs])
    g_all = _lift(_softplus)(ms + dt_ref[...]) * (-jnp.exp(al_ref[...]))
    beta_all = _lift(jax.nn.sigmoid)(ms)
    if t_valid % c != 0:
        vm = (t * c + lax.broadcasted_iota(I32, (c, 1), 0) < t_valid).astype(F32)
        g_all = g_all * vm
        beta_all = beta_all * vm
        kn = kn * vm
        vv = vv * vm

    r = lax.broadcasted_iota(I32, (c, HW), 0)
    lane = lax.broadcasted_iota(I32, (c, HW), 1)
    q = lane % 64
    hb = lane // 64
    incl = r >= q
    strict = r > q
    r64 = lax.broadcasted_iota(I32, (c, c), 0)
    q64 = lax.broadcasted_iota(I32, (c, c), 1)
    gcum_all = msel(r64 >= q64, g_all)
    lane128 = lax.broadcasted_iota(I32, (c, LANES), 1)
    gc = _Lock([jnp.zeros((c, HW), F32)] * ns)
    bt = _Lock([jnp.zeros((c, HW), F32)] * ns)
    for h in range(NH):
        gc = where(hb == h, gcum_all[:, M_BA + h:M_BA + h + 1], gc)
        bt = where(hb == h, beta_all[:, M_BB + h:M_BB + h + 1], bt)
    gsel = _lift(lambda g: jnp.concatenate([jnp.where(lane128 == M_BA + h, g, 0.0) for h in range(NH)],
                                           axis=0))(gcum_all)
    grow = msel(jnp.ones((c, LANES), BF16), gsel, "nt")
    dec = exp(where(incl, gc - grow, 0.0))
    knb = _lift(lambda a: a.astype(BF16))(kn)
    kbd = _lift(_bd)(knb)
    kk = _lift(_bdot)(knb, kbd, "nt")
    qk = _lift(_bdot)(_lift(lambda a: a.astype(BF16))(qn), kbd, "nt")
    mm = where(strict, bt * kk * dec, 0.0)
    inv = _st_unit_lower_inverse_l(mm, r, q)
    egc = exp(gc)
    uu = st3(inv, bt * vv)
    ww = st3(inv, (bt * egc) * kn)
    s_old = _Lock([s_scr[j] for j in seqs])
    delta = uu - mm3(ww, s_old)
    o = mm1(qn * egc, s_old) + st1(where(incl, qk * dec, 0.0), delta)
    glast = gc[c - 1:c, :]
    kd = kn * exp(glast - gc)
    rb = lax.broadcasted_iota(I32, (HW, HW), 0) // 64
    cb = lax.broadcasted_iota(I32, (HW, HW), 1) // 64
    gl = _Lock([jnp.zeros((HW, HW), F32)] * ns)
    for h in range(NH):
        gl = where(rb == h, exp(gcum_all[c - 1:c, M_BA + h:M_BA + h + 1]), gl)
    s_new = gl * s_old + where(rb == cb, mm3(kd, delta, "tn"), 0.0)
    ms_o = _lift(_group_sum)(o * o, bd512_ref[0:HW, 0:HW]) * (1.0 / 64)
    out = o * _lift(lax.rsqrt)(ms_o + EPS) * dn_ref[...] * _lift(_silu)(_Lock([bg_ref[j] for j in seqs]))
    for j in seqs:
        s_scr[j] = s_new.v[j]
        o_ref[j] = out.v[j]

    @pl.when(t == nt - 1)
    def _():
        sout_ref[...] = s_scr[...]


def _delta_st(z, nb, t_pad, t_valid, buf0, s0_bd, cw, al_row, dt_row, dn_row, bd512):
    c = CHUNK
    ns = SEQ_PER_STEP
    assert c == 64 and nb % ns == 0
    nt = t_pad // c
    z3 = z.reshape(nb, t_pad, NZ)
    const = lambda *shape: pl.BlockSpec(shape, lambda b, t: (0,) * len(shape))
    tok = lambda w, col: pl.BlockSpec((ns, c, w), lambda b, t: (b, t, col))
    per_seq = lambda *shape: pl.BlockSpec((ns,) + shape, lambda b, t: (b,) + (0,) * len(shape))
    o, s_out, nbuf = pl.pallas_call(
        functools.partial(_delta_st_kernel, c=c, t_valid=t_valid, nt=nt),
        grid=(nb // ns, nt),
        in_specs=[tok(B_CONV, Z_CONV // B_CONV), tok(HW, Z_BG // HW), tok(LANES, Z_MISC // LANES),
                  per_seq(CONV_W - 1, B_CONV), per_seq(HW, HW),
                  const(CONV_W, B_CONV), const(1, LANES), const(1, LANES), const(1, HW),
                  const(512, 512)],
        out_specs=[tok(HW, 0), per_seq(HW, HW), per_seq(CONV_W - 1, B_CONV)],
        out_shape=[jax.ShapeDtypeStruct((nb, t_pad, HW), F32),
                   jax.ShapeDtypeStruct((nb, HW, HW), F32),
                   jax.ShapeDtypeStruct((nb, CONV_W - 1, B_CONV), F32)],
        scratch_shapes=[pltpu.VMEM((ns, 8 + c, B_CONV), F32), pltpu.VMEM((ns, HW, HW), F32)],
        compiler_params=_cparams(("arbitrary", "arbitrary")),
        name="delta",
    )(z3, z3, z3, buf0, s0_bd, cw, al_row, dt_row, dn_row, bd512)
    return o.reshape(nb * t_pad, HW), s_out, nbuf


def _hgrn_st_kernel(x_ref, s0_ref, lb_ref, cn_ref, bd512_ref, o_ref, sout_ref, s_scr, *, c, t_valid, nt):
    t = pl.program_id(1)
    seqs = range(SEQ_PER_STEP)
    where, exp, minimum = _lift(jnp.where), _lift(jnp.exp), _lift(jnp.minimum)
    st1, mm1, mm3, msel, gsum = _lift(_st1), _lift(_mm1), _lift(_mm3), _lift(_mm_sel), _lift(_group_sum)

    @pl.when(t == 0)
    def _():
        s_scr[...] = s0_ref[...]

    lb = lb_ref[...]
    cf = _Lock([x_ref[j, :, HW:2 * HW] for j in seqs])
    logf = _lift(jnp.log)(_lift(jax.nn.sigmoid)(cf) * (1.0 - lb) + lb)
    kk = _lift(jax.nn.sigmoid)(-cf) * (1.0 - lb)
    qq = _lift(_silu)(_Lock([x_ref[j, :, 0:HW] for j in seqs]))
    vv = _Lock([x_ref[j, :, 2 * HW:3 * HW] for j in seqs])
    if t_valid % c != 0:
        vm = (t * c + lax.broadcasted_iota(I32, (c, 1), 0) < t_valid).astype(F32)
        logf = logf * vm
        kk = kk * vm

    r64 = lax.broadcasted_iota(I32, (c, c), 0)
    q64 = lax.broadcasted_iota(I32, (c, c), 1)
    b = msel(r64 >= q64, logf)
    r = lax.broadcasted_iota(I32, (c, HW), 0)
    q = lax.broadcasted_iota(I32, (c, HW), 1) % 64

    amat = _Lock([jnp.zeros((c, HW), F32)] * SEQ_PER_STEP)
    half = c // 2
    while half >= 1:
        sz = 2 * half
        bnd = (r64 // sz) * sz + half - 1
        rl = msel(q64 == bnd, b)
        qf = qq * exp(minimum(b - rl, 0.0))
        kf = kk * exp(minimum(rl - b, 0.0))
        lmask = jnp.logical_and(r // sz == q // sz, jnp.logical_and(r % sz >= half, q % sz < half))
        amat = amat + where(lmask, st1(qf, kf, "nt"), 0.0)
        half //= 2

    st_old = _Lock([s_scr[j] for j in seqs])
    o = mm1(qq * exp(b), st_old, "nt") + st1(amat, vv)
    o = o + gsum(qq * kk, bd512_ref[0:HW, 0:HW]) * vv
    blast = b[c - 1:c, :]
    kd = kk * exp(blast - b)
    rb = lax.broadcasted_iota(I32, (HW, HW), 0) // 64
    cb = lax.broadcasted_iota(I32, (HW, HW), 1) // 64
    st_new = st_old * exp(blast) + where(rb == cb, mm3(vv, kd, "tn"), 0.0)
    ms_o = gsum(o * o, bd512_ref[0:HW, 0:HW]) * (1.0 / 64)
    out = o * _lift(lax.rsqrt)(ms_o + EPS) * cn_ref[...]
    for j in seqs:
        s_scr[j] = st_new.v[j]
        o_ref[j] = out.v[j]

    @pl.when(t == nt - 1)
    def _():
        sout_ref[...] = s_scr[...]


def _hgrn_st(z, nb, t_pad, t_valid, s0_bd, lb_row, cn_row, bd512):
    c = CHUNK
    ns = SEQ_PER_STEP
    assert c == 64 and nb % ns == 0
    nt = t_pad // c
    z3 = z.reshape(nb, t_pad, NZ)
    const = lambda *shape: pl.BlockSpec(shape, lambda b, t: (0,) * len(shape))
    per_seq = lambda *shape: pl.BlockSpec((ns,) + shape, lambda b, t: (b,) + (0,) * len(shape))
    o, s_out = pl.pallas_call(
        functools.partial(_hgrn_st_kernel, c=c, t_valid=t_valid, nt=nt),
        grid=(nb // ns, nt),
        in_specs=[pl.BlockSpec((ns, c, 768), lambda b, t: (b, t, Z_HG // 768)),
                  per_seq(HW, HW),
                  const(1, HW), const(1, HW), const(512, 512)],
        out_specs=[pl.BlockSpec((ns, c, HW), lambda b, t: (b, t, 0)), per_seq(HW, HW)],
        out_shape=[jax.ShapeDtypeStruct((nb, t_pad, HW), F32),
                   jax.ShapeDtypeStruct((nb, HW, HW), F32)],
        scratch_shapes=[pltpu.VMEM((ns, HW, HW), F32)],
        compiler_params=_cparams(("arbitrary", "arbitrary")),
        name="hgrn",
    )(z3, s0_bd, lb_row, cn_row, bd512)
    return o.reshape(nb * t_pad, HW), s_out


def _out_mlp_kernel(x_ref, oa_ref, ob_ref, oc_ref, wo_ref, g2_ref, w1_ref, w2_ref, y_ref, h_scr):
    @pl.when(pl.program_id(1) == 0)
    def _():
        x1 = (x_ref[...]
              + _dot(oa_ref[...].astype(BF16), wo_ref[0:A_WIDTH, :])
              + _dot(ob_ref[...].astype(BF16), wo_ref[A_WIDTH:A_WIDTH + 256, :])
              + _dot(oc_ref[...].astype(BF16), wo_ref[A_WIDTH + 256:A_WIDTH + 512, :]))
        y_ref[...] = x1
        ms = jnp.mean(x1 * x1, axis=-1, keepdims=True)
        h_scr[...] = (x1 * lax.rsqrt(ms + EPS) * g2_ref[...]).astype(BF16)

    u = jnp.maximum(_dot(h_scr[...], w1_ref[...]), 0.0)
    y_ref[...] += _dot((u * u).astype(BF16), w2_ref[...])


def _out_mlp(x, oa, ob, oc, wo, g2, w1, w2):
    m, d = x.shape
    tm = min(512, m)
    tf = 512
    nf = w1.shape[1] // tf
    tok = lambda w: pl.BlockSpec((tm, w), lambda i, j: (i, 0))
    return pl.pallas_call(
        _out_mlp_kernel,
        grid=(m // tm, nf),
        in_specs=[tok(d), tok(A_WIDTH), tok(256), tok(256),
                  pl.BlockSpec((d, d), lambda i, j: (0, 0)),
                  pl.BlockSpec((1, d), lambda i, j: (0, 0)),
                  pl.BlockSpec((d, tf), lambda i, j: (0, j)),
                  pl.BlockSpec((tf, d), lambda i, j: (j, 0))],
        out_specs=tok(d),
        out_shape=jax.ShapeDtypeStruct((m, d), F32),
        scratch_shapes=[pltpu.VMEM((tm, d), BF16)],
        compiler_params=_cparams(("parallel", "arbitrary")),
        name="out_mlp",
    )(x, oa, ob, oc, wo, g2, w1, w2)


def _relayout_w_in(w):
    d = w.shape[0]
    off = np.cumsum([0, 512, 512, 512, 1024, 16, 64, 256, 256, 256, 4, 4, 256, 256, 256, 256])
    (aq, ak, av, iq, iw, ik, bq, bk, bv, ba, bb, bg, cq, cf, ci, end) = [int(o) for o in off]
    zpad = lambda n: jnp.zeros((d, n), w.dtype)
    cols = [w[:, aq:iw],
            w[:, bg:cq],
            w[:, ik:bq], w[:, iw:ik], w[:, ba:bb], w[:, bb:bg], zpad(LANES - 88),
            zpad(Z_CONV - Z_MISC - LANES),
            w[:, bq:ba],
            w[:, cq:end]]
    out = jnp.concatenate(cols, axis=1)
    assert out.shape[1] == NZ
    return out.astype(BF16)


def _rope_tables(pos):
    pos = pos.astype(F32)
    half = HEAD_DIM // 2
    inv = ROPE_THETA ** (-jnp.arange(half, dtype=F32) / half)
    ang = pos[:, None] * inv[None, :]
    cos, sin = jnp.cos(ang), jnp.sin(ang)
    ca = jnp.tile(jnp.concatenate([cos, cos], -1), (1, 2))
    sa = jnp.tile(jnp.concatenate([-sin, sin], -1), (1, 2))
    half_i = IDX_ROPE // 2
    inv_i = ROPE_THETA ** (-jnp.arange(half_i, dtype=F32) / half_i)
    ang_i = pos[:, None] * inv_i[None, :]
    cos_i, sin_i = jnp.cos(ang_i), jnp.sin(ang_i)
    rest = IDX_DIM - IDX_ROPE
    one = jnp.ones((pos.shape[0], rest), F32)
    zero = jnp.zeros((pos.shape[0], rest), F32)
    ci = jnp.tile(jnp.concatenate([cos_i, cos_i, one], -1), (1, 2))
    si = jnp.tile(jnp.concatenate([-sin_i, sin_i, zero], -1), (1, 2))
    return ca, sa, ci, si


def _to_bd4(s):
    z = jnp.zeros_like(s[:, 0])
    rows = [jnp.concatenate([s[:, h] if g == h else z for g in range(NH)], axis=-1) for h in range(NH)]
    return jnp.concatenate(rows, axis=-2)


def _from_bd4(p):
    return jnp.stack([p[:, 64 * h:64 * (h + 1), 64 * h:64 * (h + 1)] for h in range(NH)], axis=1)


def _lane_row(vals, offset):
    return jnp.zeros((1, LANES), F32).at[0, offset:offset + vals.shape[0]].set(vals.astype(F32))


def _block_diag_ones(n):
    i = np.arange(n) // 64
    return jnp.asarray((i[:, None] == i[None, :]).astype(np.float32)).astype(BF16)


def kernel(x_prompt, x_sample, cache_k, cache_v, cache_idx_k, state_delta, state_delta_conv, state_hgrn,
           page_table, norm1_g, w_in, q_norm_g, k_norm_g, idx_k_ln_g, idx_k_ln_b, conv_w, a_log, dt_bias,
           delta_norm_g, lb_param, hgrn_norm_g, w_out, norm2_g, w_mlp_in, w_mlp_out):
    depth = w_in.shape[0]
    bp, s_len, d = x_prompt.shape
    bd_, t_dec, _ = x_sample.shape
    assert t_dec == 1 and d == D_MODEL
    past = page_table.shape[1] * PAGE_SIZE

    lb_all = jnp.cumsum(jax.nn.softmax(lb_param.astype(F32), axis=0), axis=0)
    lb_all = lb_all - lb_all[0:1]
    tabs_p = _rope_tables(jnp.arange(s_len))
    tabs_s = _rope_tables(past + jnp.arange(t_dec))
    bd512 = _block_diag_ones(512)
    bd128 = _block_diag_ones(LANES)

    xp = x_prompt.reshape(bp * s_len, d)
    xs = x_sample.reshape(bd_ * t_dec, d)
    t_pad_s = CHUNK
    zeros_buf = jnp.zeros((bp, CONV_W - 1, B_CONV), F32)
    zeros_state = jnp.zeros((bp, HW, HW), F32)

    outs_p = {k: [] for k in ("k", "v", "ik", "ds", "dc", "hs")}
    outs_s = {k: [] for k in ("k", "v", "ik", "ds", "dc", "hs")}
    for l in range(depth):
        w_in_l = _relayout_w_in(w_in[l])
        g1 = norm1_g[l].reshape(1, d)
        qg = jnp.tile(q_norm_g[l], A_HEADS).reshape(1, A_WIDTH)
        kg = jnp.tile(k_norm_g[l], A_HEADS).reshape(1, A_WIDTH)
        ikg = _lane_row(idx_k_ln_g[l], 0)
        ikb = _lane_row(idx_k_ln_b[l], 0)
        al_row = _lane_row(a_log[l], M_BA)
        dt_row = _lane_row(dt_bias[l], M_BA)
        dn_row = jnp.tile(delta_norm_g[l], NH).reshape(1, HW).astype(F32)
        cn_row = jnp.tile(hgrn_norm_g[l], NH).reshape(1, HW).astype(F32)
        lb_row = lb_all[l].reshape(1, 256)
        wo = w_out[l].astype(BF16)
        g2 = norm2_g[l].reshape(1, d)
        w1 = w_mlp_in[l].astype(BF16)
        w2 = w_mlp_out[l].astype(BF16)
        cw = conv_w[l]

        z = _norm_matmul(xp, g1, w_in_l, 768)
        pp = _prep_t(z, tabs_p, bp, s_len, qg, kg, ikg, ikb, bd512)
        oa = _attn_prompt(pp, bp, s_len)
        ob, ds, dc = _delta_st(z, bp, s_len, s_len, zeros_buf, zeros_state, cw, al_row, dt_row, dn_row, bd512)
        oc, hs = _hgrn_st(z, bp, s_len, s_len, zeros_state, lb_row, cn_row, bd512)
        xp = _out_mlp(xp, oa, ob, oc, wo, g2, w1, w2)
        heads_last = lambda a: jnp.transpose(a.reshape(bp, A_HEADS, HEAD_DIM, s_len), (0, 3, 1, 2))
        outs_p["k"].append(heads_last(pp["kt"]))
        outs_p["v"].append(heads_last(pp["vt"]))
        outs_p["ik"].append(jnp.transpose(pp["ikt"], (0, 2, 1)))
        outs_p["ds"].append(_from_bd4(ds))
        outs_p["dc"].append(dc)
        outs_p["hs"].append(jnp.swapaxes(_from_bd4(hs), -1, -2))

        zs = _norm_matmul(xs, g1, w_in_l, 768)
        ps = _prep(zs, tabs_s, qg, kg, ikg, ikb, bd512)
        oa_s = _attn_sample(ps, cache_k, cache_v, cache_idx_k, page_table, l)
        zs_pad = jnp.pad(zs.reshape(bd_, 1, NZ), ((0, 0), (0, t_pad_s - 1), (0, 0))).reshape(bd_ * t_pad_s, NZ)
        ob_s, ds_s, dc_s = _delta_st(zs_pad, bd_, t_pad_s, 1, state_delta_conv[l].astype(F32),
                                     _to_bd4(state_delta[l].astype(F32)), cw, al_row, dt_row, dn_row, bd512)
        oc_s, hs_s = _hgrn_st(zs_pad, bd_, t_pad_s, 1, _to_bd4(jnp.swapaxes(state_hgrn[l].astype(F32), -1, -2)),
                              lb_row, cn_row, bd512)
        ob_s = ob_s.reshape(bd_, t_pad_s, 256)[:, 0]
        oc_s = oc_s.reshape(bd_, t_pad_s, 256)[:, 0]
        xs = _out_mlp(xs, oa_s, ob_s, oc_s, wo, g2, w1, w2)
        outs_s["k"].append(ps["kf"].reshape(bd_, t_dec, A_HEADS, HEAD_DIM))
        outs_s["v"].append(ps["vf"].reshape(bd_, t_dec, A_HEADS, HEAD_DIM))
        outs_s["ik"].append(ps["ikf"].reshape(bd_, t_dec, IDX_DIM))
        outs_s["ds"].append(_from_bd4(ds_s))
        outs_s["dc"].append(dc_s)
        outs_s["hs"].append(jnp.swapaxes(_from_bd4(hs_s), -1, -2))

    st = lambda o, k: jnp.stack(o[k])
    return (xp.reshape(bp, s_len, d), xs.reshape(bd_, t_dec, d),
            st(outs_p, "k"), st(outs_p, "v"), st(outs_p, "ik"), st(outs_p, "ds"), st(outs_p, "dc"), st(outs_p, "hs"),
            st(outs_s, "k"), st(outs_s, "v"), st(outs_s, "ik"), st(outs_s, "ds"), st(outs_s, "dc"), st(outs_s, "hs"))
```

```python
import functools

import numpy as np
import jax
import jax.numpy as jnp
from jax import lax
from jax.experimental import pallas as pl
from jax.experimental.pallas import tpu as pltpu

F32 = jnp.float32
BF16 = jnp.bfloat16
I32 = jnp.int32
I16 = jnp.int16
HIGHEST = lax.Precision.HIGHEST

D_MODEL = 1024
HEAD_DIM = 64
A_HEADS = 8
A_WIDTH = A_HEADS * HEAD_DIM
IDX_HEADS = 16
IDX_DIM = 64
IDX_ROPE = 32
TOPK_MAX = 256
B_HEADS = 4
B_DK = 64
B_CONV = 768
CONV_W = 4
C_HEADS = 4
D_FF = 4 * D_MODEL
ROPE_THETA = 10000.0
EPS = 1e-6
PAGE_SIZE = 128
CHUNK = 64

LANES = 128
VMEM_LIMIT = 48 * 1024 * 1024

Z_IQ = 3 * A_WIDTH
Z_BG = Z_IQ + IDX_HEADS * IDX_DIM
Z_MISC = Z_BG + 256
Z_CONV = 3072
Z_HG = Z_CONV + B_CONV
NZ = Z_HG + 768
M_IW = 64
M_BA = 80
M_BB = 84

NEG = -1e30
INT_MIN = -2 ** 31
LOG2E = 1.4426950408889634
KSPLIT = 4


def _cparams(sem):
    return pltpu.CompilerParams(dimension_semantics=sem, vmem_limit_bytes=VMEM_LIMIT)


def _dot(a, b, precision=None):
    return jnp.dot(a, b, preferred_element_type=F32, precision=precision)


def _dot_nt(a, b, precision=None):
    return lax.dot_general(a, b, (((1,), (1,)), ((), ())), preferred_element_type=F32, precision=precision)


def _dot_tn(a, b, precision=None):
    return lax.dot_general(a, b, (((0,), (0,)), ((), ())), preferred_element_type=F32, precision=precision)


_DIMS = {"nn": (((1,), (0,)), ((), ())), "nt": (((1,), (1,)), ((), ())), "tn": (((0,), (0,)), ((), ()))}


def _bdot(a, b, form):
    return lax.dot_general(a, b, _DIMS[form], preferred_element_type=F32)


def _split2(x):
    hi = x.astype(BF16)
    return hi, (x - hi.astype(F32)).astype(BF16)


def _split3(x):
    hi = x.astype(BF16)
    r = x - hi.astype(F32)
    mid = r.astype(BF16)
    return hi, mid, (r - mid.astype(F32)).astype(BF16)


def _mm1(a, b, form="nn"):
    return _bdot(a.astype(BF16), b.astype(BF16), form)


def _mm3(a, b, form="nn"):
    ah, al = _split2(a)
    bh, bl = _split2(b)
    return _bdot(ah, bh, form) + (_bdot(ah, bl, form) + _bdot(al, bh, form))


def _mm_sel(a01, b, form="nn"):
    a = a01.astype(BF16)
    h, m, l = _split3(b)
    return _bdot(a, h, form) + (_bdot(a, m, form) + _bdot(a, l, form))


def _group_sum(y, bd):
    hi = y.astype(BF16)
    lo = (y - hi.astype(F32)).astype(BF16)
    return _dot(hi, bd) + _dot(lo, bd)


def _to_key(x):
    bits = lax.bitcast_convert_type(x, I32)
    return jnp.where(bits < 0, bits ^ jnp.int32(0x7FFFFFFF), bits)


def _silu(x):
    return x * jax.nn.sigmoid(x)


def _softplus(x):
    return jnp.maximum(x, 0.0) + jnp.log(1.0 + jnp.exp(-jnp.abs(x)))


def _norm_matmul_kernel(x_ref, g_ref, w_ref, o_ref, h_scr):
    @pl.when(pl.program_id(1) == 0)
    def _():
        x = x_ref[...]
        ms = jnp.mean(x * x, axis=-1, keepdims=True)
        h_scr[...] = (x * lax.rsqrt(ms + EPS) * g_ref[...]).astype(BF16)

    o_ref[...] = _dot(h_scr[...], w_ref[...])


def _norm_matmul(x, g, w, tn):
    m, d = x.shape
    n = w.shape[1]
    tm = min(512, m)
    return pl.pallas_call(
        _norm_matmul_kernel,
        grid=(m // tm, n // tn),
        in_specs=[pl.BlockSpec((tm, d), lambda i, j: (i, 0)),
                  pl.BlockSpec((1, d), lambda i, j: (0, 0)),
                  pl.BlockSpec((d, tn), lambda i, j: (0, j))],
        out_specs=pl.BlockSpec((tm, tn), lambda i, j: (i, j)),
        out_shape=jax.ShapeDtypeStruct((m, n), F32),
        scratch_shapes=[pltpu.VMEM((tm, d), BF16)],
        compiler_params=_cparams(("parallel", "arbitrary")),
        name="norm_matmul",
    )(x, g, w)


def _prep_values(za_ref, zm_ref, ca_ref, sa_ref, ci_ref, si_ref, qg_ref, kg_ref, ikg_ref, ikb_ref, bd_ref):
    tm = za_ref.shape[0]
    bd = bd_ref[...]
    ca = jnp.concatenate([ca_ref[...]] * 4, axis=1)
    sa = jnp.concatenate([sa_ref[...]] * 4, axis=1)
    lane_a = lax.broadcasted_iota(I32, (tm, A_WIDTH), 1)
    lo_half = (lane_a % HEAD_DIM) < (HEAD_DIM // 2)
    lane128 = lax.broadcasted_iota(I32, (tm, LANES), 1)
    left = lane128 < 64

    def norm_rope(x, g):
        ss = _group_sum(x * x, bd)
        y = x * lax.rsqrt(ss * (1.0 / HEAD_DIM) + EPS) * g
        sw = jnp.where(lo_half, pltpu.roll(y, A_WIDTH - 32, 1), pltpu.roll(y, 32, 1))
        return y * ca + sw * sa

    q = norm_rope(za_ref[:, 0:A_WIDTH], qg_ref[...]) * (HEAD_DIM ** -0.5 * LOG2E)
    k = norm_rope(za_ref[:, A_WIDTH:2 * A_WIDTH], kg_ref[...])
    v = za_ref[:, 2 * A_WIDTH:3 * A_WIDTH]

    nq = IDX_HEADS * IDX_DIM
    xi = za_ref[:, Z_IQ:Z_IQ + nq]
    lane_i = lax.broadcasted_iota(I32, (tm, nq), 1)
    ci = jnp.concatenate([ci_ref[...]] * 8, axis=1)
    si = jnp.concatenate([si_ref[...]] * 8, axis=1)
    swi = jnp.where((lane_i % IDX_DIM) < 16, pltpu.roll(xi, nq - 16, 1), pltpu.roll(xi, 16, 1))
    iq = xi * ci + swi * si

    m = zm_ref[...]
    mu = jnp.sum(jnp.where(left, m, 0.0), axis=-1, keepdims=True) * (1.0 / IDX_DIM)
    xc = jnp.where(left, m - mu, 0.0)
    var = jnp.sum(xc * xc, axis=-1, keepdims=True) * (1.0 / IDX_DIM)
    y = xc * lax.rsqrt(var + EPS) * ikg_ref[...] + ikb_ref[...]
    swk = jnp.where(lane128 < 16, pltpu.roll(y, LANES - 16, 1), pltpu.roll(y, 16, 1))
    ikr = y * ci_ref[...] + swk * si_ref[...]
    ikdup = jnp.where(left, ikr, pltpu.roll(ikr, 64, 1)).astype(BF16)
    misc = m * (IDX_HEADS ** -0.5 * IDX_DIM ** -0.5)
    return q, k, v, iq, ikr, ikdup, misc


def _prep_kernel(*refs):
    ins, (qpad_ref, kf_ref, vf_ref, iqpad_ref, ikf_ref, ikdup_ref, misc_ref) = refs[:11], refs[11:]
    q, k, v, iq, ikr, ikdup, misc = _prep_values(*ins)
    left = lax.broadcasted_iota(I32, (q.shape[0], LANES), 1) < 64
    for h in range(A_HEADS):
        keep = left if h % 2 == 0 else jnp.logical_not(left)
        qpad_ref[h] = jnp.where(keep, q[:, (h // 2) * LANES:(h // 2 + 1) * LANES], 0.0).astype(BF16)
    for h in range(IDX_HEADS):
        keep = left if h % 2 == 0 else jnp.logical_not(left)
        iqpad_ref[h] = jnp.where(keep, iq[:, (h // 2) * LANES:(h // 2 + 1) * LANES], 0.0).astype(BF16)
    kf_ref[...] = k
    vf_ref[...] = v
    ikf_ref[...] = ikr[:, 0:IDX_DIM]
    ikdup_ref[...] = ikdup
    misc_ref[...] = misc


def _prep_t_kernel(*refs):
    ins, (qt_ref, kt_ref, kb_ref, vt_ref, vaug_ref, iqt_ref, ikt_ref, ikdup_ref, iwt_ref) = refs[:11], refs[11:]
    q, k, v, iq, ikr, ikdup, misc = _prep_values(*ins)
    tm = q.shape[0]
    top = lax.broadcasted_iota(I32, (LANES, tm), 0) < 64
    qt = q.T
    vt = v.T
    iqt = iq.T
    for h in range(A_HEADS):
        keep = top if h % 2 == 0 else jnp.logical_not(top)
        rows = slice((h // 2) * LANES, (h // 2 + 1) * LANES)
        qt_ref[h] = jnp.where(keep, qt[rows, :], 0.0).astype(BF16)
        vaug_ref[h] = jnp.where(keep, vt[rows, :], 1.0).astype(BF16)
    for h in range(IDX_HEADS):
        keep = top if h % 2 == 0 else jnp.logical_not(top)
        iqt_ref[h] = jnp.where(keep, iqt[(h // 2) * LANES:(h // 2 + 1) * LANES, :], 0.0).astype(BF16)
    kt_ref[...] = k.T
    kb_ref[...] = k.astype(BF16)
    vt_ref[...] = vt
    ikt_ref[...] = ikr.T[0:IDX_DIM, :]
    ikdup_ref[...] = ikdup
    iwt_ref[...] = misc.T[M_IW:M_IW + IDX_HEADS, :]


def _prep_in_specs(tm, tab_spec):
    row = lambda w: pl.BlockSpec((1, w), lambda i: (0, 0))
    return [pl.BlockSpec((tm, Z_BG), lambda i: (i, 0)),
            pl.BlockSpec((tm, LANES), lambda i: (i, Z_MISC // LANES)),
            tab_spec, tab_spec, tab_spec, tab_spec,
            row(A_WIDTH), row(A_WIDTH), row(LANES), row(LANES),
            pl.BlockSpec((A_WIDTH, A_WIDTH), lambda i: (0, 0))]


def _prep(z, tabs, qg, kg, ikg, ikb, bd512):
    m = z.shape[0]
    tm = m
    tab_spec = pl.BlockSpec((1, LANES), lambda i: (0, 0))
    tok = lambda w: pl.BlockSpec((tm, w), lambda i: (i, 0))
    outs = pl.pallas_call(
        _prep_kernel,
        grid=(m // tm,),
        in_specs=_prep_in_specs(tm, tab_spec),
        out_specs=[pl.BlockSpec((A_HEADS, tm, LANES), lambda i: (0, i, 0)),
                   tok(A_WIDTH), tok(A_WIDTH),
                   pl.BlockSpec((IDX_HEADS, tm, LANES), lambda i: (0, i, 0)),
                   tok(IDX_DIM), tok(LANES), tok(LANES)],
        out_shape=[jax.ShapeDtypeStruct((A_HEADS, m, LANES), BF16),
                   jax.ShapeDtypeStruct((m, A_WIDTH), F32),
                   jax.ShapeDtypeStruct((m, A_WIDTH), F32),
                   jax.ShapeDtypeStruct((IDX_HEADS, m, LANES), BF16),
                   jax.ShapeDtypeStruct((m, IDX_DIM), F32),
                   jax.ShapeDtypeStruct((m, LANES), BF16),
                   jax.ShapeDtypeStruct((m, LANES), F32)],
        compiler_params=_cparams(("parallel",)),
        name="prep",
    )(z, z, *tabs, qg, kg, ikg, ikb, bd512)
    names = ("qpad", "kf", "vf", "iqpad", "ikf", "ikdup", "misc")
    return dict(zip(names, outs))


def _prep_t(z, tabs, nb, t_seq, qg, kg, ikg, ikb, bd512):
    m = z.shape[0]
    tm = min(256, t_seq)
    assert t_seq % tm == 0
    nblk = t_seq // tm
    tab_spec = pl.BlockSpec((tm, LANES), lambda i: (i % nblk, 0))
    feat = lambda f: pl.BlockSpec((None, f, tm), lambda i: (i // nblk, 0, i % nblk))
    heads = lambda nh: pl.BlockSpec((nh, LANES, tm), lambda i: (0, 0, i))
    outs = pl.pallas_call(
        _prep_t_kernel,
        grid=(m // tm,),
        in_specs=_prep_in_specs(tm, tab_spec),
        out_specs=[heads(A_HEADS), feat(A_WIDTH),
                   pl.BlockSpec((tm, A_WIDTH), lambda i: (i, 0)),
                   feat(A_WIDTH), heads(A_HEADS), heads(IDX_HEADS), feat(IDX_DIM),
                   pl.BlockSpec((tm, LANES), lambda i: (i, 0)),
                   pl.BlockSpec((IDX_HEADS, tm), lambda i: (0, i))],
        out_shape=[jax.ShapeDtypeStruct((A_HEADS, LANES, m), BF16),
                   jax.ShapeDtypeStruct((nb, A_WIDTH, t_seq), F32),
                   jax.ShapeDtypeStruct((m, A_WIDTH), BF16),
                   jax.ShapeDtypeStruct((nb, A_WIDTH, t_seq), F32),
                   jax.ShapeDtypeStruct((A_HEADS, LANES, m), BF16),
                   jax.ShapeDtypeStruct((IDX_HEADS, LANES, m), BF16),
                   jax.ShapeDtypeStruct((nb, IDX_DIM, t_seq), F32),
                   jax.ShapeDtypeStruct((m, LANES), BF16),
                   jax.ShapeDtypeStruct((IDX_HEADS, m), F32)],
        compiler_params=_cparams(("parallel",)),
        name="prep_t",
    )(z, z, *tabs, qg, kg, ikg, ikb, bd512)
    names = ("qt", "kt", "kb", "vt", "vaug", "iqt", "ikt", "ikdup", "iwt")
    return dict(zip(names, outs))


def _attn_prompt_kernel(qb_tab, kb_tab, nkb_tab, qt_ref, iqt_ref, iwt_ref, ik_ref, k_ref, vaug_ref, o_ref,
                        key_scr, thr_scr, m_scr, acc_scr, iq2_scr, k16_scr, l16_scr, lo_scr, *, tq, tk, topk):
    s = pl.program_id(1)
    qb = qb_tab[s]
    kb = kb_tab[s]
    nkb = nkb_tab[s]
    koff = lax.broadcasted_iota(I32, (tk, tq), 0)
    tpos = qb * tq + lax.broadcasted_iota(I32, (tk, tq), 1)

    @pl.when(kb == 0)
    def _():
        iwt = iwt_ref[...]
        for j in range(IDX_HEADS // 2):
            iq2_scr[j] = jnp.concatenate([iqt_ref[2 * j], iqt_ref[2 * j + 1]], axis=1)

        def score_chunk(c, carry):
            ikc = ik_ref[pl.ds(pl.multiple_of(c * tk, tk), tk), :]
            acc = jnp.zeros((tk, tq), F32)
            for j in range(IDX_HEADS // 2):
                s2 = _dot(ikc, iq2_scr[j])
                acc = acc + jnp.maximum(s2[:, 0:tq], 0.0) * iwt[2 * j:2 * j + 1, :]
                acc = acc + jnp.maximum(s2[:, tq:2 * tq], 0.0) * iwt[2 * j + 1:2 * j + 2, :]
            acc = jnp.where(c * tk + koff <= tpos, acc, -jnp.inf)
            key = _to_key(acc)
            key_scr[c] = key
            k16_scr[c] = (key >> 16).astype(I16)
            return carry

        lax.fori_loop(0, nkb, score_chunk, 0)

        def count16(src_scr, cand, strict=False):
            c16 = cand.astype(I16)

            def body(c, cnt):
                hit = src_scr[c] > c16 if strict else src_scr[c] >= c16
                one = jnp.where(hit, jnp.int16(1), jnp.int16(0))
                for g in range(tk // 16):
                    cnt = cnt + one[g * 16:(g + 1) * 16, :]
                return cnt

            cnt = lax.fori_loop(0, nkb, body, jnp.zeros((16, tq), I16))
            return jnp.sum(cnt.astype(I32), axis=0, keepdims=True)

        def search16(count_fn, opn):
            zero = jnp.zeros((1, tq), I32)
            c0 = count_fn(zero)
            t0 = jnp.where(c0 >= topk, zero, jnp.full((1, tq), -32768, I32))
            opn = jnp.where(c0 == topk, 0, opn)

            def cond(st):
                i, _, _, n_open = st
                return jnp.logical_and(i < 15, n_open > 0)

            def bit_step(st):
                i, t, o, _ = st
                cand = t | jnp.left_shift(jnp.int32(1), 14 - i)
                cnt = count_fn(cand)
                t = jnp.where(cnt >= topk, cand, t)
                o = jnp.where(cnt == topk, 0, o)
                return i + 1, t, o, jnp.sum(o)

            _, t, opn, _ = lax.while_loop(cond, bit_step, (jnp.int32(0), t0, opn, jnp.sum(opn)))
            return t, opn

        n_adm = qb * tq + lax.broadcasted_iota(I32, (1, tq), 1) + 1
        open0 = (n_adm >= topk).astype(I32)
        hi, opn = search16(lambda cand: count16(k16_scr, cand), open0)
        lo_scr[...] = jnp.full((1, tq), -32768, I32)

        @pl.when(jnp.sum(opn) > 0)
        def _():
            def low_chunk(c, carry):
                key = key_scr[c]
                low = (key & 0xFFFF) - 32768
                l16_scr[c] = jnp.where((key >> 16) == hi, low, -32768).astype(I16)
                return carry

            lax.fori_loop(0, nkb, low_chunk, 0)
            above = count16(k16_scr, hi, strict=True)
            lo, _ = search16(lambda cand: above + count16(l16_scr, cand), opn)
            lo_scr[...] = lo

        thr_scr[...] = hi * 65536 + (lo_scr[...] + 32768)
        m_scr[...] = jnp.full(m_scr.shape, NEG, F32)
        acc_scr[...] = jnp.zeros(acc_scr.shape, F32)

    sel = jnp.logical_and(key_scr[kb] >= thr_scr[...], kb * tk + koff <= tpos)
    bias = jnp.where(sel, 0.0, NEG)
    ts = tk // KSPLIT
    for u in range(KSPLIT):
        rows = slice(u * ts, (u + 1) * ts)
        for h in range(A_HEADS):
            p0 = (h // 2) * LANES
            sc = _dot(k_ref[rows, p0:p0 + LANES], qt_ref[h]) + bias[rows, :]
            m_old = m_scr[h:h + 1, :]
            m_new = jnp.maximum(m_old, jnp.max(sc, axis=0, keepdims=True))
            alpha = jnp.exp2(m_old - m_new)
            p = jnp.exp2(sc - m_new).astype(BF16)
            acc_scr[h] = alpha * acc_scr[h] + _dot(vaug_ref[h, :, rows], p)
            m_scr[h:h + 1, :] = m_new

    @pl.when(kb == nkb - 1)
    def _():
        for p in range(A_HEADS // 2):
            a = acc_scr[2 * p]
            b = acc_scr[2 * p + 1]
            ot = jnp.concatenate([a[0:64, :] / a[64:65, :], b[64:128, :] / b[0:1, :]], axis=0)
            o_ref[:, p * LANES:(p + 1) * LANES] = ot.T


def _attn_prompt(pp, nb, s_len):
    tq, tk = 256, 512
    tk = min(tk, s_len)
    topk = min(TOPK_MAX, s_len // 4)
    nqb = s_len // tq
    nkblk = s_len // tk
    qb_l, kb_l, nkb_l = [], [], []
    for qb in range(nqb):
        nkb = ((qb + 1) * tq + tk - 1) // tk
        for kb in range(nkb):
            qb_l.append(qb)
            kb_l.append(kb)
            nkb_l.append(nkb)
    tabs = [jnp.asarray(np.array(a, np.int32)) for a in (qb_l, kb_l, nkb_l)]
    nsteps = len(qb_l)
    m = nb * s_len
    grid_spec = pltpu.PrefetchScalarGridSpec(
        num_scalar_prefetch=3,
        grid=(nb, nsteps),
        in_specs=[
            pl.BlockSpec((A_HEADS, LANES, tq), lambda b, s, qt, kt, nt: (0, 0, b * nqb + qt[s])),
            pl.BlockSpec((IDX_HEADS, LANES, tq), lambda b, s, qt, kt, nt: (0, 0, b * nqb + qt[s])),
            pl.BlockSpec((IDX_HEADS, tq), lambda b, s, qt, kt, nt: (0, b * nqb + qt[s])),
            pl.BlockSpec((s_len, LANES), lambda b, s, qt, kt, nt: (b, 0)),
            pl.BlockSpec((tk, A_WIDTH), lambda b, s, qt, kt, nt: (b * nkblk + kt[s], 0)),
            pl.BlockSpec((A_HEADS, LANES, tk), lambda b, s, qt, kt, nt: (0, 0, b * nkblk + kt[s])),
        ],
        out_specs=pl.BlockSpec((tq, A_WIDTH), lambda b, s, qt, kt, nt: (b * nqb + qt[s], 0)),
        scratch_shapes=[pltpu.VMEM((nkblk, tk, tq), I32),
                        pltpu.VMEM((1, tq), I32),
                        pltpu.VMEM((A_HEADS, tq), F32),
                        pltpu.VMEM((A_HEADS, LANES, tq), F32),
                        pltpu.VMEM((IDX_HEADS // 2, LANES, 2 * tq), BF16),
                        pltpu.VMEM((nkblk, tk, tq), I16),
                        pltpu.VMEM((nkblk, tk, tq), I16),
                        pltpu.VMEM((1, tq), I32)],
    )
    return pl.pallas_call(
        functools.partial(_attn_prompt_kernel, tq=tq, tk=tk, topk=topk),
        grid_spec=grid_spec,
        out_shape=jax.ShapeDtypeStruct((m, A_WIDTH), F32),
        compiler_params=_cparams(("arbitrary", "arbitrary")),
        name="attn_prompt",
    )(*tabs, pp["qt"], pp["iqt"], pp["iwt"], pp["ikdup"], pp["kb"], pp["vaug"])


def _sample_score_kernel(pt_ref, iq_ref, iw_ref, *rest, npg):
    page_refs = rest[:npg]
    o_ref = rest[npg]
    iq = iq_ref[...]
    w = iw_ref[...]
    for g in range(npg):
        pg = page_refs[g][...].astype(BF16)
        sc = _dot(iq, pg)
        o_ref[g:g + 1, :] = jnp.sum(jnp.maximum(sc, 0.0) * w, axis=0, keepdims=True)


def _sample_attn_kernel(pt_ref, sc_ref, iq_ref, iw_ref, ikn_ref, qbd_ref, kn_ref, vn_ref, *rest, npg, topk):
    k_refs = rest[:npg]
    v_refs = rest[npg:2 * npg]
    o_ref = rest[2 * npg]
    key_scr, thr_scr, nk_scr, m_scr, l_scr, acc_scr = rest[2 * npg + 1:]
    j = pl.program_id(1)
    nj = pl.num_programs(1)

    @pl.when(j == 0)
    def _():
        keys = _to_key(sc_ref[...])
        key_scr[...] = keys
        sn = jnp.sum(iq_ref[...].astype(F32) * ikn_ref[...].astype(F32), axis=1, keepdims=True)
        snew = jnp.sum(jnp.maximum(sn, 0.0) * iw_ref[...], axis=0, keepdims=True)
        knew = _to_key(snew)
        nk_scr[...] = knew

        def count_ge(cand):
            c = jnp.sum((keys >= cand).astype(I32), axis=1, keepdims=True)
            return jnp.sum(c, axis=0, keepdims=True) + (knew >= cand).astype(I32)

        thr = jnp.full((1, 1), INT_MIN, I32)
        zero = jnp.zeros((1, 1), I32)
        thr = jnp.where(count_ge(zero) >= topk, zero, thr)

        def bit_step(i, thr):
            cand = thr | jnp.left_shift(jnp.int32(1), 30 - i)
            return jnp.where(count_ge(cand) >= topk, cand, thr)

        thr_scr[...] = lax.fori_loop(0, 31, bit_step, thr)
        m_scr[...] = jnp.full(m_scr.shape, NEG, F32)
        l_scr[...] = jnp.zeros(l_scr.shape, F32)
        acc_scr[...] = jnp.zeros(acc_scr.shape, F32)

    qbd = qbd_ref[...]
    thr = thr_scr[...]
    sel = jnp.concatenate([key_scr[pl.ds(j * npg + g, 1), :] >= thr for g in range(npg)], axis=1)
    sc = jnp.concatenate([_dot(qbd, k_refs[g][...].astype(BF16)) for g in range(npg)], axis=1)
    sc = jnp.where(sel, sc, NEG)
    m_old = m_scr[...]
    m_new = jnp.maximum(m_old, jnp.max(sc, axis=1, keepdims=True))
    alpha = jnp.exp2(m_old - m_new)
    p = jnp.where(sel, jnp.exp2(sc - m_new), 0.0)
    l_scr[...] = alpha * l_scr[...] + jnp.sum(p, axis=1, keepdims=True)
    pv = jnp.zeros(acc_scr.shape, F32)
    for g in range(npg):
        pv = pv + _dot_nt(p[:, g * PAGE_SIZE:(g + 1) * PAGE_SIZE].astype(BF16), v_refs[g][...].astype(BF16))
    acc_scr[...] = alpha * acc_scr[...] + pv
    m_scr[...] = m_new

    @pl.when(j == nj - 1)
    def _():
        sel = nk_scr[...] >= thr
        sc = jnp.sum(qbd.astype(F32) * kn_ref[...].astype(BF16).astype(F32), axis=1, keepdims=True)
        sc = jnp.where(sel, sc, NEG)
        m_old = m_scr[...]
        m_new = jnp.maximum(m_old, sc)
        alpha = jnp.exp2(m_old - m_new)
        p = jnp.where(sel, jnp.exp2(sc - m_new), 0.0)
        l_new = alpha * l_scr[...] + p
        acc = alpha * acc_scr[...] + p.astype(BF16).astype(F32) * vn_ref[...].astype(BF16).astype(F32)
        o = acc / l_new
        hrow = lax.broadcasted_iota(I32, (A_HEADS, A_WIDTH), 0)
        hcol = lax.broadcasted_iota(I32, (A_HEADS, A_WIDTH), 1) // HEAD_DIM
        o_ref[...] = jnp.sum(jnp.where(hrow == hcol, o, 0.0), axis=0, keepdims=True)


def _attn_sample(pp, cache_k, cache_v, cache_ik, page_table, layer):
    nb, npages = page_table.shape
    npg_s = int(np.gcd(npages, 32))
    npg = int(np.gcd(npages, 16))
    topk = min(TOPK_MAX, (npages * PAGE_SIZE + 1) // 4)
    pt = page_table.reshape(-1).astype(I32)
    n_pool = cache_k.shape[1]
    ck = jnp.transpose(cache_k, (0, 1, 3, 4, 2)).reshape(cache_k.shape[0], n_pool, A_WIDTH, PAGE_SIZE)
    cv = jnp.transpose(cache_v, (0, 1, 3, 4, 2)).reshape(cache_v.shape[0], n_pool, A_WIDTH, PAGE_SIZE)
    cache_ik = jnp.transpose(cache_ik, (0, 1, 3, 2))

    iq = pp["iqpad"]
    iq = jnp.transpose(iq[:, :, :64] + iq[:, :, 64:], (1, 0, 2))
    iw = pp["misc"][:, M_IW:M_IW + IDX_HEADS].reshape(nb, IDX_HEADS, 1)
    ikn = pp["ikdup"][:, :IDX_DIM].reshape(nb, 1, IDX_DIM)
    qp = jnp.transpose(pp["qpad"], (1, 0, 2))
    qbd = jnp.zeros((nb, A_HEADS, A_HEADS // 2, LANES), BF16)
    for h in range(A_HEADS):
        qbd = qbd.at[:, h, h // 2, :].set(qp[:, h, :])
    qbd = qbd.reshape(nb, A_HEADS, A_WIDTH)
    kn = pp["kf"].reshape(nb, 1, A_WIDTH)
    vn = pp["vf"].reshape(nb, 1, A_WIDTH)

    def page_spec(width, g, per_step):
        return pl.BlockSpec((None, None, width, PAGE_SIZE),
                            lambda b, j, ptr: (layer, ptr[b * npages + j * per_step + g], 0, 0))

    per_b = lambda *shape: pl.BlockSpec((None,) + shape, lambda b, j, ptr: (b,) + (0,) * len(shape))

    scores = pl.pallas_call(
        functools.partial(_sample_score_kernel, npg=npg_s),
        grid_spec=pltpu.PrefetchScalarGridSpec(
            num_scalar_prefetch=1,
            grid=(nb, npages // npg_s),
            in_specs=[per_b(IDX_HEADS, IDX_DIM), per_b(IDX_HEADS, 1)]
                     + [page_spec(IDX_DIM, g, npg_s) for g in range(npg_s)],
            out_specs=pl.BlockSpec((None, npg_s, PAGE_SIZE), lambda b, j, ptr: (b, j, 0)),
        ),
        out_shape=jax.ShapeDtypeStruct((nb, npages, PAGE_SIZE), F32),
        compiler_params=_cparams(("arbitrary", "arbitrary")),
        name="sample_score",
    )(pt, iq, iw, *([cache_ik] * npg_s))

    out = pl.pallas_call(
        functools.partial(_sample_attn_kernel, npg=npg, topk=topk),
        grid_spec=pltpu.PrefetchScalarGridSpec(
            num_scalar_prefetch=1,
            grid=(nb, npages // npg),
            in_specs=[per_b(npages, PAGE_SIZE), per_b(IDX_HEADS, IDX_DIM), per_b(IDX_HEADS, 1),
                      per_b(1, IDX_DIM), per_b(A_HEADS, A_WIDTH), per_b(1, A_WIDTH), per_b(1, A_WIDTH)]
                     + [page_spec(A_WIDTH, g, npg) for g in range(npg)]
                     + [page_spec(A_WIDTH, g, npg) for g in range(npg)],
            out_specs=pl.BlockSpec((None, 1, A_WIDTH), lambda b, j, ptr: (b, 0, 0)),
            scratch_shapes=[pltpu.VMEM((npages, PAGE_SIZE), I32),
                            pltpu.VMEM((1, 1), I32),
                            pltpu.VMEM((1, 1), I32),
                            pltpu.VMEM((A_HEADS, 1), F32),
                            pltpu.VMEM((A_HEADS, 1), F32),
                            pltpu.VMEM((A_HEADS, A_WIDTH), F32)],
        ),
        out_shape=jax.ShapeDtypeStruct((nb, 1, A_WIDTH), F32),
        compiler_params=_cparams(("arbitrary", "arbitrary")),
        name="sample_attn",
    )(pt, scores, iq, iw, ikn, qbd, kn, vn, *([ck] * npg), *([cv] * npg))
    return out.reshape(nb, A_WIDTH)


def _unit_lower_inverse(mm, c):
    r = lax.broadcasted_iota(I32, (c, c), 0)
    q = lax.broadcasted_iota(I32, (c, c), 1)
    eye = (r == q).astype(F32)
    blk = min(16, c)
    pw = jnp.where(r // blk == q // blk, -mm, 0.0)
    inv = eye + pw
    k = 1
    while 2 * k < blk:
        pw = _mm3(pw, pw)
        inv = inv + _mm3(inv, pw)
        k *= 2
    sz = 2 * blk
    while sz <= c:
        half = sz // 2
        e = jnp.where(jnp.logical_and(r // sz == q // sz, r // half != q // half), mm, 0.0)
        inv = inv - _mm3(inv, _mm3(e, inv))
        sz *= 2
    return inv


def _delta_kernel(xc_ref, bg_ref, ms_ref, buf_ref, s0_ref, cw_ref, al_ref, dt_ref, dn_ref, bd512_ref, bd128_ref,
                  o_ref, sout_ref, nbuf_ref, xp_scr, s_scr, *, c, t_valid, nt):
    t = pl.program_id(1)

    @pl.when(t == 0)
    def _():
        xp_scr[5:8, :] = buf_ref[...]
        s_scr[...] = s0_ref[...]

    xp_scr[8:8 + c, :] = xc_ref[...]
    cw = cw_ref[...]
    y = (cw[0:1] * xp_scr[5:5 + c, :] + cw[1:2] * xp_scr[6:6 + c, :]
         + cw[2:3] * xp_scr[7:7 + c, :] + cw[3:4] * xp_scr[8:8 + c, :])

    tb, lt = (t_valid - 1) // c, (t_valid - 1) % c

    @pl.when(t == tb)
    def _():
        nbuf_ref[...] = xp_scr[8 + lt - 2:8 + lt + 1, :]

    hist = xp_scr[c + 5:c + 8, :]
    xp_scr[5:8, :] = hist

    y = _silu(y)
    ss = _group_sum(y[:, 0:512] * y[:, 0:512], bd512_ref[...])
    inv_n = lax.rsqrt(ss + EPS)
    qn = y[:, 0:256] * inv_n[:, 0:256] * (B_DK ** -0.5)
    kn = y[:, 256:512] * inv_n[:, 256:512]
    vv = y[:, 512:768]
    ms = ms_ref[...]
    g_all = -jnp.exp(al_ref[...]) * _softplus(ms + dt_ref[...])
    beta_all = jax.nn.sigmoid(ms)
    if t_valid % c != 0:
        vm = (t * c + lax.broadcasted_iota(I32, (c, 1), 0) < t_valid).astype(F32)
        g_all = g_all * vm
        beta_all = beta_all * vm
        kn = kn * vm
        vv = vv * vm

    r = lax.broadcasted_iota(I32, (c, c), 0)
    q = lax.broadcasted_iota(I32, (c, c), 1)
    incl = r >= q
    strict = r > q
    gcum_all = _mm_sel(incl, g_all)
    lane = lax.broadcasted_iota(I32, (c, LANES), 1)
    left = lane < 64
    ones = jnp.ones((c, LANES), BF16)
    row128 = lax.broadcasted_iota(I32, (LANES, LANES), 0)
    col128 = lax.broadcasted_iota(I32, (LANES, LANES), 1)
    bdmask = (row128 // 64) == (col128 // 64)

    for p in range(B_HEADS // 2):
        kp = kn[:, p * LANES:(p + 1) * LANES]
        qp = qn[:, p * LANES:(p + 1) * LANES]
        vp = vv[:, p * LANES:(p + 1) * LANES]
        s_old = s_scr[p]
        u_pair = jnp.zeros((c, LANES), F32)
        w_pair = jnp.zeros((c, LANES), F32)
        pmats, gcs = [], []
        for e in range(2):
            h = 2 * p + e
            hm = left if e == 0 else jnp.logical_not(left)
            gc = gcum_all[:, M_BA + h:M_BA + h + 1]
            bt = beta_all[:, M_BB + h:M_BB + h + 1]
            grow = _mm_sel(ones, jnp.where(lane == M_BA + h, gcum_all, 0.0), "nt")
            dec = jnp.exp(jnp.where(incl, gc - grow, 0.0))
            km = jnp.where(hm, kp, 0.0)
            kk = _mm1(km, kp, "nt")
            mm = jnp.where(strict, bt * kk * dec, 0.0)
            inv = _unit_lower_inverse(mm, c)
            rhs = jnp.concatenate([bt * jnp.where(hm, vp, 0.0), (bt * jnp.exp(gc)) * km], axis=1)
            uw = _mm3(inv, rhs)
            u_pair = u_pair + uw[:, 0:LANES]
            w_pair = w_pair + uw[:, LANES:2 * LANES]
            qk = _mm1(jnp.where(hm, qp, 0.0), kp, "nt")
            pmats.append(jnp.where(incl, qk * dec, 0.0))
            gcs.append(gc)
        gc_pair = jnp.where(left, gcs[0], gcs[1])
        delta = u_pair - _mm3(w_pair, s_old)
        o = _mm1(qp * jnp.exp(gc_pair), s_old)
        o = o + _mm1(pmats[0], jnp.where(left, delta, 0.0))
        o = o + _mm1(pmats[1], jnp.where(left, 0.0, delta))
        glast = gc_pair[c - 1:c, :]
        kd = kp * jnp.exp(glast - gc_pair)
        gl_rows = jnp.where(row128 < 64, jnp.exp(gcs[0][c - 1:c, :]), jnp.exp(gcs[1][c - 1:c, :]))
        s_scr[p] = gl_rows * s_old + jnp.where(bdmask, _mm3(kd, delta, "tn"), 0.0)
        ms_o = _group_sum(o * o, bd128_ref[...]) * (1.0 / 64)
        o_ref[:, p * LANES:(p + 1) * LANES] = (o * lax.rsqrt(ms_o + EPS) * dn_ref[...]
                                               * _silu(bg_ref[:, p * LANES:(p + 1) * LANES]))

    @pl.when(t == nt - 1)
    def _():
        sout_ref[...] = s_scr[...]


def _delta(z, nb, t_pad, t_valid, buf0, s0_pair, cw, al_row, dt_row, dn_row, bd512, bd128):
    c = CHUNK
    nt = t_pad // c
    m = nb * t_pad
    const = lambda *shape: pl.BlockSpec(shape, lambda b, t: (0,) * len(shape))
    return pl.pallas_call(
        functools.partial(_delta_kernel, c=c, t_valid=t_valid, nt=nt),
        grid=(nb, nt),
        in_specs=[pl.BlockSpec((c, B_CONV), lambda b, t: (b * nt + t, Z_CONV // B_CONV)),
                  pl.BlockSpec((c, 256), lambda b, t: (b * nt + t, Z_BG // 256)),
                  pl.BlockSpec((c, LANES), lambda b, t: (b * nt + t, Z_MISC // LANES)),
                  pl.BlockSpec((None, CONV_W - 1, B_CONV), lambda b, t: (b, 0, 0)),
                  pl.BlockSpec((None, 2, LANES, LANES), lambda b, t: (b, 0, 0, 0)),
                  const(CONV_W, B_CONV), const(1, LANES), const(1, LANES), const(1, LANES),
                  const(512, 512), const(LANES, LANES)],
        out_specs=[pl.BlockSpec((c, 256), lambda b, t: (b * nt + t, 0)),
                   pl.BlockSpec((None, 2, LANES, LANES), lambda b, t: (b, 0, 0, 0)),
                   pl.BlockSpec((None, CONV_W - 1, B_CONV), lambda b, t: (b, 0, 0))],
        out_shape=[jax.ShapeDtypeStruct((m, 256), F32),
                   jax.ShapeDtypeStruct((nb, 2, LANES, LANES), F32),
                   jax.ShapeDtypeStruct((nb, CONV_W - 1, B_CONV), F32)],
        scratch_shapes=[pltpu.VMEM((8 + c, B_CONV), F32), pltpu.VMEM((2, LANES, LANES), F32)],
        compiler_params=_cparams(("arbitrary", "arbitrary")),
        name="delta",
    )(z, z, z, buf0, s0_pair, cw, al_row, dt_row, dn_row, bd512, bd128)


def _hgrn_kernel(x_ref, s0_ref, lb_ref, cn_ref, bd128_ref, o_ref, sout_ref, s_scr, *, c, t_valid, nt):
    t = pl.program_id(1)

    @pl.when(t == 0)
    def _():
        s_scr[...] = s0_ref[...]

    lb = lb_ref[...]
    cf = x_ref[:, 256:512]
    logf = jnp.log(lb + (1.0 - lb) * jax.nn.sigmoid(cf))
    kk = (1.0 - lb) * jax.nn.sigmoid(-cf)
    qq = _silu(x_ref[:, 0:256])
    vv = x_ref[:, 512:768]
    if t_valid % c != 0:
        vm = (t * c + lax.broadcasted_iota(I32, (c, 1), 0) < t_valid).astype(F32)
        logf = logf * vm
        kk = kk * vm

    r = lax.broadcasted_iota(I32, (c, c), 0)
    q = lax.broadcasted_iota(I32, (c, c), 1)
    b = _mm_sel(r >= q, logf)
    lane = lax.broadcasted_iota(I32, (c, LANES), 1)
    left = lane < 64
    row128 = lax.broadcasted_iota(I32, (LANES, LANES), 0)
    col128 = lax.broadcasted_iota(I32, (LANES, LANES), 1)
    bdmask = (row128 // 64) == (col128 // 64)

    amats = [jnp.zeros((c, c), F32) for _ in range(C_HEADS)]
    half = c // 2
    while half >= 1:
        sz = 2 * half
        bnd = (r // sz) * sz + half - 1
        rl = _mm_sel(q == bnd, b)
        qf = qq * jnp.exp(jnp.minimum(b - rl, 0.0))
        kf = kk * jnp.exp(jnp.minimum(rl - b, 0.0))
        lmask = jnp.logical_and(r // sz == q // sz, jnp.logical_and(r % sz >= half, q % sz < half))
        for h in range(C_HEADS):
            p0 = (h // 2) * LANES
            hm = left if h % 2 == 0 else jnp.logical_not(left)
            a = _mm1(jnp.where(hm, qf[:, p0:p0 + LANES], 0.0), kf[:, p0:p0 + LANES], "nt")
            amats[h] = amats[h] + jnp.where(lmask, a, 0.0)
        half //= 2

    eb = jnp.exp(b)
    for p in range(C_HEADS // 2):
        sl = slice(p * LANES, (p + 1) * LANES)
        st_old = s_scr[p]
        qp, kp, vp, bp = qq[:, sl], kk[:, sl], vv[:, sl], b[:, sl]
        o = _mm1(qp * eb[:, sl], st_old, "nt")
        o = o + _mm1(amats[2 * p], jnp.where(left, vp, 0.0))
        o = o + _mm1(amats[2 * p + 1], jnp.where(left, 0.0, vp))
        o = o + _group_sum(qp * kp, bd128_ref[...]) * vp
        blast = bp[c - 1:c, :]
        kd = kp * jnp.exp(blast - bp)
        s_scr[p] = st_old * jnp.exp(blast) + jnp.where(bdmask, _mm3(vp, kd, "tn"), 0.0)
        ms_o = _group_sum(o * o, bd128_ref[...]) * (1.0 / 64)
        o_ref[:, sl] = o * lax.rsqrt(ms_o + EPS) * cn_ref[...]

    @pl.when(t == nt - 1)
    def _():
        sout_ref[...] = s_scr[...]


def _hgrn(z, nb, t_pad, t_valid, s0_pair, lb_row, cn_row, bd128):
    c = CHUNK
    nt = t_pad // c
    m = nb * t_pad
    const = lambda *shape: pl.BlockSpec(shape, lambda b, t: (0,) * len(shape))
    return pl.pallas_call(
        functools.partial(_hgrn_kernel, c=c, t_valid=t_valid, nt=nt),
        grid=(nb, nt),
        in_specs=[pl.BlockSpec((c, 768), lambda b, t: (b * nt + t, Z_HG // 768)),
                  pl.BlockSpec((None, 2, LANES, LANES), lambda b, t: (b, 0, 0, 0)),
                  const(1, 256), const(1, LANES), const(LANES, LANES)],
        out_specs=[pl.BlockSpec((c, 256), lambda b, t: (b * nt + t, 0)),
                   pl.BlockSpec((None, 2, LANES, LANES), lambda b, t: (b, 0, 0, 0))],
        out_shape=[jax.ShapeDtypeStruct((m, 256), F32),
                   jax.ShapeDtypeStruct((nb, 2, LANES, LANES), F32)],
        scratch_shapes=[pltpu.VMEM((2, LANES, LANES), F32)],
        compiler_params=_cparams(("arbitrary", "arbitrary")),
        name="hgrn",
    )(z, s0_pair, lb_row, cn_row, bd128)


NH = 4
HW = NH * 64


def _bd(y):
    hb = lax.broadcasted_iota(I32, y.shape, 1) // 64
    zero = jnp.zeros_like(y)
    return jnp.concatenate([jnp.where(hb == h, y, zero) for h in range(NH)], axis=0)


def _st1(x, y, form="nn"):
    return _bdot(x.astype(BF16), _bd(y.astype(BF16)), form)


def _st3(x, y):
    xh, xl = _split2(x)
    yh, yl = _split2(y)
    bh, bl = _bd(yh), _bd(yl)
    return _bdot(xh, bh, "nn") + (_bdot(xh, bl, "nn") + _bdot(xl, bh, "nn"))


def _st_unit_lower_inverse(mm, r, q):
    pw = jnp.where(r // 16 == q // 16, -mm, 0.0)
    inv = (r == q).astype(F32) + pw
    k = 1
    while 2 * k < 16:
        pw = _st3(pw, pw)
        inv = inv + _st3(inv, pw)
        k *= 2
    for sz in (32, 64):
        half = sz // 2
        e = jnp.where(jnp.logical_and(r // sz == q // sz, r // half != q // half), mm, 0.0)
        inv = inv - _st3(inv, _st3(e, inv))
    return inv


SEQ_PER_STEP = 4


class _Lock:
    def __init__(self, vals):
        self.v = list(vals)

    def _bin(self, other, f):
        ov = other.v if isinstance(other, _Lock) else [other] * len(self.v)
        return _Lock([f(a, b) for a, b in zip(self.v, ov)])

    def __add__(self, o):
        return self._bin(o, lambda a, b: a + b)

    def __radd__(self, o):
        return self._bin(o, lambda a, b: b + a)

    def __sub__(self, o):
        return self._bin(o, lambda a, b: a - b)

    def __rsub__(self, o):
        return self._bin(o, lambda a, b: b - a)

    def __mul__(self, o):
        return self._bin(o, lambda a, b: a * b)

    def __rmul__(self, o):
        return self._bin(o, lambda a, b: b * a)

    def __neg__(self):
        return _Lock([-a for a in self.v])

    def __getitem__(self, idx):
        return _Lock([a[idx] for a in self.v])


def _lift(f):
    def g(*args, **kw):
        n = max(len(a.v) for a in args if isinstance(a, _Lock))
        pick = lambda a, i: a.v[i] if isinstance(a, _Lock) else a
        return _Lock([f(*[pick(a, i) for a in args], **kw) for i in range(n)])
    return g


def _st_unit_lower_inverse_l(mm, r, q):
    st3, where = _lift(_st3), _lift(jnp.where)
    pw = where(r // 16 == q // 16, -mm, 0.0)
    inv = pw + (r == q).astype(F32)
    k = 1
    while 2 * k < 16:
        pw = st3(pw, pw)
        inv = inv + st3(inv, pw)
        k *= 2
    for sz in (32, 64):
        half = sz // 2
        e = where(jnp.logical_and(r // sz == q // sz, r // half != q // half), mm, 0.0)
        inv = inv - st3(inv, st3(e, inv))
    return inv


def _delta_st_kernel(xc_ref, bg_ref, ms_ref, buf_ref, s0_ref, cw_ref, al_ref, dt_ref, dn_ref, bd512_ref,
                     o_ref, sout_ref, nbuf_ref, xp_scr, s_scr, *, c, t_valid, nt):
    t = pl.program_id(1)
    ns = SEQ_PER_STEP
    seqs = range(ns)
    where, exp = _lift(jnp.where), _lift(jnp.exp)
    st1, st3, mm1, mm3, msel = _lift(_st1), _lift(_st3), _lift(_mm1), _lift(_mm3), _lift(_mm_sel)

    @pl.when(t == 0)
    def _():
        xp_scr[:, 5:8, :] = buf_ref[...]
        s_scr[...] = s0_ref[...]

    xp_scr[:, 8:8 + c, :] = xc_ref[...]
    cw = cw_ref[...]
    y = _Lock([cw[0:1] * xp_scr[j, 5:5 + c, :] + cw[1:2] * xp_scr[j, 6:6 + c, :]
               + cw[2:3] * xp_scr[j, 7:7 + c, :] + cw[3:4] * xp_scr[j, 8:8 + c, :] for j in seqs])

    tb, lt = (t_valid - 1) // c, (t_valid - 1) % c

    @pl.when(t == tb)
    def _():
        nbuf_ref[...] = xp_scr[:, 8 + lt - 2:8 + lt + 1, :]

    hist = xp_scr[:, c + 5:c + 8, :]
    xp_scr[:, 5:8, :] = hist

    y = _lift(_silu)(y)
    ss = _lift(_group_sum)(y[:, 0:512] * y[:, 0:512], bd512_ref[...])
    inv_n = _lift(lax.rsqrt)(ss + EPS)
    qn = y[:, 0:HW] * inv_n[:, 0:HW] * (B_DK ** -0.5)
    kn = y[:, HW:2 * HW] * inv_n[:, HW:2 * HW]
    vv = y[:, 2 * HW:3 * HW]
    ms = _Lock([ms_ref[j] for j in seqs])
    g_all = _lift(_softplus)(ms + dt_ref[...]) * (-jnp.exp(al_ref[...]))
    beta_all = _lift(jax.nn.sigmoid)(ms)
    if t_valid % c != 0:
        vm = (t * c + lax.broadcasted_iota(I32, (c, 1), 0) < t_valid).astype(F32)
        g_all = g_all * vm
        beta_all = beta_all * vm
        kn = kn * vm
        vv = vv * vm

    r = lax.broadcasted_iota(I32, (c, HW), 0)
    lane = lax.broadcasted_iota(I32, (c, HW), 1)
    q = lane % 64
    hb = lane // 64
    incl = r >= q
    strict = r > q
    r64 = lax.broadcasted_iota(I32, (c, c), 0)
    q64 = lax.broadcasted_iota(I32, (c, c), 1)
    gcum_all = msel(r64 >= q64, g_all)
    lane128 = lax.broadcasted_iota(I32, (c, LANES), 1)
    gc = _Lock([jnp.zeros((c, HW), F32)] * ns)
    bt = _Lock([jnp.zeros((c, HW), F32)] * ns)
    for h in range(NH):
        gc = where(hb == h, gcum_all[:, M_BA + h:M_BA + h + 1], gc)
        bt = where(hb == h, beta_all[:, M_BB + h:M_BB + h + 1], bt)
    gsel = _lift(lambda g: jnp.concatenate([jnp.where(lane128 == M_BA + h, g, 0.0) for h in range(NH)],
                                           axis=0))(gcum_all)
    grow = msel(jnp.ones((c, LANES), BF16), gsel, "nt")
    dec = exp(where(incl, gc - grow, 0.0))
    knb = _lift(lambda a: a.astype(BF16))(kn)
    kbd = _lift(_bd)(knb)
    kk = _lift(_bdot)(knb, kbd, "nt")
    qk = _lift(_bdot)(_lift(lambda a: a.astype(BF16))(qn), kbd, "nt")
    mm = where(strict, bt * kk * dec, 0.0)
    inv = _st_unit_lower_inverse_l(mm, r, q)
    egc = exp(gc)
    uu = st3(inv, bt * vv)
    ww = st3(inv, (bt * egc) * kn)
    s_old = _Lock([s_scr[j] for j in seqs])
    delta = uu - mm3(ww, s_old)
    o = mm1(qn * egc, s_old) + st1(where(incl, qk * dec, 0.0), delta)
    glast = gc[c - 1:c, :]
    kd = kn * exp(glast - gc)
    rb = lax.broadcasted_iota(I32, (HW, HW), 0) // 64
    cb = lax.broadcasted_iota(I32, (HW, HW), 1) // 64
    gl = _Lock([jnp.zeros((HW, HW), F32)] * ns)
    for h in range(NH):
        gl = where(rb == h, exp(gcum_all[c - 1:c, M_BA + h:M_BA + h + 1]), gl)
    s_new = gl * s_old + where(rb == cb, mm3(kd, delta, "tn"), 0.0)
    ms_o = _lift(_group_sum)(o * o, bd512_ref[0:HW, 0:HW]) * (1.0 / 64)
    out = o * _lift(lax.rsqrt)(ms_o + EPS) * dn_ref[...] * _lift(_silu)(_Lock([bg_ref[j] for j in seqs]))
    for j in seqs:
        s_scr[j] = s_new.v[j]
        o_ref[j] = out.v[j]

    @pl.when(t == nt - 1)
    def _():
        sout_ref[...] = s_scr[...]


def _delta_st(z, nb, t_pad, t_valid, buf0, s0_bd, cw, al_row, dt_row, dn_row, bd512):
    c = CHUNK
    ns = SEQ_PER_STEP
    assert c == 64 and nb % ns == 0
    nt = t_pad // c
    z3 = z.reshape(nb, t_pad, NZ)
    const = lambda *shape: pl.BlockSpec(shape, lambda b, t: (0,) * len(shape))
    tok = lambda w, col: pl.BlockSpec((ns, c, w), lambda b, t: (b, t, col))
    per_seq = lambda *shape: pl.BlockSpec((ns,) + shape, lambda b, t: (b,) + (0,) * len(shape))
    o, s_out, nbuf = pl.pallas_call(
        functools.partial(_delta_st_kernel, c=c, t_valid=t_valid, nt=nt),
        grid=(nb // ns, nt),
        in_specs=[tok(B_CONV, Z_CONV // B_CONV), tok(HW, Z_BG // HW), tok(LANES, Z_MISC // LANES),
                  per_seq(CONV_W - 1, B_CONV), per_seq(HW, HW),
                  const(CONV_W, B_CONV), const(1, LANES), const(1, LANES), const(1, HW),
                  const(512, 512)],
        out_specs=[tok(HW, 0), per_seq(HW, HW), per_seq(CONV_W - 1, B_CONV)],
        out_shape=[jax.ShapeDtypeStruct((nb, t_pad, HW), F32),
                   jax.ShapeDtypeStruct((nb, HW, HW), F32),
                   jax.ShapeDtypeStruct((nb, CONV_W - 1, B_CONV), F32)],
        scratch_shapes=[pltpu.VMEM((ns, 8 + c, B_CONV), F32), pltpu.VMEM((ns, HW, HW), F32)],
        compiler_params=_cparams(("arbitrary", "arbitrary")),
        name="delta",
    )(z3, z3, z3, buf0, s0_bd, cw, al_row, dt_row, dn_row, bd512)
    return o.reshape(nb * t_pad, HW), s_out, nbuf


def _hgrn_st_kernel(x_ref, s0_ref, lb_ref, cn_ref, bd512_ref, o_ref, sout_ref, s_scr, *, c, t_valid, nt):
    t = pl.program_id(1)
    seqs = range(SEQ_PER_STEP)
    where, exp, minimum = _lift(jnp.where), _lift(jnp.exp), _lift(jnp.minimum)
    st1, mm1, mm3, msel, gsum = _lift(_st1), _lift(_mm1), _lift(_mm3), _lift(_mm_sel), _lift(_group_sum)

    @pl.when(t == 0)
    def _():
        s_scr[...] = s0_ref[...]

    lb = lb_ref[...]
    cf = _Lock([x_ref[j, :, HW:2 * HW] for j in seqs])
    logf = _lift(jnp.log)(_lift(jax.nn.sigmoid)(cf) * (1.0 - lb) + lb)
    kk = _lift(jax.nn.sigmoid)(-cf) * (1.0 - lb)
    qq = _lift(_silu)(_Lock([x_ref[j, :, 0:HW] for j in seqs]))
    vv = _Lock([x_ref[j, :, 2 * HW:3 * HW] for j in seqs])
    if t_valid % c != 0:
        vm = (t * c + lax.broadcasted_iota(I32, (c, 1), 0) < t_valid).astype(F32)
        logf = logf * vm
        kk = kk * vm

    r64 = lax.broadcasted_iota(I32, (c, c), 0)
    q64 = lax.broadcasted_iota(I32, (c, c), 1)
    b = msel(r64 >= q64, logf)
    r = lax.broadcasted_iota(I32, (c, HW), 0)
    q = lax.broadcasted_iota(I32, (c, HW), 1) % 64

    amat = _Lock([jnp.zeros((c, HW), F32)] * SEQ_PER_STEP)
    half = c // 2
    while half >= 1:
        sz = 2 * half
        bnd = (r64 // sz) * sz + half - 1
        rl = msel(q64 == bnd, b)
        qf = qq * exp(minimum(b - rl, 0.0))
        kf = kk * exp(minimum(rl - b, 0.0))
        lmask = jnp.logical_and(r // sz == q // sz, jnp.logical_and(r % sz >= half, q % sz < half))
        amat = amat + where(lmask, st1(qf, kf, "nt"), 0.0)
        half //= 2

    st_old = _Lock([s_scr[j] for j in seqs])
    o = mm1(qq * exp(b), st_old, "nt") + st1(amat, vv)
    o = o + gsum(qq * kk, bd512_ref[0:HW, 0:HW]) * vv
    blast = b[c - 1:c, :]
    kd = kk * exp(blast - b)
    rb = lax.broadcasted_iota(I32, (HW, HW), 0) // 64
    cb = lax.broadcasted_iota(I32, (HW, HW), 1) // 64
    st_new = st_old * exp(blast) + where(rb == cb, mm3(vv, kd, "tn"), 0.0)
    ms_o = gsum(o * o, bd512_ref[0:HW, 0:HW]) * (1.0 / 64)
    out = o * _lift(lax.rsqrt)(ms_o + EPS) * cn_ref[...]
    for j in seqs:
        s_scr[j] = st_new.v[j]
        o_ref[j] = out.v[j]

    @pl.when(t == nt - 1)
    def _():
        sout_ref[...] = s_scr[...]


def _hgrn_st(z, nb, t_pad, t_valid, s0_bd, lb_row, cn_row, bd512):
    c = CHUNK
    ns = SEQ_PER_STEP
    assert c == 64 and nb % ns == 0
    nt = t_pad // c
    z3 = z.reshape(nb, t_pad, NZ)
    const = lambda *shape: pl.BlockSpec(shape, lambda b, t: (0,) * len(shape))
    per_seq = lambda *shape: pl.BlockSpec((ns,) + shape, lambda b, t: (b,) + (0,) * len(shape))
    o, s_out = pl.pallas_call(
        functools.partial(_hgrn_st_kernel, c=c, t_valid=t_valid, nt=nt),
        grid=(nb // ns, nt),
        in_specs=[pl.BlockSpec((ns, c, 768), lambda b, t: (b, t, Z_HG // 768)),
                  per_seq(HW, HW),
                  const(1, HW), const(1, HW), const(512, 512)],
        out_specs=[pl.BlockSpec((ns, c, HW), lambda b, t: (b, t, 0)), per_seq(HW, HW)],
        out_shape=[jax.ShapeDtypeStruct((nb, t_pad, HW), F32),
                   jax.ShapeDtypeStruct((nb, HW, HW), F32)],
        scratch_shapes=[pltpu.VMEM((ns, HW, HW), F32)],
        compiler_params=_cparams(("arbitrary", "arbitrary")),
        name="hgrn",
    )(z3, s0_bd, lb_row, cn_row, bd512)
    return o.reshape(nb * t_pad, HW), s_out


def _out_mlp_kernel(x_ref, oa_ref, ob_ref, oc_ref, wo_ref, g2_ref, w1_ref, w2_ref, y_ref, h_scr):
    @pl.when(pl.program_id(1) == 0)
    def _():
        x1 = (x_ref[...]
              + _dot(oa_ref[...].astype(BF16), wo_ref[0:A_WIDTH, :])
              + _dot(ob_ref[...].astype(BF16), wo_ref[A_WIDTH:A_WIDTH + 256, :])
              + _dot(oc_ref[...].astype(BF16), wo_ref[A_WIDTH + 256:A_WIDTH + 512, :]))
        y_ref[...] = x1
        ms = jnp.mean(x1 * x1, axis=-1, keepdims=True)
        h_scr[...] = (x1 * lax.rsqrt(ms + EPS) * g2_ref[...]).astype(BF16)

    u = jnp.maximum(_dot(h_scr[...], w1_ref[...]), 0.0)
    y_ref[...] += _dot((u * u).astype(BF16), w2_ref[...])


def _out_mlp(x, oa, ob, oc, wo, g2, w1, w2):
    m, d = x.shape
    tm = min(512, m)
    tf = 512
    nf = w1.shape[1] // tf
    tok = lambda w: pl.BlockSpec((tm, w), lambda i, j: (i, 0))
    return pl.pallas_call(
        _out_mlp_kernel,
        grid=(m // tm, nf),
        in_specs=[tok(d), tok(A_WIDTH), tok(256), tok(256),
                  pl.BlockSpec((d, d), lambda i, j: (0, 0)),
                  pl.BlockSpec((1, d), lambda i, j: (0, 0)),
                  pl.BlockSpec((d, tf), lambda i, j: (0, j)),
                  pl.BlockSpec((tf, d), lambda i, j: (j, 0))],
        out_specs=tok(d),
        out_shape=jax.ShapeDtypeStruct((m, d), F32),
        scratch_shapes=[pltpu.VMEM((tm, d), BF16)],
        compiler_params=_cparams(("parallel", "arbitrary")),
        name="out_mlp",
    )(x, oa, ob, oc, wo, g2, w1, w2)


def _relayout_w_in(w):
    d = w.shape[0]
    off = np.cumsum([0, 512, 512, 512, 1024, 16, 64, 256, 256, 256, 4, 4, 256, 256, 256, 256])
    (aq, ak, av, iq, iw, ik, bq, bk, bv, ba, bb, bg, cq, cf, ci, end) = [int(o) for o in off]
    zpad = lambda n: jnp.zeros((d, n), w.dtype)
    cols = [w[:, aq:iw],
            w[:, bg:cq],
            w[:, ik:bq], w[:, iw:ik], w[:, ba:bb], w[:, bb:bg], zpad(LANES - 88),
            zpad(Z_CONV - Z_MISC - LANES),
            w[:, bq:ba],
            w[:, cq:end]]
    out = jnp.concatenate(cols, axis=1)
    assert out.shape[1] == NZ
    return out.astype(BF16)


def _rope_tables(pos):
    pos = pos.astype(F32)
    half = HEAD_DIM // 2
    inv = ROPE_THETA ** (-jnp.arange(half, dtype=F32) / half)
    ang = pos[:, None] * inv[None, :]
    cos, sin = jnp.cos(ang), jnp.sin(ang)
    ca = jnp.tile(jnp.concatenate([cos, cos], -1), (1, 2))
    sa = jnp.tile(jnp.concatenate([-sin, sin], -1), (1, 2))
    half_i = IDX_ROPE // 2
    inv_i = ROPE_THETA ** (-jnp.arange(half_i, dtype=F32) / half_i)
    ang_i = pos[:, None] * inv_i[None, :]
    cos_i, sin_i = jnp.cos(ang_i), jnp.sin(ang_i)
    rest = IDX_DIM - IDX_ROPE
    one = jnp.ones((pos.shape[0], rest), F32)
    zero = jnp.zeros((pos.shape[0], rest), F32)
    ci = jnp.tile(jnp.concatenate([cos_i, cos_i, one], -1), (1, 2))
    si = jnp.tile(jnp.concatenate([-sin_i, sin_i, zero], -1), (1, 2))
    return ca, sa, ci, si


def _to_bd4(s):
    z = jnp.zeros_like(s[:, 0])
    rows = [jnp.concatenate([s[:, h] if g == h else z for g in range(NH)], axis=-1) for h in range(NH)]
    return jnp.concatenate(rows, axis=-2)


def _from_bd4(p):
    return jnp.stack([p[:, 64 * h:64 * (h + 1), 64 * h:64 * (h + 1)] for h in range(NH)], axis=1)


def _lane_row(vals, offset):
    return jnp.zeros((1, LANES), F32).at[0, offset:offset + vals.shape[0]].set(vals.astype(F32))


def _block_diag_ones(n):
    i = np.arange(n) // 64
    return jnp.asarray((i[:, None] == i[None, :]).astype(np.float32)).astype(BF16)


def kernel(x_prompt, x_sample, cache_k, cache_v, cache_idx_k, state_delta, state_delta_conv, state_hgrn,
           page_table, norm1_g, w_in, q_norm_g, k_norm_g, idx_k_ln_g, idx_k_ln_b, conv_w, a_log, dt_bias,
           delta_norm_g, lb_param, hgrn_norm_g, w_out, norm2_g, w_mlp_in, w_mlp_out):
    depth = w_in.shape[0]
    bp, s_len, d = x_prompt.shape
    bd_, t_dec, _ = x_sample.shape
    assert t_dec == 1 and d == D_MODEL
    past = page_table.shape[1] * PAGE_SIZE

    lb_all = jnp.cumsum(jax.nn.softmax(lb_param.astype(F32), axis=0), axis=0)
    lb_all = lb_all - lb_all[0:1]
    tabs_p = _rope_tables(jnp.arange(s_len))
    tabs_s = _rope_tables(past + jnp.arange(t_dec))
    bd512 = _block_diag_ones(512)
    bd128 = _block_diag_ones(LANES)

    xp = x_prompt.reshape(bp * s_len, d)
    xs = x_sample.reshape(bd_ * t_dec, d)
    t_pad_s = CHUNK
    zeros_buf = jnp.zeros((bp, CONV_W - 1, B_CONV), F32)
    zeros_state = jnp.zeros((bp, HW, HW), F32)

    outs_p = {k: [] for k in ("k", "v", "ik", "ds", "dc", "hs")}
    outs_s = {k: [] for k in ("k", "v", "ik", "ds", "dc", "hs")}
    for l in range(depth):
        w_in_l = _relayout_w_in(w_in[l])
        g1 = norm1_g[l].reshape(1, d)
        qg = jnp.tile(q_norm_g[l], A_HEADS).reshape(1, A_WIDTH)
        kg = jnp.tile(k_norm_g[l], A_HEADS).reshape(1, A_WIDTH)
        ikg = _lane_row(idx_k_ln_g[l], 0)
        ikb = _lane_row(idx_k_ln_b[l], 0)
        al_row = _lane_row(a_log[l], M_BA)
        dt_row = _lane_row(dt_bias[l], M_BA)
        dn_row = jnp.tile(delta_norm_g[l], NH).reshape(1, HW).astype(F32)
        cn_row = jnp.tile(hgrn_norm_g[l], NH).reshape(1, HW).astype(F32)
        lb_row = lb_all[l].reshape(1, 256)
        wo = w_out[l].astype(BF16)
        g2 = norm2_g[l].reshape(1, d)
        w1 = w_mlp_in[l].astype(BF16)
        w2 = w_mlp_out[l].astype(BF16)
        cw = conv_w[l]

        z = _norm_matmul(xp, g1, w_in_l, 768)
        pp = _prep_t(z, tabs_p, bp, s_len, qg, kg, ikg, ikb, bd512)
        oa = _attn_prompt(pp, bp, s_len)
        ob, ds, dc = _delta_st(z, bp, s_len, s_len, zeros_buf, zeros_state, cw, al_row, dt_row, dn_row, bd512)
        oc, hs = _hgrn_st(z, bp, s_len, s_len, zeros_state, lb_row, cn_row, bd512)
        xp = _out_mlp(xp, oa, ob, oc, wo, g2, w1, w2)
        heads_last = lambda a: jnp.transpose(a.reshape(bp, A_HEADS, HEAD_DIM, s_len), (0, 3, 1, 2))
        outs_p["k"].append(heads_last(pp["kt"]))
        outs_p["v"].append(heads_last(pp["vt"]))
        outs_p["ik"].append(jnp.transpose(pp["ikt"], (0, 2, 1)))
        outs_p["ds"].append(_from_bd4(ds))
        outs_p["dc"].append(dc)
        outs_p["hs"].append(jnp.swapaxes(_from_bd4(hs), -1, -2))

        zs = _norm_matmul(xs, g1, w_in_l, 768)
        ps = _prep(zs, tabs_s, qg, kg, ikg, ikb, bd512)
        oa_s = _attn_sample(ps, cache_k, cache_v, cache_idx_k, page_table, l)
        zs_pad = jnp.pad(zs.reshape(bd_, 1, NZ), ((0, 0), (0, t_pad_s - 1), (0, 0))).reshape(bd_ * t_pad_s, NZ)
        ob_s, ds_s, dc_s = _delta_st(zs_pad, bd_, t_pad_s, 1, state_delta_conv[l].astype(F32),
                                     _to_bd4(state_delta[l].astype(F32)), cw, al_row, dt_row, dn_row, bd512)
        oc_s, hs_s = _hgrn_st(zs_pad, bd_, t_pad_s, 1, _to_bd4(jnp.swapaxes(state_hgrn[l].astype(F32), -1, -2)),
                              lb_row, cn_row, bd512)
        ob_s = ob_s.reshape(bd_, t_pad_s, 256)[:, 0]
        oc_s = oc_s.reshape(bd_, t_pad_s, 256)[:, 0]
        xs = _out_mlp(xs, oa_s, ob_s, oc_s, wo, g2, w1, w2)
        outs_s["k"].append(ps["kf"].reshape(bd_, t_dec, A_HEADS, HEAD_DIM))
        outs_s["v"].append(ps["vf"].reshape(bd_, t_dec, A_HEADS, HEAD_DIM))
        outs_s["ik"].append(ps["ikf"].reshape(bd_, t_dec, IDX_DIM))
        outs_s["ds"].append(_from_bd4(ds_s))
        outs_s["dc"].append(dc_s)
        outs_s["hs"].append(jnp.swapaxes(_from_bd4(hs_s), -1, -2))

    st = lambda o, k: jnp.stack(o[k])
    return (xp.reshape(bp, s_len, d), xs.reshape(bd_, t_dec, d),
            st(outs_p, "k"), st(outs_p, "v"), st(outs_p, "ik"), st(outs_p, "ds"), st(outs_p, "dc"), st(outs_p, "hs"),
            st(outs_s, "k"), st(outs_s, "v"), st(outs_s, "ik"), st(outs_s, "ds"), st(outs_s, "dc"), st(outs_s, "hs"))
```

```python
import functools

import numpy as np
import jax
import jax.numpy as jnp
from jax import lax
from jax.experimental import pallas as pl
from jax.experimental.pallas import tpu as pltpu

F32 = jnp.float32
BF16 = jnp.bfloat16
I32 = jnp.int32
I16 = jnp.int16

D_MODEL = 1024
HEAD_DIM = 64
A_HEADS = 8
A_WIDTH = A_HEADS * HEAD_DIM
IDX_HEADS = 16
IDX_DIM = 64
IDX_ROPE = 32
TOPK_MAX = 256
B_HEADS = 4
B_DK = 64
B_CONV = 768
CONV_W = 4
C_HEADS = 4
D_FF = 4 * D_MODEL
ROPE_THETA = 10000.0
EPS = 1e-6
PAGE_SIZE = 128
CHUNK = 64

LANES = 128
VMEM_LIMIT = 48 * 1024 * 1024

Z_IQ = 3 * A_WIDTH
Z_BG = Z_IQ + IDX_HEADS * IDX_DIM
Z_MISC = Z_BG + 256
Z_CONV = 3072
Z_HG = Z_CONV + B_CONV
NZ = Z_HG + 768
M_IW = 64
M_BA = 80
M_BB = 84

NEG = -1e30
INT_MIN = -2 ** 31
LOG2E = 1.4426950408889634
KSPLIT = 4


def _cparams(sem):
    return pltpu.CompilerParams(dimension_semantics=sem, vmem_limit_bytes=VMEM_LIMIT)


def _dot(a, b, precision=None):
    return jnp.dot(a, b, preferred_element_type=F32, precision=precision)


def _dot_nt(a, b, precision=None):
    return lax.dot_general(a, b, (((1,), (1,)), ((), ())), preferred_element_type=F32, precision=precision)


_DIMS = {"nn": (((1,), (0,)), ((), ())), "nt": (((1,), (1,)), ((), ())), "tn": (((0,), (0,)), ((), ()))}


def _bdot(a, b, form):
    return lax.dot_general(a, b, _DIMS[form], preferred_element_type=F32)


def _split2(x):
    hi = x.astype(BF16)
    return hi, (x - hi.astype(F32)).astype(BF16)


def _split3(x):
    hi = x.astype(BF16)
    r = x - hi.astype(F32)
    mid = r.astype(BF16)
    return hi, mid, (r - mid.astype(F32)).astype(BF16)


def _mm1(a, b, form="nn"):
    return _bdot(a.astype(BF16), b.astype(BF16), form)


def _mm3(a, b, form="nn"):
    ah, al = _split2(a)
    bh, bl = _split2(b)
    return _bdot(ah, bh, form) + (_bdot(ah, bl, form) + _bdot(al, bh, form))


def _mm_sel(a01, b, form="nn"):
    a = a01.astype(BF16)
    h, m, l = _split3(b)
    return _bdot(a, h, form) + (_bdot(a, m, form) + _bdot(a, l, form))


def _group_sum(y, bd):
    hi = y.astype(BF16)
    lo = (y - hi.astype(F32)).astype(BF16)
    return _dot(hi, bd) + _dot(lo, bd)


def _to_key(x):
    bits = lax.bitcast_convert_type(x, I32)
    return jnp.where(bits < 0, bits ^ jnp.int32(0x7FFFFFFF), bits)


def _silu(x):
    return x * jax.nn.sigmoid(x)


def _softplus(x):
    return jnp.maximum(x, 0.0) + jnp.log(1.0 + jnp.exp(-jnp.abs(x)))


def _norm_matmul_kernel(x_ref, g_ref, w_ref, o_ref, h_scr):
    @pl.when(pl.program_id(1) == 0)
    def _():
        x = x_ref[...]
        ms = jnp.mean(x * x, axis=-1, keepdims=True)
        h_scr[...] = (x * lax.rsqrt(ms + EPS) * g_ref[...]).astype(BF16)

    o_ref[...] = _dot(h_scr[...], w_ref[...])


def _norm_matmul(x, g, w, tn):
    m, d = x.shape
    n = w.shape[1]
    tm = min(512, m)
    return pl.pallas_call(
        _norm_matmul_kernel,
        grid=(m // tm, n // tn),
        in_specs=[pl.BlockSpec((tm, d), lambda i, j: (i, 0)),
                  pl.BlockSpec((1, d), lambda i, j: (0, 0)),
                  pl.BlockSpec((d, tn), lambda i, j: (0, j))],
        out_specs=pl.BlockSpec((tm, tn), lambda i, j: (i, j)),
        out_shape=jax.ShapeDtypeStruct((m, n), F32),
        scratch_shapes=[pltpu.VMEM((tm, d), BF16)],
        compiler_params=_cparams(("parallel", "arbitrary")),
        name="norm_matmul",
    )(x, g, w)


def _prep_values(za_ref, zm_ref, ca_ref, sa_ref, ci_ref, si_ref, qg_ref, kg_ref, ikg_ref, ikb_ref, bd_ref):
    tm = za_ref.shape[0]
    bd = bd_ref[...]
    ca = jnp.concatenate([ca_ref[...]] * 4, axis=1)
    sa = jnp.concatenate([sa_ref[...]] * 4, axis=1)
    lane_a = lax.broadcasted_iota(I32, (tm, A_WIDTH), 1)
    lo_half = (lane_a % HEAD_DIM) < (HEAD_DIM // 2)
    lane128 = lax.broadcasted_iota(I32, (tm, LANES), 1)
    left = lane128 < 64

    def norm_rope(x, g):
        ss = _group_sum(x * x, bd)
        y = x * lax.rsqrt(ss * (1.0 / HEAD_DIM) + EPS) * g
        sw = jnp.where(lo_half, pltpu.roll(y, A_WIDTH - 32, 1), pltpu.roll(y, 32, 1))
        return y * ca + sw * sa

    q = norm_rope(za_ref[:, 0:A_WIDTH], qg_ref[...]) * (HEAD_DIM ** -0.5 * LOG2E)
    k = norm_rope(za_ref[:, A_WIDTH:2 * A_WIDTH], kg_ref[...])
    v = za_ref[:, 2 * A_WIDTH:3 * A_WIDTH]

    nq = IDX_HEADS * IDX_DIM
    xi = za_ref[:, Z_IQ:Z_IQ + nq]
    lane_i = lax.broadcasted_iota(I32, (tm, nq), 1)
    ci = jnp.concatenate([ci_ref[...]] * 8, axis=1)
    si = jnp.concatenate([si_ref[...]] * 8, axis=1)
    swi = jnp.where((lane_i % IDX_DIM) < 16, pltpu.roll(xi, nq - 16, 1), pltpu.roll(xi, 16, 1))
    iq = xi * ci + swi * si

    m = zm_ref[...]
    mu = jnp.sum(jnp.where(left, m, 0.0), axis=-1, keepdims=True) * (1.0 / IDX_DIM)
    xc = jnp.where(left, m - mu, 0.0)
    var = jnp.sum(xc * xc, axis=-1, keepdims=True) * (1.0 / IDX_DIM)
    y = xc * lax.rsqrt(var + EPS) * ikg_ref[...] + ikb_ref[...]
    swk = jnp.where(lane128 < 16, pltpu.roll(y, LANES - 16, 1), pltpu.roll(y, 16, 1))
    ikr = y * ci_ref[...] + swk * si_ref[...]
    ikdup = jnp.where(left, ikr, pltpu.roll(ikr, 64, 1)).astype(BF16)
    misc = m * (IDX_HEADS ** -0.5 * IDX_DIM ** -0.5)
    return q, k, v, iq, ikr, ikdup, misc


def _prep_kernel(*refs):
    ins, (qpad_ref, kf_ref, vf_ref, iqpad_ref, ikf_ref, ikdup_ref, misc_ref) = refs[:11], refs[11:]
    q, k, v, iq, ikr, ikdup, misc = _prep_values(*ins)
    left = lax.broadcasted_iota(I32, (q.shape[0], LANES), 1) < 64
    for h in range(A_HEADS):
        keep = left if h % 2 == 0 else jnp.logical_not(left)
        qpad_ref[h] = jnp.where(keep, q[:, (h // 2) * LANES:(h // 2 + 1) * LANES], 0.0).astype(BF16)
    for h in range(IDX_HEADS):
        keep = left if h % 2 == 0 else jnp.logical_not(left)
        iqpad_ref[h] = jnp.where(keep, iq[:, (h // 2) * LANES:(h // 2 + 1) * LANES], 0.0).astype(BF16)
    kf_ref[...] = k
    vf_ref[...] = v
    ikf_ref[...] = ikr[:, 0:IDX_DIM]
    ikdup_ref[...] = ikdup
    misc_ref[...] = misc


def _prep_t_kernel(*refs):
    ins, (qt_ref, kt_ref, kb_ref, vt_ref, vaug_ref, iqt_ref, ikt_ref, ikdup_ref, iwt_ref) = refs[:11], refs[11:]
    q, k, v, iq, ikr, ikdup, misc = _prep_values(*ins)
    tm = q.shape[0]
    top = lax.broadcasted_iota(I32, (LANES, tm), 0) < 64
    qt = q.T
    vt = v.T
    iqt = iq.T
    for h in range(A_HEADS):
        keep = top if h % 2 == 0 else jnp.logical_not(top)
        rows = slice((h // 2) * LANES, (h // 2 + 1) * LANES)
        qt_ref[h] = jnp.where(keep, qt[rows, :], 0.0).astype(BF16)
        vaug_ref[h] = jnp.where(keep, vt[rows, :], 1.0).astype(BF16)
    for h in range(IDX_HEADS):
        keep = top if h % 2 == 0 else jnp.logical_not(top)
        iqt_ref[h] = jnp.where(keep, iqt[(h // 2) * LANES:(h // 2 + 1) * LANES, :], 0.0).astype(BF16)
    kt_ref[...] = k.T
    kb_ref[...] = k.astype(BF16)
    vt_ref[...] = vt
    ikt_ref[...] = ikr.T[0:IDX_DIM, :]
    ikdup_ref[...] = ikdup
    iwt_ref[...] = misc.T[M_IW:M_IW + IDX_HEADS, :]


def _prep_in_specs(tm, tab_spec):
    row = lambda w: pl.BlockSpec((1, w), lambda i: (0, 0))
    return [pl.BlockSpec((tm, Z_BG), lambda i: (i, 0)),
            pl.BlockSpec((tm, LANES), lambda i: (i, Z_MISC // LANES)),
            tab_spec, tab_spec, tab_spec, tab_spec,
            row(A_WIDTH), row(A_WIDTH), row(LANES), row(LANES),
            pl.BlockSpec((A_WIDTH, A_WIDTH), lambda i: (0, 0))]


def _prep(z, tabs, qg, kg, ikg, ikb, bd512):
    m = z.shape[0]
    tm = m
    tab_spec = pl.BlockSpec((1, LANES), lambda i: (0, 0))
    tok = lambda w: pl.BlockSpec((tm, w), lambda i: (i, 0))
    outs = pl.pallas_call(
        _prep_kernel,
        grid=(m // tm,),
        in_specs=_prep_in_specs(tm, tab_spec),
        out_specs=[pl.BlockSpec((A_HEADS, tm, LANES), lambda i: (0, i, 0)),
                   tok(A_WIDTH), tok(A_WIDTH),
                   pl.BlockSpec((IDX_HEADS, tm, LANES), lambda i: (0, i, 0)),
                   tok(IDX_DIM), tok(LANES), tok(LANES)],
        out_shape=[jax.ShapeDtypeStruct((A_HEADS, m, LANES), BF16),
                   jax.ShapeDtypeStruct((m, A_WIDTH), F32),
                   jax.ShapeDtypeStruct((m, A_WIDTH), F32),
                   jax.ShapeDtypeStruct((IDX_HEADS, m, LANES), BF16),
                   jax.ShapeDtypeStruct((m, IDX_DIM), F32),
                   jax.ShapeDtypeStruct((m, LANES), BF16),
                   jax.ShapeDtypeStruct((m, LANES), F32)],
        compiler_params=_cparams(("parallel",)),
        name="prep",
    )(z, z, *tabs, qg, kg, ikg, ikb, bd512)
    names = ("qpad", "kf", "vf", "iqpad", "ikf", "ikdup", "misc")
    return dict(zip(names, outs))


def _prep_t(z, tabs, nb, t_seq, qg, kg, ikg, ikb, bd512):
    m = z.shape[0]
    tm = min(256, t_seq)
    assert t_seq % tm == 0
    nblk = t_seq // tm
    tab_spec = pl.BlockSpec((tm, LANES), lambda i: (i % nblk, 0))
    feat = lambda f: pl.BlockSpec((None, f, tm), lambda i: (i // nblk, 0, i % nblk))
    heads = lambda nh: pl.BlockSpec((nh, LANES, tm), lambda i: (0, 0, i))
    outs = pl.pallas_call(
        _prep_t_kernel,
        grid=(m // tm,),
        in_specs=_prep_in_specs(tm, tab_spec),
        out_specs=[heads(A_HEADS), feat(A_WIDTH),
                   pl.BlockSpec((tm, A_WIDTH), lambda i: (i, 0)),
                   feat(A_WIDTH), heads(A_HEADS), heads(IDX_HEADS), feat(IDX_DIM),
                   pl.BlockSpec((tm, LANES), lambda i: (i, 0)),
                   pl.BlockSpec((IDX_HEADS, tm), lambda i: (0, i))],
        out_shape=[jax.ShapeDtypeStruct((A_HEADS, LANES, m), BF16),
                   jax.ShapeDtypeStruct((nb, A_WIDTH, t_seq), F32),
                   jax.ShapeDtypeStruct((m, A_WIDTH), BF16),
                   jax.ShapeDtypeStruct((nb, A_WIDTH, t_seq), F32),
                   jax.ShapeDtypeStruct((A_HEADS, LANES, m), BF16),
                   jax.ShapeDtypeStruct((IDX_HEADS, LANES, m), BF16),
                   jax.ShapeDtypeStruct((nb, IDX_DIM, t_seq), F32),
                   jax.ShapeDtypeStruct((m, LANES), BF16),
                   jax.ShapeDtypeStruct((IDX_HEADS, m), F32)],
        compiler_params=_cparams(("parallel",)),
        name="prep_t",
    )(z, z, *tabs, qg, kg, ikg, ikb, bd512)
    names = ("qt", "kt", "kb", "vt", "vaug", "iqt", "ikt", "ikdup", "iwt")
    return dict(zip(names, outs))


def _attn_prompt_kernel(qb_tab, kb_tab, nkb_tab, qt_ref, iqt_ref, iwt_ref, ik_ref, k_ref, vaug_ref, o_ref,
                        key_scr, thr_scr, m_scr, acc_scr, iq2_scr, k16_scr, l16_scr, lo_scr, *, tq, tk, topk):
    s = pl.program_id(1)
    qb = qb_tab[s]
    kb = kb_tab[s]
    nkb = nkb_tab[s]
    koff = lax.broadcasted_iota(I32, (tk, tq), 0)
    tpos = qb * tq + lax.broadcasted_iota(I32, (tk, tq), 1)

    @pl.when(kb == 0)
    def _():
        iwt = iwt_ref[...]
        for j in range(IDX_HEADS // 2):
            iq2_scr[j] = jnp.concatenate([iqt_ref[2 * j], iqt_ref[2 * j + 1]], axis=1)

        def score_chunk(c, carry):
            ikc = ik_ref[pl.ds(pl.multiple_of(c * tk, tk), tk), :]
            acc = jnp.zeros((tk, tq), F32)
            for j in range(IDX_HEADS // 2):
                s2 = _dot(ikc, iq2_scr[j])
                acc = acc + jnp.maximum(s2[:, 0:tq], 0.0) * iwt[2 * j:2 * j + 1, :]
                acc = acc + jnp.maximum(s2[:, tq:2 * tq], 0.0) * iwt[2 * j + 1:2 * j + 2, :]
            acc = jnp.where(c * tk + koff <= tpos, acc, -jnp.inf)
            key = _to_key(acc)
            key_scr[c] = key
            k16_scr[c] = (key >> 16).astype(I16)
            return carry

        lax.fori_loop(0, nkb, score_chunk, 0)

        def count16(src_scr, cand, strict=False):
            c16 = cand.astype(I16)

            def body(c, cnt):
                hit = src_scr[c] > c16 if strict else src_scr[c] >= c16
                one = jnp.where(hit, jnp.int16(1), jnp.int16(0))
                for g in range(tk // 16):
                    cnt = cnt + one[g * 16:(g + 1) * 16, :]
                return cnt

            cnt = lax.fori_loop(0, nkb, body, jnp.zeros((16, tq), I16))
            return jnp.sum(cnt.astype(I32), axis=0, keepdims=True)

        def search16(count_fn, opn):
            zero = jnp.zeros((1, tq), I32)
            c0 = count_fn(zero)
            t0 = jnp.where(c0 >= topk, zero, jnp.full((1, tq), -32768, I32))
            opn = jnp.where(c0 == topk, 0, opn)

            def cond(st):
                i, _, _, n_open = st
                return jnp.logical_and(i < 15, n_open > 0)

            def bit_step(st):
                i, t, o, _ = st
                cand = t | jnp.left_shift(jnp.int32(1), 14 - i)
                cnt = count_fn(cand)
                t = jnp.where(cnt >= topk, cand, t)
                o = jnp.where(cnt == topk, 0, o)
                return i + 1, t, o, jnp.sum(o)

            _, t, opn, _ = lax.while_loop(cond, bit_step, (jnp.int32(0), t0, opn, jnp.sum(opn)))
            return t, opn

        n_adm = qb * tq + lax.broadcasted_iota(I32, (1, tq), 1) + 1
        open0 = (n_adm >= topk).astype(I32)
        hi, opn = search16(lambda cand: count16(k16_scr, cand), open0)
        lo_scr[...] = jnp.full((1, tq), -32768, I32)

        @pl.when(jnp.sum(opn) > 0)
        def _():
            def low_chunk(c, carry):
                key = key_scr[c]
                low = (key & 0xFFFF) - 32768
                l16_scr[c] = jnp.where((key >> 16) == hi, low, -32768).astype(I16)
                return carry

            lax.fori_loop(0, nkb, low_chunk, 0)
            above = count16(k16_scr, hi, strict=True)
            lo, _ = search16(lambda cand: above + count16(l16_scr, cand), opn)
            lo_scr[...] = lo

        thr_scr[...] = hi * 65536 + (lo_scr[...] + 32768)
        m_scr[...] = jnp.full(m_scr.shape, NEG, F32)
        acc_scr[...] = jnp.zeros(acc_scr.shape, F32)

    sel = jnp.logical_and(key_scr[kb] >= thr_scr[...], kb * tk + koff <= tpos)
    bias = jnp.where(sel, 0.0, NEG)
    ts = tk // KSPLIT
    for u in range(KSPLIT):
        rows = slice(u * ts, (u + 1) * ts)
        for h in range(A_HEADS):
            p0 = (h // 2) * LANES
            sc = _dot(k_ref[rows, p0:p0 + LANES], qt_ref[h]) + bias[rows, :]
            m_old = m_scr[h:h + 1, :]
            m_new = jnp.maximum(m_old, jnp.max(sc, axis=0, keepdims=True))
            alpha = jnp.exp2(m_old - m_new)
            p = jnp.exp2(sc - m_new).astype(BF16)
            acc_scr[h] = alpha * acc_scr[h] + _dot(vaug_ref[h, :, rows], p)
            m_scr[h:h + 1, :] = m_new

    @pl.when(kb == nkb - 1)
    def _():
        for p in range(A_HEADS // 2):
            a = acc_scr[2 * p]
            b = acc_scr[2 * p + 1]
            ot = jnp.concatenate([a[0:64, :] / a[64:65, :], b[64:128, :] / b[0:1, :]], axis=0)
            o_ref[:, p * LANES:(p + 1) * LANES] = ot.T


def _attn_prompt(pp, nb, s_len):
    tq, tk = 256, 512
    tk = min(tk, s_len)
    topk = min(TOPK_MAX, s_len // 4)
    nqb = s_len // tq
    nkblk = s_len // tk
    qb_l, kb_l, nkb_l = [], [], []
    for qb in range(nqb):
        nkb = ((qb + 1) * tq + tk - 1) // tk
        for kb in range(nkb):
            qb_l.append(qb)
            kb_l.append(kb)
            nkb_l.append(nkb)
    tabs = [jnp.asarray(np.array(a, np.int32)) for a in (qb_l, kb_l, nkb_l)]
    nsteps = len(qb_l)
    m = nb * s_len
    grid_spec = pltpu.PrefetchScalarGridSpec(
        num_scalar_prefetch=3,
        grid=(nb, nsteps),
        in_specs=[
            pl.BlockSpec((A_HEADS, LANES, tq), lambda b, s, qt, kt, nt: (0, 0, b * nqb + qt[s])),
            pl.BlockSpec((IDX_HEADS, LANES, tq), lambda b, s, qt, kt, nt: (0, 0, b * nqb + qt[s])),
            pl.BlockSpec((IDX_HEADS, tq), lambda b, s, qt, kt, nt: (0, b * nqb + qt[s])),
            pl.BlockSpec((s_len, LANES), lambda b, s, qt, kt, nt: (b, 0)),
            pl.BlockSpec((tk, A_WIDTH), lambda b, s, qt, kt, nt: (b * nkblk + kt[s], 0)),
            pl.BlockSpec((A_HEADS, LANES, tk), lambda b, s, qt, kt, nt: (0, 0, b * nkblk + kt[s])),
        ],
        out_specs=pl.BlockSpec((tq, A_WIDTH), lambda b, s, qt, kt, nt: (b * nqb + qt[s], 0)),
        scratch_shapes=[pltpu.VMEM((nkblk, tk, tq), I32),
                        pltpu.VMEM((1, tq), I32),
                        pltpu.VMEM((A_HEADS, tq), F32),
                        pltpu.VMEM((A_HEADS, LANES, tq), F32),
                        pltpu.VMEM((IDX_HEADS // 2, LANES, 2 * tq), BF16),
                        pltpu.VMEM((nkblk, tk, tq), I16),
                        pltpu.VMEM((nkblk, tk, tq), I16),
                        pltpu.VMEM((1, tq), I32)],
    )
    return pl.pallas_call(
        functools.partial(_attn_prompt_kernel, tq=tq, tk=tk, topk=topk),
        grid_spec=grid_spec,
        out_shape=jax.ShapeDtypeStruct((m, A_WIDTH), F32),
        compiler_params=_cparams(("arbitrary", "arbitrary")),
        name="attn_prompt",
    )(*tabs, pp["qt"], pp["iqt"], pp["iwt"], pp["ikdup"], pp["kb"], pp["vaug"])


def _sample_score_kernel(pt_ref, iq_ref, iw_ref, *rest, npg):
    page_refs = rest[:npg]
    o_ref = rest[npg]
    iq = iq_ref[...]
    w = iw_ref[...]
    for g in range(npg):
        pg = page_refs[g][...].astype(BF16)
        sc = _dot(iq, pg)
        o_ref[g:g + 1, :] = jnp.sum(jnp.maximum(sc, 0.0) * w, axis=0, keepdims=True)


def _sample_attn_kernel(pt_ref, sc_ref, iq_ref, iw_ref, ikn_ref, qbd_ref, kn_ref, vn_ref, *rest, npg, topk):
    k_refs = rest[:npg]
    v_refs = rest[npg:2 * npg]
    o_ref = rest[2 * npg]
    key_scr, thr_scr, nk_scr, m_scr, l_scr, acc_scr = rest[2 * npg + 1:]
    j = pl.program_id(1)
    nj = pl.num_programs(1)

    @pl.when(j == 0)
    def _():
        keys = _to_key(sc_ref[...])
        key_scr[...] = keys
        sn = jnp.sum(iq_ref[...].astype(F32) * ikn_ref[...].astype(F32), axis=1, keepdims=True)
        snew = jnp.sum(jnp.maximum(sn, 0.0) * iw_ref[...], axis=0, keepdims=True)
        knew = _to_key(snew)
        nk_scr[...] = knew

        def count_ge(cand):
            c = jnp.sum((keys >= cand).astype(I32), axis=1, keepdims=True)
            return jnp.sum(c, axis=0, keepdims=True) + (knew >= cand).astype(I32)

        thr = jnp.full((1, 1), INT_MIN, I32)
        zero = jnp.zeros((1, 1), I32)
        thr = jnp.where(count_ge(zero) >= topk, zero, thr)

        def bit_step(i, thr):
            cand = thr | jnp.left_shift(jnp.int32(1), 30 - i)
            return jnp.where(count_ge(cand) >= topk, cand, thr)

        thr_scr[...] = lax.fori_loop(0, 31, bit_step, thr)
        m_scr[...] = jnp.full(m_scr.shape, NEG, F32)
        l_scr[...] = jnp.zeros(l_scr.shape, F32)
        acc_scr[...] = jnp.zeros(acc_scr.shape, F32)

    qbd = qbd_ref[...]
    thr = thr_scr[...]
    sel = jnp.concatenate([key_scr[pl.ds(j * npg + g, 1), :] >= thr for g in range(npg)], axis=1)
    sc = jnp.concatenate([_dot(qbd, k_refs[g][...].astype(BF16)) for g in range(npg)], axis=1)
    sc = jnp.where(sel, sc, NEG)
    m_old = m_scr[...]
    m_new = jnp.maximum(m_old, jnp.max(sc, axis=1, keepdims=True))
    alpha = jnp.exp2(m_old - m_new)
    p = jnp.where(sel, jnp.exp2(sc - m_new), 0.0)
    l_scr[...] = alpha * l_scr[...] + jnp.sum(p, axis=1, keepdims=True)
    pv = jnp.zeros(acc_scr.shape, F32)
    for g in range(npg):
        pv = pv + _dot_nt(p[:, g * PAGE_SIZE:(g + 1) * PAGE_SIZE].astype(BF16), v_refs[g][...].astype(BF16))
    acc_scr[...] = alpha * acc_scr[...] + pv
    m_scr[...] = m_new

    @pl.when(j == nj - 1)
    def _():
        sel = nk_scr[...] >= thr
        sc = jnp.sum(qbd.astype(F32) * kn_ref[...].astype(BF16).astype(F32), axis=1, keepdims=True)
        sc = jnp.where(sel, sc, NEG)
        m_old = m_scr[...]
        m_new = jnp.maximum(m_old, sc)
        alpha = jnp.exp2(m_old - m_new)
        p = jnp.where(sel, jnp.exp2(sc - m_new), 0.0)
        l_new = alpha * l_scr[...] + p
        acc = alpha * acc_scr[...] + p.astype(BF16).astype(F32) * vn_ref[...].astype(BF16).astype(F32)
        o = acc / l_new
        hrow = lax.broadcasted_iota(I32, (A_HEADS, A_WIDTH), 0)
        hcol = lax.broadcasted_iota(I32, (A_HEADS, A_WIDTH), 1) // HEAD_DIM
        o_ref[...] = jnp.sum(jnp.where(hrow == hcol, o, 0.0), axis=0, keepdims=True)


def _attn_sample(pp, cache_k, cache_v, cache_ik, page_table, layer):
    nb, npages = page_table.shape
    npg_s = int(np.gcd(npages, 32))
    npg = int(np.gcd(npages, 16))
    topk = min(TOPK_MAX, (npages * PAGE_SIZE + 1) // 4)
    pt = page_table.reshape(-1).astype(I32)
    n_pool = cache_k.shape[1]
    ck = jnp.transpose(cache_k, (0, 1, 3, 4, 2)).reshape(cache_k.shape[0], n_pool, A_WIDTH, PAGE_SIZE)
    cv = jnp.transpose(cache_v, (0, 1, 3, 4, 2)).reshape(cache_v.shape[0], n_pool, A_WIDTH, PAGE_SIZE)
    cache_ik = jnp.transpose(cache_ik, (0, 1, 3, 2))

    iq = pp["iqpad"]
    iq = jnp.transpose(iq[:, :, :64] + iq[:, :, 64:], (1, 0, 2))
    iw = pp["misc"][:, M_IW:M_IW + IDX_HEADS].reshape(nb, IDX_HEADS, 1)
    ikn = pp["ikdup"][:, :IDX_DIM].reshape(nb, 1, IDX_DIM)
    qp = jnp.transpose(pp["qpad"], (1, 0, 2))
    qbd = jnp.zeros((nb, A_HEADS, A_HEADS // 2, LANES), BF16)
    for h in range(A_HEADS):
        qbd = qbd.at[:, h, h // 2, :].set(qp[:, h, :])
    qbd = qbd.reshape(nb, A_HEADS, A_WIDTH)
    kn = pp["kf"].reshape(nb, 1, A_WIDTH)
    vn = pp["vf"].reshape(nb, 1, A_WIDTH)

    def page_spec(width, g, per_step):
        return pl.BlockSpec((None, None, width, PAGE_SIZE),
                            lambda b, j, ptr: (layer, ptr[b * npages + j * per_step + g], 0, 0))

    per_b = lambda *shape: pl.BlockSpec((None,) + shape, lambda b, j, ptr: (b,) + (0,) * len(shape))

    scores = pl.pallas_call(
        functools.partial(_sample_score_kernel, npg=npg_s),
        grid_spec=pltpu.PrefetchScalarGridSpec(
            num_scalar_prefetch=1,
            grid=(nb, npages // npg_s),
            in_specs=[per_b(IDX_HEADS, IDX_DIM), per_b(IDX_HEADS, 1)]
                     + [page_spec(IDX_DIM, g, npg_s) for g in range(npg_s)],
            out_specs=pl.BlockSpec((None, npg_s, PAGE_SIZE), lambda b, j, ptr: (b, j, 0)),
        ),
        out_shape=jax.ShapeDtypeStruct((nb, npages, PAGE_SIZE), F32),
        compiler_params=_cparams(("arbitrary", "arbitrary")),
        name="sample_score",
    )(pt, iq, iw, *([cache_ik] * npg_s))

    out = pl.pallas_call(
        functools.partial(_sample_attn_kernel, npg=npg, topk=topk),
        grid_spec=pltpu.PrefetchScalarGridSpec(
            num_scalar_prefetch=1,
            grid=(nb, npages // npg),
            in_specs=[per_b(npages, PAGE_SIZE), per_b(IDX_HEADS, IDX_DIM), per_b(IDX_HEADS, 1),
                      per_b(1, IDX_DIM), per_b(A_HEADS, A_WIDTH), per_b(1, A_WIDTH), per_b(1, A_WIDTH)]
                     + [page_spec(A_WIDTH, g, npg) for g in range(npg)]
                     + [page_spec(A_WIDTH, g, npg) for g in range(npg)],
            out_specs=pl.BlockSpec((None, 1, A_WIDTH), lambda b, j, ptr: (b, 0, 0)),
            scratch_shapes=[pltpu.VMEM((npages, PAGE_SIZE), I32),
                            pltpu.VMEM((1, 1), I32),
                            pltpu.VMEM((1, 1), I32),
                            pltpu.VMEM((A_HEADS, 1), F32),
                            pltpu.VMEM((A_HEADS, 1), F32),
                            pltpu.VMEM((A_HEADS, A_WIDTH), F32)],
        ),
        out_shape=jax.ShapeDtypeStruct((nb, 1, A_WIDTH), F32),
        compiler_params=_cparams(("arbitrary", "arbitrary")),
        name="sample_attn",
    )(pt, scores, iq, iw, ikn, qbd, kn, vn, *([ck] * npg), *([cv] * npg))
    return out.reshape(nb, A_WIDTH)


NH = 4
SEQ_PER_STEP = 4
HW = NH * 64


def _bd(y):
    hb = lax.broadcasted_iota(I32, y.shape, 1) // 64
    zero = jnp.zeros_like(y)
    return jnp.concatenate([jnp.where(hb == h, y, zero) for h in range(NH)], axis=0)


def _st1(x, y, form="nn"):
    return _bdot(x.astype(BF16), _bd(y.astype(BF16)), form)


def _st3(x, y):
    xh, xl = _split2(x)
    yh, yl = _split2(y)
    bh, bl = _bd(yh), _bd(yl)
    return _bdot(xh, bh, "nn") + (_bdot(xh, bl, "nn") + _bdot(xl, bh, "nn"))


class _Lock:
    def __init__(self, vals):
        self.v = list(vals)

    def _bin(self, other, f):
        ov = other.v if isinstance(other, _Lock) else [other] * len(self.v)
        return _Lock([f(a, b) for a, b in zip(self.v, ov)])

    def __add__(self, o):
        return self._bin(o, lambda a, b: a + b)

    def __radd__(self, o):
        return self._bin(o, lambda a, b: b + a)

    def __sub__(self, o):
        return self._bin(o, lambda a, b: a - b)

    def __rsub__(self, o):
        return self._bin(o, lambda a, b: b - a)

    def __mul__(self, o):
        return self._bin(o, lambda a, b: a * b)

    def __rmul__(self, o):
        return self._bin(o, lambda a, b: b * a)

    def __neg__(self):
        return _Lock([-a for a in self.v])

    def __getitem__(self, idx):
        return _Lock([a[idx] for a in self.v])


def _lift(f):
    def g(*args, **kw):
        n = max(len(a.v) for a in args if isinstance(a, _Lock))
        pick = lambda a, i: a.v[i] if isinstance(a, _Lock) else a
        return _Lock([f(*[pick(a, i) for a in args], **kw) for i in range(n)])
    return g


def _st_unit_lower_inverse(mm, r, q):
    st3, where = _lift(_st3), _lift(jnp.where)
    pw = where(r // 16 == q // 16, -mm, 0.0)
    inv = pw + (r == q).astype(F32)
    k = 1
    while 2 * k < 16:
        pw = st3(pw, pw)
        inv = inv + st3(inv, pw)
        k *= 2
    for sz in (32, 64):
        half = sz // 2
        e = where(jnp.logical_and(r // sz == q // sz, r // half != q // half), mm, 0.0)
        inv = inv - st3(inv, st3(e, inv))
    return inv


def _delta_st_kernel(xc_ref, bg_ref, ms_ref, buf_ref, s0_ref, cw_ref, al_ref, dt_ref, dn_ref, bd512_ref,
                     o_ref, sout_ref, nbuf_ref, xp_scr, s_scr, *, c, t_valid, nt):
    t = pl.program_id(1)
    ns = SEQ_PER_STEP
    seqs = range(ns)
    where, exp = _lift(jnp.where), _lift(jnp.exp)
    st1, st3, mm1, mm3, msel = _lift(_st1), _lift(_st3), _lift(_mm1), _lift(_mm3), _lift(_mm_sel)

    @pl.when(t == 0)
    def _():
        xp_scr[:, 5:8, :] = buf_ref[...]
        s_scr[...] = s0_ref[...]

    xp_scr[:, 8:8 + c, :] = xc_ref[...]
    cw = cw_ref[...]
    y = _Lock([cw[0:1] * xp_scr[j, 5:5 + c, :] + cw[1:2] * xp_scr[j, 6:6 + c, :]
               + cw[2:3] * xp_scr[j, 7:7 + c, :] + cw[3:4] * xp_scr[j, 8:8 + c, :] for j in seqs])

    tb, lt = (t_valid - 1) // c, (t_valid - 1) % c

    @pl.when(t == tb)
    def _():
        nbuf_ref[...] = xp_scr[:, 8 + lt - 2:8 + lt + 1, :]

    hist = xp_scr[:, c + 5:c + 8, :]
    xp_scr[:, 5:8, :] = hist

    y = _lift(_silu)(y)
    ss = _lift(_group_sum)(y[:, 0:512] * y[:, 0:512], bd512_ref[...])
    inv_n = _lift(lax.rsqrt)(ss + EPS)
    qn = y[:, 0:HW] * inv_n[:, 0:HW] * (B_DK ** -0.5)
    kn = y[:, HW:2 * HW] * inv_n[:, HW:2 * HW]
    vv = y[:, 2 * HW:3 * HW]
    ms = _Lock([ms_ref[j] for j in seqs])
    g_all = _lift(_softplus)(ms + dt_ref[...]) * (-jnp.exp(al_ref[...]))
    beta_all = _lift(jax.nn.sigmoid)(ms)
    if t_valid % c != 0:
        vm = (t * c + lax.broadcasted_iota(I32, (c, 1), 0) < t_valid).astype(F32)
        g_all = g_all * vm
        beta_all = beta_all * vm
        kn = kn * vm
        vv = vv * vm

    r = lax.broadcasted_iota(I32, (c, HW), 0)
    lane = lax.broadcasted_iota(I32, (c, HW), 1)
    q = lane % 64
    hb = lane // 64
    incl = r >= q
    strict = r > q
    r64 = lax.broadcasted_iota(I32, (c, c), 0)
    q64 = lax.broadcasted_iota(I32, (c, c), 1)
    gcum_all = msel(r64 >= q64, g_all)
    lane128 = lax.broadcasted_iota(I32, (c, LANES), 1)
    gc = _Lock([jnp.zeros((c, HW), F32)] * ns)
    bt = _Lock([jnp.zeros((c, HW), F32)] * ns)
    for h in range(NH):
        gc = where(hb == h, gcum_all[:, M_BA + h:M_BA + h + 1], gc)
        bt = where(hb == h, beta_all[:, M_BB + h:M_BB + h + 1], bt)
    gsel = _lift(lambda g: jnp.concatenate([jnp.where(lane128 == M_BA + h, g, 0.0) for h in range(NH)],
                                           axis=0))(gcum_all)
    grow = msel(jnp.ones((c, LANES), BF16), gsel, "nt")
    dec = exp(where(incl, gc - grow, 0.0))
    knb = _lift(lambda a: a.astype(BF16))(kn)
    kbd = _lift(_bd)(knb)
    kk = _lift(_bdot)(knb, kbd, "nt")
    qk = _lift(_bdot)(_lift(lambda a: a.astype(BF16))(qn), kbd, "nt")
    mm = where(strict, bt * kk * dec, 0.0)
    inv = _st_unit_lower_inverse(mm, r, q)
    egc = exp(gc)
    uu = st3(inv, bt * vv)
    ww = st3(inv, (bt * egc) * kn)
    s_old = _Lock([s_scr[j] for j in seqs])
    delta = uu - mm3(ww, s_old)
    o = mm1(qn * egc, s_old) + st1(where(incl, qk * dec, 0.0), delta)
    glast = gc[c - 1:c, :]
    kd = kn * exp(glast - gc)
    rb = lax.broadcasted_iota(I32, (HW, HW), 0) // 64
    cb = lax.broadcasted_iota(I32, (HW, HW), 1) // 64
    gl = _Lock([jnp.zeros((HW, HW), F32)] * ns)
    for h in range(NH):
        gl = where(rb == h, exp(gcum_all[c - 1:c, M_BA + h:M_BA + h + 1]), gl)
    s_new = gl * s_old + where(rb == cb, mm3(kd, delta, "tn"), 0.0)
    ms_o = _lift(_group_sum)(o * o, bd512_ref[0:HW, 0:HW]) * (1.0 / 64)
    out = o * _lift(lax.rsqrt)(ms_o + EPS) * dn_ref[...] * _lift(_silu)(_Lock([bg_ref[j] for j in seqs]))
    for j in seqs:
        s_scr[j] = s_new.v[j]
        o_ref[j] = out.v[j]

    @pl.when(t == nt - 1)
    def _():
        sout_ref[...] = s_scr[...]


def _delta_st(z, nb, t_pad, t_valid, buf0, s0_bd, cw, al_row, dt_row, dn_row, bd512):
    c = CHUNK
    ns = SEQ_PER_STEP
    assert c == 64 and nb % ns == 0
    nt = t_pad // c
    z3 = z.reshape(nb, t_pad, NZ)
    const = lambda *shape: pl.BlockSpec(shape, lambda b, t: (0,) * len(shape))
    tok = lambda w, col: pl.BlockSpec((ns, c, w), lambda b, t: (b, t, col))
    per_seq = lambda *shape: pl.BlockSpec((ns,) + shape, lambda b, t: (b,) + (0,) * len(shape))
    o, s_out, nbuf = pl.pallas_call(
        functools.partial(_delta_st_kernel, c=c, t_valid=t_valid, nt=nt),
        grid=(nb // ns, nt),
        in_specs=[tok(B_CONV, Z_CONV // B_CONV), tok(HW, Z_BG // HW), tok(LANES, Z_MISC // LANES),
                  per_seq(CONV_W - 1, B_CONV), per_seq(HW, HW),
                  const(CONV_W, B_CONV), const(1, LANES), const(1, LANES), const(1, HW),
                  const(512, 512)],
        out_specs=[tok(HW, 0), per_seq(HW, HW), per_seq(CONV_W - 1, B_CONV)],
        out_shape=[jax.ShapeDtypeStruct((nb, t_pad, HW), F32),
                   jax.ShapeDtypeStruct((nb, HW, HW), F32),
                   jax.ShapeDtypeStruct((nb, CONV_W - 1, B_CONV), F32)],
        scratch_shapes=[pltpu.VMEM((ns, 8 + c, B_CONV), F32), pltpu.VMEM((ns, HW, HW), F32)],
        compiler_params=_cparams(("arbitrary", "arbitrary")),
        name="delta",
    )(z3, z3, z3, buf0, s0_bd, cw, al_row, dt_row, dn_row, bd512)
    return o.reshape(nb * t_pad, HW), s_out, nbuf


def _hgrn_st_kernel(x_ref, s0_ref, lb_ref, cn_ref, bd512_ref, o_ref, sout_ref, s_scr, *, c, t_valid, nt):
    t = pl.program_id(1)
    seqs = range(SEQ_PER_STEP)
    where, exp, minimum = _lift(jnp.where), _lift(jnp.exp), _lift(jnp.minimum)
    st1, mm1, mm3, msel, gsum = _lift(_st1), _lift(_mm1), _lift(_mm3), _lift(_mm_sel), _lift(_group_sum)

    @pl.when(t == 0)
    def _():
        s_scr[...] = s0_ref[...]

    lb = lb_ref[...]
    cf = _Lock([x_ref[j, :, HW:2 * HW] for j in seqs])
    logf = _lift(jnp.log)(_lift(jax.nn.sigmoid)(cf) * (1.0 - lb) + lb)
    kk = _lift(jax.nn.sigmoid)(-cf) * (1.0 - lb)
    qq = _lift(_silu)(_Lock([x_ref[j, :, 0:HW] for j in seqs]))
    vv = _Lock([x_ref[j, :, 2 * HW:3 * HW] for j in seqs])
    if t_valid % c != 0:
        vm = (t * c + lax.broadcasted_iota(I32, (c, 1), 0) < t_valid).astype(F32)
        logf = logf * vm
        kk = kk * vm

    r64 = lax.broadcasted_iota(I32, (c, c), 0)
    q64 = lax.broadcasted_iota(I32, (c, c), 1)
    b = msel(r64 >= q64, logf)
    r = lax.broadcasted_iota(I32, (c, HW), 0)
    q = lax.broadcasted_iota(I32, (c, HW), 1) % 64

    amat = _Lock([jnp.zeros((c, HW), F32)] * SEQ_PER_STEP)
    half = c // 2
    while half >= 1:
        sz = 2 * half
        bnd = (r64 // sz) * sz + half - 1
        rl = msel(q64 == bnd, b)
        qf = qq * exp(minimum(b - rl, 0.0))
        kf = kk * exp(minimum(rl - b, 0.0))
        lmask = jnp.logical_and(r // sz == q // sz, jnp.logical_and(r % sz >= half, q % sz < half))
        amat = amat + where(lmask, st1(qf, kf, "nt"), 0.0)
        half //= 2

    st_old = _Lock([s_scr[j] for j in seqs])
    o = mm1(qq * exp(b), st_old, "nt") + st1(amat, vv)
    o = o + gsum(qq * kk, bd512_ref[0:HW, 0:HW]) * vv
    blast = b[c - 1:c, :]
    kd = kk * exp(blast - b)
    rb = lax.broadcasted_iota(I32, (HW, HW), 0) // 64
    cb = lax.broadcasted_iota(I32, (HW, HW), 1) // 64
    st_new = st_old * exp(blast) + where(rb == cb, mm3(vv, kd, "tn"), 0.0)
    ms_o = gsum(o * o, bd512_ref[0:HW, 0:HW]) * (1.0 / 64)
    out = o * _lift(lax.rsqrt)(ms_o + EPS) * cn_ref[...]
    for j in seqs:
        s_scr[j] = st_new.v[j]
        o_ref[j] = out.v[j]

    @pl.when(t == nt - 1)
    def _():
        sout_ref[...] = s_scr[...]


def _hgrn_st(z, nb, t_pad, t_valid, s0_bd, lb_row, cn_row, bd512):
    c = CHUNK
    ns = SEQ_PER_STEP
    assert c == 64 and nb % ns == 0
    nt = t_pad // c
    z3 = z.reshape(nb, t_pad, NZ)
    const = lambda *shape: pl.BlockSpec(shape, lambda b, t: (0,) * len(shape))
    per_seq = lambda *shape: pl.BlockSpec((ns,) + shape, lambda b, t: (b,) + (0,) * len(shape))
    o, s_out = pl.pallas_call(
        functools.partial(_hgrn_st_kernel, c=c, t_valid=t_valid, nt=nt),
        grid=(nb // ns, nt),
        in_specs=[pl.BlockSpec((ns, c, 768), lambda b, t: (b, t, Z_HG // 768)),
                  per_seq(HW, HW),
                  const(1, HW), const(1, HW), const(512, 512)],
        out_specs=[pl.BlockSpec((ns, c, HW), lambda b, t: (b, t, 0)), per_seq(HW, HW)],
        out_shape=[jax.ShapeDtypeStruct((nb, t_pad, HW), F32),
                   jax.ShapeDtypeStruct((nb, HW, HW), F32)],
        scratch_shapes=[pltpu.VMEM((ns, HW, HW), F32)],
        compiler_params=_cparams(("arbitrary", "arbitrary")),
        name="hgrn",
    )(z3, s0_bd, lb_row, cn_row, bd512)
    return o.reshape(nb * t_pad, HW), s_out


def _out_mlp_kernel(x_ref, oa_ref, ob_ref, oc_ref, wo_ref, g2_ref, w1_ref, w2_ref, y_ref, h_scr):
    @pl.when(pl.program_id(1) == 0)
    def _():
        x1 = (x_ref[...]
              + _dot(oa_ref[...].astype(BF16), wo_ref[0:A_WIDTH, :])
              + _dot(ob_ref[...].astype(BF16), wo_ref[A_WIDTH:A_WIDTH + 256, :])
              + _dot(oc_ref[...].astype(BF16), wo_ref[A_WIDTH + 256:A_WIDTH + 512, :]))
        y_ref[...] = x1
        ms = jnp.mean(x1 * x1, axis=-1, keepdims=True)
        h_scr[...] = (x1 * lax.rsqrt(ms + EPS) * g2_ref[...]).astype(BF16)

    u = jnp.maximum(_dot(h_scr[...], w1_ref[...]), 0.0)
    y_ref[...] += _dot((u * u).astype(BF16), w2_ref[...])


def _out_mlp(x, oa, ob, oc, wo, g2, w1, w2):
    m, d = x.shape
    tm = min(512, m)
    tf = 512
    nf = w1.shape[1] // tf
    tok = lambda w: pl.BlockSpec((tm, w), lambda i, j: (i, 0))
    return pl.pallas_call(
        _out_mlp_kernel,
        grid=(m // tm, nf),
        in_specs=[tok(d), tok(A_WIDTH), tok(256), tok(256),
                  pl.BlockSpec((d, d), lambda i, j: (0, 0)),
                  pl.BlockSpec((1, d), lambda i, j: (0, 0)),
                  pl.BlockSpec((d, tf), lambda i, j: (0, j)),
                  pl.BlockSpec((tf, d), lambda i, j: (j, 0))],
        out_specs=tok(d),
        out_shape=jax.ShapeDtypeStruct((m, d), F32),
        scratch_shapes=[pltpu.VMEM((tm, d), BF16)],
        compiler_params=_cparams(("parallel", "arbitrary")),
        name="out_mlp",
    )(x, oa, ob, oc, wo, g2, w1, w2)


def _relayout_w_in(w):
    d = w.shape[0]
    off = np.cumsum([0, 512, 512, 512, 1024, 16, 64, 256, 256, 256, 4, 4, 256, 256, 256, 256])
    (aq, ak, av, iq, iw, ik, bq, bk, bv, ba, bb, bg, cq, cf, ci, end) = [int(o) for o in off]
    zpad = lambda n: jnp.zeros((d, n), w.dtype)
    cols = [w[:, aq:iw],
            w[:, bg:cq],
            w[:, ik:bq], w[:, iw:ik], w[:, ba:bb], w[:, bb:bg], zpad(LANES - 88),
            zpad(Z_CONV - Z_MISC - LANES),
            w[:, bq:ba],
            w[:, cq:end]]
    out = jnp.concatenate(cols, axis=1)
    assert out.shape[1] == NZ
    return out.astype(BF16)


def _rope_tables(pos):
    pos = pos.astype(F32)
    half = HEAD_DIM // 2
    inv = ROPE_THETA ** (-jnp.arange(half, dtype=F32) / half)
    ang = pos[:, None] * inv[None, :]
    cos, sin = jnp.cos(ang), jnp.sin(ang)
    ca = jnp.tile(jnp.concatenate([cos, cos], -1), (1, 2))
    sa = jnp.tile(jnp.concatenate([-sin, sin], -1), (1, 2))
    half_i = IDX_ROPE // 2
    inv_i = ROPE_THETA ** (-jnp.arange(half_i, dtype=F32) / half_i)
    ang_i = pos[:, None] * inv_i[None, :]
    cos_i, sin_i = jnp.cos(ang_i), jnp.sin(ang_i)
    rest = IDX_DIM - IDX_ROPE
    one = jnp.ones((pos.shape[0], rest), F32)
    zero = jnp.zeros((pos.shape[0], rest), F32)
    ci = jnp.tile(jnp.concatenate([cos_i, cos_i, one], -1), (1, 2))
    si = jnp.tile(jnp.concatenate([-sin_i, sin_i, zero], -1), (1, 2))
    return ca, sa, ci, si


def _to_bd4(s):
    z = jnp.zeros_like(s[:, 0])
    rows = [jnp.concatenate([s[:, h] if g == h else z for g in range(NH)], axis=-1) for h in range(NH)]
    return jnp.concatenate(rows, axis=-2)


def _from_bd4(p):
    return jnp.stack([p[:, 64 * h:64 * (h + 1), 64 * h:64 * (h + 1)] for h in range(NH)], axis=1)


def _lane_row(vals, offset):
    return jnp.zeros((1, LANES), F32).at[0, offset:offset + vals.shape[0]].set(vals.astype(F32))


def _block_diag_ones(n):
    i = np.arange(n) // 64
    return jnp.asarray((i[:, None] == i[None, :]).astype(np.float32)).astype(BF16)


def kernel(x_prompt, x_sample, cache_k, cache_v, cache_idx_k, state_delta, state_delta_conv, state_hgrn,
           page_table, norm1_g, w_in, q_norm_g, k_norm_g, idx_k_ln_g, idx_k_ln_b, conv_w, a_log, dt_bias,
           delta_norm_g, lb_param, hgrn_norm_g, w_out, norm2_g, w_mlp_in, w_mlp_out):
    depth = w_in.shape[0]
    bp, s_len, d = x_prompt.shape
    bd_, t_dec, _ = x_sample.shape
    assert t_dec == 1 and d == D_MODEL
    past = page_table.shape[1] * PAGE_SIZE

    lb_all = jnp.cumsum(jax.nn.softmax(lb_param.astype(F32), axis=0), axis=0)
    lb_all = lb_all - lb_all[0:1]
    tabs_p = _rope_tables(jnp.arange(s_len))
    tabs_s = _rope_tables(past + jnp.arange(t_dec))
    bd512 = _block_diag_ones(512)

    xp = x_prompt.reshape(bp * s_len, d)
    xs = x_sample.reshape(bd_ * t_dec, d)
    t_pad_s = CHUNK
    zeros_buf = jnp.zeros((bp, CONV_W - 1, B_CONV), F32)
    zeros_state = jnp.zeros((bp, HW, HW), F32)

    outs_p = {k: [] for k in ("k", "v", "ik", "ds", "dc", "hs")}
    outs_s = {k: [] for k in ("k", "v", "ik", "ds", "dc", "hs")}
    for l in range(depth):
        w_in_l = _relayout_w_in(w_in[l])
        g1 = norm1_g[l].reshape(1, d)
        qg = jnp.tile(q_norm_g[l], A_HEADS).reshape(1, A_WIDTH)
        kg = jnp.tile(k_norm_g[l], A_HEADS).reshape(1, A_WIDTH)
        ikg = _lane_row(idx_k_ln_g[l], 0)
        ikb = _lane_row(idx_k_ln_b[l], 0)
        al_row = _lane_row(a_log[l], M_BA)
        dt_row = _lane_row(dt_bias[l], M_BA)
        dn_row = jnp.tile(delta_norm_g[l], NH).reshape(1, HW).astype(F32)
        cn_row = jnp.tile(hgrn_norm_g[l], NH).reshape(1, HW).astype(F32)
        lb_row = lb_all[l].reshape(1, 256)
        wo = w_out[l].astype(BF16)
        g2 = norm2_g[l].reshape(1, d)
        w1 = w_mlp_in[l].astype(BF16)
        w2 = w_mlp_out[l].astype(BF16)
        cw = conv_w[l]

        z = _norm_matmul(xp, g1, w_in_l, 768)
        pp = _prep_t(z, tabs_p, bp, s_len, qg, kg, ikg, ikb, bd512)
        oa = _attn_prompt(pp, bp, s_len)
        ob, ds, dc = _delta_st(z, bp, s_len, s_len, zeros_buf, zeros_state, cw, al_row, dt_row, dn_row, bd512)
        oc, hs = _hgrn_st(z, bp, s_len, s_len, zeros_state, lb_row, cn_row, bd512)
        xp = _out_mlp(xp, oa, ob, oc, wo, g2, w1, w2)
        heads_last = lambda a: jnp.transpose(a.reshape(bp, A_HEADS, HEAD_DIM, s_len), (0, 3, 1, 2))
        outs_p["k"].append(heads_last(pp["kt"]))
        outs_p["v"].append(heads_last(pp["vt"]))
        outs_p["ik"].append(jnp.transpose(pp["ikt"], (0, 2, 1)))
        outs_p["ds"].append(_from_bd4(ds))
        outs_p["dc"].append(dc)
        outs_p["hs"].append(jnp.swapaxes(_from_bd4(hs), -1, -2))

        zs = _norm_matmul(xs, g1, w_in_l, 768)
        ps = _prep(zs, tabs_s, qg, kg, ikg, ikb, bd512)
        oa_s = _attn_sample(ps, cache_k, cache_v, cache_idx_k, page_table, l)
        zs_pad = jnp.pad(zs.reshape(bd_, 1, NZ), ((0, 0), (0, t_pad_s - 1), (0, 0))).reshape(bd_ * t_pad_s, NZ)
        ob_s, ds_s, dc_s = _delta_st(zs_pad, bd_, t_pad_s, 1, state_delta_conv[l].astype(F32),
                                     _to_bd4(state_delta[l].astype(F32)), cw, al_row, dt_row, dn_row, bd512)
        oc_s, hs_s = _hgrn_st(zs_pad, bd_, t_pad_s, 1, _to_bd4(jnp.swapaxes(state_hgrn[l].astype(F32), -1, -2)),
                              lb_row, cn_row, bd512)
        ob_s = ob_s.reshape(bd_, t_pad_s, 256)[:, 0]
        oc_s = oc_s.reshape(bd_, t_pad_s, 256)[:, 0]
        xs = _out_mlp(xs, oa_s, ob_s, oc_s, wo, g2, w1, w2)
        outs_s["k"].append(ps["kf"].reshape(bd_, t_dec, A_HEADS, HEAD_DIM))
        outs_s["v"].append(ps["vf"].reshape(bd_, t_dec, A_HEADS, HEAD_DIM))
        outs_s["ik"].append(ps["ikf"].reshape(bd_, t_dec, IDX_DIM))
        outs_s["ds"].append(_from_bd4(ds_s))
        outs_s["dc"].append(dc_s)
        outs_s["hs"].append(jnp.swapaxes(_from_bd4(hs_s), -1, -2))

    st = lambda o, k: jnp.stack(o[k])
    return (xp.reshape(bp, s_len, d), xs.reshape(bd_, t_dec, d),
            st(outs_p, "k"), st(outs_p, "v"), st(outs_p, "ik"), st(outs_p, "ds"), st(outs_p, "dc"), st(outs_p, "hs"),
            st(outs_s, "k"), st(outs_s, "v"), st(outs_s, "ik"), st(outs_s, "ds"), st(outs_s, "dc"), st(outs_s, "hs"))
```

```python
import functools

import numpy as np
import jax
import jax.numpy as jnp
from jax import lax
from jax.experimental import pallas as pl
from jax.experimental.pallas import tpu as pltpu

F32 = jnp.float32
BF16 = jnp.bfloat16
I32 = jnp.int32
I16 = jnp.int16

D_MODEL = 1024
HEAD_DIM = 64
A_HEADS = 8
A_WIDTH = A_HEADS * HEAD_DIM
IDX_HEADS = 16
IDX_DIM = 64
IDX_ROPE = 32
TOPK_MAX = 256
B_HEADS = 4
B_DK = 64
B_CONV = 768
CONV_W = 4
C_HEADS = 4
D_FF = 4 * D_MODEL
ROPE_THETA = 10000.0
EPS = 1e-6
PAGE_SIZE = 128
CHUNK = 64

LANES = 128
VMEM_LIMIT = 48 * 1024 * 1024

Z_IQ = 3 * A_WIDTH
Z_BG = Z_IQ + IDX_HEADS * IDX_DIM
Z_MISC = Z_BG + 256
Z_CONV = 3072
Z_HG = Z_CONV + B_CONV
NZ = Z_HG + 768
M_IW = 64
M_BA = 80
M_BB = 84

NEG = -1e30
INT_MIN = -2 ** 31
LOG2E = 1.4426950408889634
KSPLIT = 4


def _cparams(sem):
    return pltpu.CompilerParams(dimension_semantics=sem, vmem_limit_bytes=VMEM_LIMIT)


def _dot(a, b, precision=None):
    return jnp.dot(a, b, preferred_element_type=F32, precision=precision)


def _dot_nt(a, b, precision=None):
    return lax.dot_general(a, b, (((1,), (1,)), ((), ())), preferred_element_type=F32, precision=precision)


_DIMS = {"nn": (((1,), (0,)), ((), ())), "nt": (((1,), (1,)), ((), ())), "tn": (((0,), (0,)), ((), ()))}


def _bdot(a, b, form):
    return lax.dot_general(a, b, _DIMS[form], preferred_element_type=F32)


def _split2(x):
    hi = x.astype(BF16)
    return hi, (x - hi.astype(F32)).astype(BF16)


def _split3(x):
    hi = x.astype(BF16)
    r = x - hi.astype(F32)
    mid = r.astype(BF16)
    return hi, mid, (r - mid.astype(F32)).astype(BF16)


def _mm1(a, b, form="nn"):
    return _bdot(a.astype(BF16), b.astype(BF16), form)


def _mm3(a, b, form="nn"):
    ah, al = _split2(a)
    bh, bl = _split2(b)
    return _bdot(ah, bh, form) + (_bdot(ah, bl, form) + _bdot(al, bh, form))


def _mm_sel(a01, b, form="nn"):
    a = a01.astype(BF16)
    h, m, l = _split3(b)
    return _bdot(a, h, form) + (_bdot(a, m, form) + _bdot(a, l, form))


def _group_sum(y, bd):
    hi = y.astype(BF16)
    lo = (y - hi.astype(F32)).astype(BF16)
    return _dot(hi, bd) + _dot(lo, bd)


def _to_key(x):
    bits = lax.bitcast_convert_type(x, I32)
    return jnp.where(bits < 0, bits ^ jnp.int32(0x7FFFFFFF), bits)


def _silu(x):
    return x * jax.nn.sigmoid(x)


def _softplus(x):
    return jnp.maximum(x, 0.0) + jnp.log(1.0 + jnp.exp(-jnp.abs(x)))


def _norm_matmul_kernel(x_ref, g_ref, w_ref, o_ref, h_scr):
    @pl.when(pl.program_id(1) == 0)
    def _():
        x = x_ref[...]
        ms = jnp.mean(x * x, axis=-1, keepdims=True)
        h_scr[...] = (x * lax.rsqrt(ms + EPS) * g_ref[...]).astype(BF16)

    o_ref[...] = _dot(h_scr[...], w_ref[...])


def _norm_matmul(x, g, w, tn):
    m, d = x.shape
    n = w.shape[1]
    tm = min(512, m)
    return pl.pallas_call(
        _norm_matmul_kernel,
        grid=(m // tm, n // tn),
        in_specs=[pl.BlockSpec((tm, d), lambda i, j: (i, 0)),
                  pl.BlockSpec((1, d), lambda i, j: (0, 0)),
                  pl.BlockSpec((d, tn), lambda i, j: (0, j))],
        out_specs=pl.BlockSpec((tm, tn), lambda i, j: (i, j)),
        out_shape=jax.ShapeDtypeStruct((m, n), F32),
        scratch_shapes=[pltpu.VMEM((tm, d), BF16)],
        compiler_params=_cparams(("parallel", "arbitrary")),
        name="norm_matmul",
    )(x, g, w)


def _prep_values(za_ref, zm_ref, ca_ref, sa_ref, ci_ref, si_ref, qg_ref, kg_ref, ikg_ref, ikb_ref, bd_ref):
    tm = za_ref.shape[0]
    bd = bd_ref[...]
    ca = jnp.concatenate([ca_ref[...]] * 4, axis=1)
    sa = jnp.concatenate([sa_ref[...]] * 4, axis=1)
    lane_a = lax.broadcasted_iota(I32, (tm, A_WIDTH), 1)
    lo_half = (lane_a % HEAD_DIM) < (HEAD_DIM // 2)
    lane128 = lax.broadcasted_iota(I32, (tm, LANES), 1)
    left = lane128 < 64

    def norm_rope(x, g):
        ss = _group_sum(x * x, bd)
        y = x * lax.rsqrt(ss * (1.0 / HEAD_DIM) + EPS) * g
        sw = jnp.where(lo_half, pltpu.roll(y, A_WIDTH - 32, 1), pltpu.roll(y, 32, 1))
        return y * ca + sw * sa

    q = norm_rope(za_ref[:, 0:A_WIDTH], qg_ref[...]) * (HEAD_DIM ** -0.5 * LOG2E)
    k = norm_rope(za_ref[:, A_WIDTH:2 * A_WIDTH], kg_ref[...])
    v = za_ref[:, 2 * A_WIDTH:3 * A_WIDTH]

    nq = IDX_HEADS * IDX_DIM
    xi = za_ref[:, Z_IQ:Z_IQ + nq]
    lane_i = lax.broadcasted_iota(I32, (tm, nq), 1)
    ci = jnp.concatenate([ci_ref[...]] * 8, axis=1)
    si = jnp.concatenate([si_ref[...]] * 8, axis=1)
    swi = jnp.where((lane_i % IDX_DIM) < 16, pltpu.roll(xi, nq - 16, 1), pltpu.roll(xi, 16, 1))
    iq = xi * ci + swi * si

    m = zm_ref[...]
    mu = jnp.sum(jnp.where(left, m, 0.0), axis=-1, keepdims=True) * (1.0 / IDX_DIM)
    xc = jnp.where(left, m - mu, 0.0)
    var = jnp.sum(xc * xc, axis=-1, keepdims=True) * (1.0 / IDX_DIM)
    y = xc * lax.rsqrt(var + EPS) * ikg_ref[...] + ikb_ref[...]
    swk = jnp.where(lane128 < 16, pltpu.roll(y, LANES - 16, 1), pltpu.roll(y, 16, 1))
    ikr = y * ci_ref[...] + swk * si_ref[...]
    ikdup = jnp.where(left, ikr, pltpu.roll(ikr, 64, 1)).astype(BF16)
    misc = m * (IDX_HEADS ** -0.5 * IDX_DIM ** -0.5)
    return q, k, v, iq, ikr, ikdup, misc


def _prep_kernel(*refs):
    ins, (qpad_ref, kf_ref, vf_ref, iqpad_ref, ikf_ref, ikdup_ref, misc_ref) = refs[:11], refs[11:]
    q, k, v, iq, ikr, ikdup, misc = _prep_values(*ins)
    left = lax.broadcasted_iota(I32, (q.shape[0], LANES), 1) < 64
    for h in range(A_HEADS):
        keep = left if h % 2 == 0 else jnp.logical_not(left)
        qpad_ref[h] = jnp.where(keep, q[:, (h // 2) * LANES:(h // 2 + 1) * LANES], 0.0).astype(BF16)
    for h in range(IDX_HEADS):
        keep = left if h % 2 == 0 else jnp.logical_not(left)
        iqpad_ref[h] = jnp.where(keep, iq[:, (h // 2) * LANES:(h // 2 + 1) * LANES], 0.0).astype(BF16)
    kf_ref[...] = k
    vf_ref[...] = v
    ikf_ref[...] = ikr[:, 0:IDX_DIM]
    ikdup_ref[...] = ikdup
    misc_ref[...] = misc


def _prep_t_kernel(*refs):
    ins, (qt_ref, kt_ref, kb_ref, vt_ref, vaug_ref, iqt_ref, ikt_ref, ikdup_ref, iwt_ref) = refs[:11], refs[11:]
    q, k, v, iq, ikr, ikdup, misc = _prep_values(*ins)
    tm = q.shape[0]
    top = lax.broadcasted_iota(I32, (LANES, tm), 0) < 64
    qt = q.T
    vt = v.T
    iqt = iq.T
    for h in range(A_HEADS):
        keep = top if h % 2 == 0 else jnp.logical_not(top)
        rows = slice((h // 2) * LANES, (h // 2 + 1) * LANES)
        qt_ref[h] = jnp.where(keep, qt[rows, :], 0.0).astype(BF16)
        vaug_ref[h] = jnp.where(keep, vt[rows, :], 1.0).astype(BF16)
    for h in range(IDX_HEADS):
        keep = top if h % 2 == 0 else jnp.logical_not(top)
        iqt_ref[h] = jnp.where(keep, iqt[(h // 2) * LANES:(h // 2 + 1) * LANES, :], 0.0).astype(BF16)
    kt_ref[...] = k.T
    kb_ref[...] = k.astype(BF16)
    vt_ref[...] = vt
    ikt_ref[...] = ikr.T[0:IDX_DIM, :]
    ikdup_ref[...] = ikdup
    iwt_ref[...] = misc.T[M_IW:M_IW + IDX_HEADS, :]


def _prep_in_specs(tm, tab_spec):
    row = lambda w: pl.BlockSpec((1, w), lambda i: (0, 0))
    return [pl.BlockSpec((tm, Z_BG), lambda i: (i, 0)),
            pl.BlockSpec((tm, LANES), lambda i: (i, Z_MISC // LANES)),
            tab_spec, tab_spec, tab_spec, tab_spec,
            row(A_WIDTH), row(A_WIDTH), row(LANES), row(LANES),
            pl.BlockSpec((A_WIDTH, A_WIDTH), lambda i: (0, 0))]


def _prep(z, tabs, qg, kg, ikg, ikb, bd512):
    m = z.shape[0]
    tm = m
    tab_spec = pl.BlockSpec((1, LANES), lambda i: (0, 0))
    tok = lambda w: pl.BlockSpec((tm, w), lambda i: (i, 0))
    outs = pl.pallas_call(
        _prep_kernel,
        grid=(m // tm,),
        in_specs=_prep_in_specs(tm, tab_spec),
        out_specs=[pl.BlockSpec((A_HEADS, tm, LANES), lambda i: (0, i, 0)),
                   tok(A_WIDTH), tok(A_WIDTH),
                   pl.BlockSpec((IDX_HEADS, tm, LANES), lambda i: (0, i, 0)),
                   tok(IDX_DIM), tok(LANES), tok(LANES)],
        out_shape=[jax.ShapeDtypeStruct((A_HEADS, m, LANES), BF16),
                   jax.ShapeDtypeStruct((m, A_WIDTH), F32),
                   jax.ShapeDtypeStruct((m, A_WIDTH), F32),
                   jax.ShapeDtypeStruct((IDX_HEADS, m, LANES), BF16),
                   jax.ShapeDtypeStruct((m, IDX_DIM), F32),
                   jax.ShapeDtypeStruct((m, LANES), BF16),
                   jax.ShapeDtypeStruct((m, LANES), F32)],
        compiler_params=_cparams(("parallel",)),
        name="prep",
    )(z, z, *tabs, qg, kg, ikg, ikb, bd512)
    names = ("qpad", "kf", "vf", "iqpad", "ikf", "ikdup", "misc")
    return dict(zip(names, outs))


def _prep_t(z, tabs, nb, t_seq, qg, kg, ikg, ikb, bd512):
    m = z.shape[0]
    tm = min(256, t_seq)
    assert t_seq % tm == 0
    nblk = t_seq // tm
    tab_spec = pl.BlockSpec((tm, LANES), lambda i: (i % nblk, 0))
    feat = lambda f: pl.BlockSpec((None, f, tm), lambda i: (i // nblk, 0, i % nblk))
    heads = lambda nh: pl.BlockSpec((nh, LANES, tm), lambda i: (0, 0, i))
    outs = pl.pallas_call(
        _prep_t_kernel,
        grid=(m // tm,),
        in_specs=_prep_in_specs(tm, tab_spec),
        out_specs=[heads(A_HEADS), feat(A_WIDTH),
                   pl.BlockSpec((tm, A_WIDTH), lambda i: (i, 0)),
                   feat(A_WIDTH), heads(A_HEADS), heads(IDX_HEADS), feat(IDX_DIM),
                   pl.BlockSpec((tm, LANES), lambda i: (i, 0)),
                   pl.BlockSpec((IDX_HEADS, tm), lambda i: (0, i))],
        out_shape=[jax.ShapeDtypeStruct((A_HEADS, LANES, m), BF16),
                   jax.ShapeDtypeStruct((nb, A_WIDTH, t_seq), F32),
                   jax.ShapeDtypeStruct((m, A_WIDTH), BF16),
                   jax.ShapeDtypeStruct((nb, A_WIDTH, t_seq), F32),
                   jax.ShapeDtypeStruct((A_HEADS, LANES, m), BF16),
                   jax.ShapeDtypeStruct((IDX_HEADS, LANES, m), BF16),
                   jax.ShapeDtypeStruct((nb, IDX_DIM, t_seq), F32),
                   jax.ShapeDtypeStruct((m, LANES), BF16),
                   jax.ShapeDtypeStruct((IDX_HEADS, m), F32)],
        compiler_params=_cparams(("parallel",)),
        name="prep_t",
    )(z, z, *tabs, qg, kg, ikg, ikb, bd512)
    names = ("qt", "kt", "kb", "vt", "vaug", "iqt", "ikt", "ikdup", "iwt")
    return dict(zip(names, outs))


def _attn_prompt_kernel(qb_tab, kb_tab, nkb_tab, qt_ref, iqt_ref, iwt_ref, ik_ref, k_ref, vaug_ref, o_ref,
                        key_scr, thr_scr, m_scr, acc_scr, iq2_scr, k16_scr, l16_scr, lo_scr, *, tq, tk, topk):
    s = pl.program_id(1)
    qb = qb_tab[s]
    kb = kb_tab[s]
    nkb = nkb_tab[s]
    koff = lax.broadcasted_iota(I32, (tk, tq), 0)
    tpos = qb * tq + lax.broadcasted_iota(I32, (tk, tq), 1)

    @pl.when(kb == 0)
    def _():
        iwt = iwt_ref[...]
        for j in range(IDX_HEADS // 2):
            iq2_scr[j] = jnp.concatenate([iqt_ref[2 * j], iqt_ref[2 * j + 1]], axis=1)

        def score_chunk(c, carry):
            ikc = ik_ref[pl.ds(pl.multiple_of(c * tk, tk), tk), :]
            acc = jnp.zeros((tk, tq), F32)
            for j in range(IDX_HEADS // 2):
                s2 = _dot(ikc, iq2_scr[j])
                acc = acc + jnp.maximum(s2[:, 0:tq], 0.0) * iwt[2 * j:2 * j + 1, :]
                acc = acc + jnp.maximum(s2[:, tq:2 * tq], 0.0) * iwt[2 * j + 1:2 * j + 2, :]
            acc = jnp.where(c * tk + koff <= tpos, acc, -jnp.inf)
            key = _to_key(acc)
            key_scr[c] = key
            k16_scr[c] = (key >> 16).astype(I16)
            return carry

        lax.fori_loop(0, nkb, score_chunk, 0)

        def count16(src_scr, cand, strict=False):
            c16 = cand.astype(I16)

            def body(c, cnt):
                hit = src_scr[c] > c16 if strict else src_scr[c] >= c16
                one = jnp.where(hit, jnp.int16(1), jnp.int16(0))
                for g in range(tk // 16):
                    cnt = cnt + one[g * 16:(g + 1) * 16, :]
                return cnt

            cnt = lax.fori_loop(0, nkb, body, jnp.zeros((16, tq), I16))
            return jnp.sum(cnt.astype(I32), axis=0, keepdims=True)

        def search16(count_fn, opn):
            zero = jnp.zeros((1, tq), I32)
            c0 = count_fn(zero)
            t0 = jnp.where(c0 >= topk, zero, jnp.full((1, tq), -32768, I32))
            opn = jnp.where(c0 == topk, 0, opn)

            def cond(st):
                i, _, _, n_open = st
                return jnp.logical_and(i < 15, n_open > 0)

            def bit_step(st):
                i, t, o, _ = st
                cand = t | jnp.left_shift(jnp.int32(1), 14 - i)
                cnt = count_fn(cand)
                t = jnp.where(cnt >= topk, cand, t)
                o = jnp.where(cnt == topk, 0, o)
                return i + 1, t, o, jnp.sum(o)

            _, t, opn, _ = lax.while_loop(cond, bit_step, (jnp.int32(0), t0, opn, jnp.sum(opn)))
            return t, opn

        n_adm = qb * tq + lax.broadcasted_iota(I32, (1, tq), 1) + 1
        open0 = (n_adm >= topk).astype(I32)
        hi, opn = search16(lambda cand: count16(k16_scr, cand), open0)
        lo_scr[...] = jnp.full((1, tq), -32768, I32)

        @pl.when(jnp.sum(opn) > 0)
        def _():
            def low_chunk(c, carry):
                key = key_scr[c]
                low = (key & 0xFFFF) - 32768
                l16_scr[c] = jnp.where((key >> 16) == hi, low, -32768).astype(I16)
                return carry

            lax.fori_loop(0, nkb, low_chunk, 0)
            above = count16(k16_scr, hi, strict=True)
            lo, _ = search16(lambda cand: above + count16(l16_scr, cand), opn)
            lo_scr[...] = lo

        thr_scr[...] = hi * 65536 + (lo_scr[...] + 32768)
        m_scr[...] = jnp.full(m_scr.shape, NEG, F32)
        acc_scr[...] = jnp.zeros(acc_scr.shape, F32)

    sel = jnp.logical_and(key_scr[kb] >= thr_scr[...], kb * tk + koff <= tpos)
    bias = jnp.where(sel, 0.0, NEG)
    ts = tk // KSPLIT
    for u in range(KSPLIT):
        rows = slice(u * ts, (u + 1) * ts)
        for h in range(A_HEADS):
            p0 = (h // 2) * LANES
            sc = _dot(k_ref[rows, p0:p0 + LANES], qt_ref[h]) + bias[rows, :]
            m_old = m_scr[h:h + 1, :]
            m_new = jnp.maximum(m_old, jnp.max(sc, axis=0, keepdims=True))
            alpha = jnp.exp2(m_old - m_new)
            p = jnp.exp2(sc - m_new).astype(BF16)
            acc_scr[h] = alpha * acc_scr[h] + _dot(vaug_ref[h, :, rows], p)
            m_scr[h:h + 1, :] = m_new

    @pl.when(kb == nkb - 1)
    def _():
        for p in range(A_HEADS // 2):
            a = acc_scr[2 * p]
            b = acc_scr[2 * p + 1]
            ot = jnp.concatenate([a[0:64, :] / a[64:65, :], b[64:128, :] / b[0:1, :]], axis=0)
            o_ref[:, p * LANES:(p + 1) * LANES] = ot.T


def _attn_prompt(pp, nb, s_len):
    tq, tk = 256, 512
    tk = min(tk, s_len)
    topk = min(TOPK_MAX, s_len // 4)
    nqb = s_len // tq
    nkblk = s_len // tk
    qb_l, kb_l, nkb_l = [], [], []
    for qb in range(nqb):
        nkb = ((qb + 1) * tq + tk - 1) // tk
        for kb in range(nkb):
            qb_l.append(qb)
            kb_l.append(kb)
            nkb_l.append(nkb)
    tabs = [jnp.asarray(np.array(a, np.int32)) for a in (qb_l, kb_l, nkb_l)]
    nsteps = len(qb_l)
    m = nb * s_len
    grid_spec = pltpu.PrefetchScalarGridSpec(
        num_scalar_prefetch=3,
        grid=(nb, nsteps),
        in_specs=[
            pl.BlockSpec((A_HEADS, LANES, tq), lambda b, s, qt, kt, nt: (0, 0, b * nqb + qt[s])),
            pl.BlockSpec((IDX_HEADS, LANES, tq), lambda b, s, qt, kt, nt: (0, 0, b * nqb + qt[s])),
            pl.BlockSpec((IDX_HEADS, tq), lambda b, s, qt, kt, nt: (0, b * nqb + qt[s])),
            pl.BlockSpec((s_len, LANES), lambda b, s, qt, kt, nt: (b, 0)),
            pl.BlockSpec((tk, A_WIDTH), lambda b, s, qt, kt, nt: (b * nkblk + kt[s], 0)),
            pl.BlockSpec((A_HEADS, LANES, tk), lambda b, s, qt, kt, nt: (0, 0, b * nkblk + kt[s])),
        ],
        out_specs=pl.BlockSpec((tq, A_WIDTH), lambda b, s, qt, kt, nt: (b * nqb + qt[s], 0)),
        scratch_shapes=[pltpu.VMEM((nkblk, tk, tq), I32),
                        pltpu.VMEM((1, tq), I32),
                        pltpu.VMEM((A_HEADS, tq), F32),
                        pltpu.VMEM((A_HEADS, LANES, tq), F32),
                        pltpu.VMEM((IDX_HEADS // 2, LANES, 2 * tq), BF16),
                        pltpu.VMEM((nkblk, tk, tq), I16),
                        pltpu.VMEM((nkblk, tk, tq), I16),
                        pltpu.VMEM((1, tq), I32)],
    )
    return pl.pallas_call(
        functools.partial(_attn_prompt_kernel, tq=tq, tk=tk, topk=topk),
        grid_spec=grid_spec,
        out_shape=jax.ShapeDtypeStruct((m, A_WIDTH), F32),
        compiler_params=_cparams(("arbitrary", "arbitrary")),
        name="attn_prompt",
    )(*tabs, pp["qt"], pp["iqt"], pp["iwt"], pp["ikdup"], pp["kb"], pp["vaug"])


def _sample_score_kernel(pt_ref, iq_ref, iw_ref, *rest, npg):
    page_refs = rest[:npg]
    o_ref = rest[npg]
    iq = iq_ref[...]
    w = iw_ref[...]
    for g in range(npg):
        pg = page_refs[g][...].astype(BF16)
        sc = _dot(iq, pg)
        o_ref[g:g + 1, :] = jnp.sum(jnp.maximum(sc, 0.0) * w, axis=0, keepdims=True)


def _sample_attn_kernel(pt_ref, sc_ref, iq_ref, iw_ref, ikn_ref, qbd_ref, kn_ref, vn_ref, *rest, npg, topk):
    k_refs = rest[:npg]
    v_refs = rest[npg:2 * npg]
    o_ref = rest[2 * npg]
    key_scr, thr_scr, nk_scr, m_scr, l_scr, acc_scr = rest[2 * npg + 1:]
    j = pl.program_id(1)
    nj = pl.num_programs(1)

    @pl.when(j == 0)
    def _():
        keys = _to_key(sc_ref[...])
        key_scr[...] = keys
        sn = jnp.sum(iq_ref[...].astype(F32) * ikn_ref[...].astype(F32), axis=1, keepdims=True)
        snew = jnp.sum(jnp.maximum(sn, 0.0) * iw_ref[...], axis=0, keepdims=True)
        knew = _to_key(snew)
        nk_scr[...] = knew

        def count_ge(cand):
            c = jnp.sum((keys >= cand).astype(I32), axis=1, keepdims=True)
            return jnp.sum(c, axis=0, keepdims=True) + (knew >= cand).astype(I32)

        thr = jnp.full((1, 1), INT_MIN, I32)
        zero = jnp.zeros((1, 1), I32)
        thr = jnp.where(count_ge(zero) >= topk, zero, thr)

        def bit_step(i, thr):
            cand = thr | jnp.left_shift(jnp.int32(1), 30 - i)
            return jnp.where(count_ge(cand) >= topk, cand, thr)

        thr_scr[...] = lax.fori_loop(0, 31, bit_step, thr)
        m_scr[...] = jnp.full(m_scr.shape, NEG, F32)
        l_scr[...] = jnp.zeros(l_scr.shape, F32)
        acc_scr[...] = jnp.zeros(acc_scr.shape, F32)

    qbd = qbd_ref[...]
    thr = thr_scr[...]
    sel = jnp.concatenate([key_scr[pl.ds(j * npg + g, 1), :] >= thr for g in range(npg)], axis=1)
    sc = jnp.concatenate([_dot(qbd, k_refs[g][...].astype(BF16)) for g in range(npg)], axis=1)
    sc = jnp.where(sel, sc, NEG)
    m_old = m_scr[...]
    m_new = jnp.maximum(m_old, jnp.max(sc, axis=1, keepdims=True))
    alpha = jnp.exp2(m_old - m_new)
    p = jnp.where(sel, jnp.exp2(sc - m_new), 0.0)
    l_scr[...] = alpha * l_scr[...] + jnp.sum(p, axis=1, keepdims=True)
    pv = jnp.zeros(acc_scr.shape, F32)
    for g in range(npg):
        pv = pv + _dot_nt(p[:, g * PAGE_SIZE:(g + 1) * PAGE_SIZE].astype(BF16), v_refs[g][...].astype(BF16))
    acc_scr[...] = alpha * acc_scr[...] + pv
    m_scr[...] = m_new

    @pl.when(j == nj - 1)
    def _():
        sel = nk_scr[...] >= thr
        sc = jnp.sum(qbd.astype(F32) * kn_ref[...].astype(BF16).astype(F32), axis=1, keepdims=True)
        sc = jnp.where(sel, sc, NEG)
        m_old = m_scr[...]
        m_new = jnp.maximum(m_old, sc)
        alpha = jnp.exp2(m_old - m_new)
        p = jnp.where(sel, jnp.exp2(sc - m_new), 0.0)
        l_new = alpha * l_scr[...] + p
        acc = alpha * acc_scr[...] + p.astype(BF16).astype(F32) * vn_ref[...].astype(BF16).astype(F32)
        o = acc / l_new
        hrow = lax.broadcasted_iota(I32, (A_HEADS, A_WIDTH), 0)
        hcol = lax.broadcasted_iota(I32, (A_HEADS, A_WIDTH), 1) // HEAD_DIM
        o_ref[...] = jnp.sum(jnp.where(hrow == hcol, o, 0.0), axis=0, keepdims=True)


def _attn_sample(pp, cache_k, cache_v, cache_ik, page_table, layer):
    nb, npages = page_table.shape
    npg_s = int(np.gcd(npages, 64))
    npg = int(np.gcd(npages, 32))
    topk = min(TOPK_MAX, (npages * PAGE_SIZE + 1) // 4)
    pt = page_table.reshape(-1).astype(I32)
    n_pool = cache_k.shape[1]
    ck = jnp.transpose(cache_k, (0, 1, 3, 4, 2)).reshape(cache_k.shape[0], n_pool, A_WIDTH, PAGE_SIZE)
    cv = jnp.transpose(cache_v, (0, 1, 3, 4, 2)).reshape(cache_v.shape[0], n_pool, A_WIDTH, PAGE_SIZE)
    cache_ik = jnp.transpose(cache_ik, (0, 1, 3, 2))

    iq = pp["iqpad"]
    iq = jnp.transpose(iq[:, :, :64] + iq[:, :, 64:], (1, 0, 2))
    iw = pp["misc"][:, M_IW:M_IW + IDX_HEADS].reshape(nb, IDX_HEADS, 1)
    ikn = pp["ikdup"][:, :IDX_DIM].reshape(nb, 1, IDX_DIM)
    qp = jnp.transpose(pp["qpad"], (1, 0, 2))
    qbd = jnp.zeros((nb, A_HEADS, A_HEADS // 2, LANES), BF16)
    for h in range(A_HEADS):
        qbd = qbd.at[:, h, h // 2, :].set(qp[:, h, :])
    qbd = qbd.reshape(nb, A_HEADS, A_WIDTH)
    kn = pp["kf"].reshape(nb, 1, A_WIDTH)
    vn = pp["vf"].reshape(nb, 1, A_WIDTH)

    def page_spec(width, g, per_step):
        return pl.BlockSpec((None, None, width, PAGE_SIZE),
                            lambda b, j, ptr: (layer, ptr[b * npages + j * per_step + g], 0, 0))

    per_b = lambda *shape: pl.BlockSpec((None,) + shape, lambda b, j, ptr: (b,) + (0,) * len(shape))

    scores = pl.pallas_call(
        functools.partial(_sample_score_kernel, npg=npg_s),
        grid_spec=pltpu.PrefetchScalarGridSpec(
            num_scalar_prefetch=1,
            grid=(nb, npages // npg_s),
            in_specs=[per_b(IDX_HEADS, IDX_DIM), per_b(IDX_HEADS, 1)]
                     + [page_spec(IDX_DIM, g, npg_s) for g in range(npg_s)],
            out_specs=pl.BlockSpec((None, npg_s, PAGE_SIZE), lambda b, j, ptr: (b, j, 0)),
        ),
        out_shape=jax.ShapeDtypeStruct((nb, npages, PAGE_SIZE), F32),
        compiler_params=_cparams(("arbitrary", "arbitrary")),
        name="sample_score",
    )(pt, iq, iw, *([cache_ik] * npg_s))

    out = pl.pallas_call(
        functools.partial(_sample_attn_kernel, npg=npg, topk=topk),
        grid_spec=pltpu.PrefetchScalarGridSpec(
            num_scalar_prefetch=1,
            grid=(nb, npages // npg),
            in_specs=[per_b(npages, PAGE_SIZE), per_b(IDX_HEADS, IDX_DIM), per_b(IDX_HEADS, 1),
                      per_b(1, IDX_DIM), per_b(A_HEADS, A_WIDTH), per_b(1, A_WIDTH), per_b(1, A_WIDTH)]
                     + [page_spec(A_WIDTH, g, npg) for g in range(npg)]
                     + [page_spec(A_WIDTH, g, npg) for g in range(npg)],
            out_specs=pl.BlockSpec((None, 1, A_WIDTH), lambda b, j, ptr: (b, 0, 0)),
            scratch_shapes=[pltpu.VMEM((npages, PAGE_SIZE), I32),
                            pltpu.VMEM((1, 1), I32),
                            pltpu.VMEM((1, 1), I32),
                            pltpu.VMEM((A_HEADS, 1), F32),
                            pltpu.VMEM((A_HEADS, 1), F32),
                            pltpu.VMEM((A_HEADS, A_WIDTH), F32)],
        ),
        out_shape=jax.ShapeDtypeStruct((nb, 1, A_WIDTH), F32),
        compiler_params=_cparams(("arbitrary", "arbitrary")),
        name="sample_attn",
    )(pt, scores, iq, iw, ikn, qbd, kn, vn, *([ck] * npg), *([cv] * npg))
    return out.reshape(nb, A_WIDTH)


NH = 4
SEQ_PER_STEP = 4
HW = NH * 64


def _bd(y):
    hb = lax.broadcasted_iota(I32, y.shape, 1) // 64
    zero = jnp.zeros_like(y)
    return jnp.concatenate([jnp.where(hb == h, y, zero) for h in range(NH)], axis=0)


def _st1(x, y, form="nn"):
    return _bdot(x.astype(BF16), _bd(y.astype(BF16)), form)


def _st3(x, y):
    xh, xl = _split2(x)
    yh, yl = _split2(y)
    bh, bl = _bd(yh), _bd(yl)
    return _bdot(xh, bh, "nn") + (_bdot(xh, bl, "nn") + _bdot(xl, bh, "nn"))


class _Lock:
    def __init__(self, vals):
        self.v = list(vals)

    def _bin(self, other, f):
        ov = other.v if isinstance(other, _Lock) else [other] * len(self.v)
        return _Lock([f(a, b) for a, b in zip(self.v, ov)])

    def __add__(self, o):
        return self._bin(o, lambda a, b: a + b)

    def __radd__(self, o):
        return self._bin(o, lambda a, b: b + a)

    def __sub__(self, o):
        return self._bin(o, lambda a, b: a - b)

    def __rsub__(self, o):
        return self._bin(o, lambda a, b: b - a)

    def __mul__(self, o):
        return self._bin(o, lambda a, b: a * b)

    def __rmul__(self, o):
        return self._bin(o, lambda a, b: b * a)

    def __neg__(self):
        return _Lock([-a for a in self.v])

    def __getitem__(self, idx):
        return _Lock([a[idx] for a in self.v])


def _lift(f):
    def g(*args, **kw):
        n = max(len(a.v) for a in args if isinstance(a, _Lock))
        pick = lambda a, i: a.v[i] if isinstance(a, _Lock) else a
        return _Lock([f(*[pick(a, i) for a in args], **kw) for i in range(n)])
    return g


def _st_unit_lower_inverse(mm, r, q):
    st3, where = _lift(_st3), _lift(jnp.where)
    pw = where(r // 16 == q // 16, -mm, 0.0)
    inv = pw + (r == q).astype(F32)
    k = 1
    while 2 * k < 16:
        pw = st3(pw, pw)
        inv = inv + st3(inv, pw)
        k *= 2
    for sz in (32, 64):
        half = sz // 2
        e = where(jnp.logical_and(r // sz == q // sz, r // half != q // half), mm, 0.0)
        inv = inv - st3(inv, st3(e, inv))
    return inv


def _delta_st_kernel(xc_ref, bg_ref, ms_ref, buf_ref, s0_ref, cw_ref, al_ref, dt_ref, dn_ref, bd512_ref,
                     o_ref, sout_ref, nbuf_ref, xp_scr, s_scr, *, c, t_valid, nt):
    t = pl.program_id(1)
    ns = SEQ_PER_STEP
    seqs = range(ns)
    where, exp = _lift(jnp.where), _lift(jnp.exp)
    st1, st3, mm1, mm3, msel = _lift(_st1), _lift(_st3), _lift(_mm1), _lift(_mm3), _lift(_mm_sel)

    @pl.when(t == 0)
    def _():
        xp_scr[:, 5:8, :] = buf_ref[...]
        s_scr[...] = s0_ref[...]

    xp_scr[:, 8:8 + c, :] = xc_ref[...]
    cw = cw_ref[...]
    y = _Lock([cw[0:1] * xp_scr[j, 5:5 + c, :] + cw[1:2] * xp_scr[j, 6:6 + c, :]
               + cw[2:3] * xp_scr[j, 7:7 + c, :] + cw[3:4] * xp_scr[j, 8:8 + c, :] for j in seqs])

    tb, lt = (t_valid - 1) // c, (t_valid - 1) % c

    @pl.when(t == tb)
    def _():
        nbuf_ref[...] = xp_scr[:, 8 + lt - 2:8 + lt + 1, :]

    hist = xp_scr[:, c + 5:c + 8, :]
    xp_scr[:, 5:8, :] = hist

    y = _lift(_silu)(y)
    ss = _lift(_group_sum)(y[:, 0:512] * y[:, 0:512], bd512_ref[...])
    inv_n = _lift(lax.rsqrt)(ss + EPS)
    qn = y[:, 0:HW] * inv_n[:, 0:HW] * (B_DK ** -0.5)
    kn = y[:, HW:2 * HW] * inv_n[:, HW:2 * HW]
    vv = y[:, 2 * HW:3 * HW]
    ms = _Lock([ms_ref[j] for j in seqs])
    g_all = _lift(_softplus)(ms + dt_ref[...]) * (-jnp.exp(al_ref[...]))
    beta_all = _lift(jax.nn.sigmoid)(ms)
    if t_valid % c != 0:
        vm = (t * c + lax.broadcasted_iota(I32, (c, 1), 0) < t_valid).astype(F32)
        g_all = g_all * vm
        beta_all = beta_all * vm
        kn = kn * vm
        vv = vv * vm

    r = lax.broadcasted_iota(I32, (c, HW), 0)
    lane = lax.broadcasted_iota(I32, (c, HW), 1)
    q = lane % 64
    hb = lane // 64
    incl = r >= q
    strict = r > q
    r64 = lax.broadcasted_iota(I32, (c, c), 0)
    q64 = lax.broadcasted_iota(I32, (c, c), 1)
    gcum_all = msel(r64 >= q64, g_all)
    lane128 = lax.broadcasted_iota(I32, (c, LANES), 1)
    gc = _Lock([jnp.zeros((c, HW), F32)] * ns)
    bt = _Lock([jnp.zeros((c, HW), F32)] * ns)
    for h in range(NH):
        gc = where(hb == h, gcum_all[:, M_BA + h:M_BA + h + 1], gc)
        bt = where(hb == h, beta_all[:, M_BB + h:M_BB + h + 1], bt)
    gsel = _lift(lambda g: jnp.concatenate([jnp.where(lane128 == M_BA + h, g, 0.0) for h in range(NH)],
                                           axis=0))(gcum_all)
    grow = msel(jnp.ones((c, LANES), BF16), gsel, "nt")
    dec = exp(where(incl, gc - grow, 0.0))
    knb = _lift(lambda a: a.astype(BF16))(kn)
    kbd = _lift(_bd)(knb)
    kk = _lift(_bdot)(knb, kbd, "nt")
    qk = _lift(_bdot)(_lift(lambda a: a.astype(BF16))(qn), kbd, "nt")
    mm = where(strict, bt * kk * dec, 0.0)
    inv = _st_unit_lower_inverse(mm, r, q)
    egc = exp(gc)
    uu = st3(inv, bt * vv)
    ww = st3(inv, (bt * egc) * kn)
    s_old = _Lock([s_scr[j] for j in seqs])
    delta = uu - mm3(ww, s_old)
    o = mm1(qn * egc, s_old) + st1(where(incl, qk * dec, 0.0), delta)
    glast = gc[c - 1:c, :]
    kd = kn * exp(glast - gc)
    rb = lax.broadcasted_iota(I32, (HW, HW), 0) // 64
    cb = lax.broadcasted_iota(I32, (HW, HW), 1) // 64
    gl = _Lock([jnp.zeros((HW, HW), F32)] * ns)
    for h in range(NH):
        gl = where(rb == h, exp(gcum_all[c - 1:c, M_BA + h:M_BA + h + 1]), gl)
    s_new = gl * s_old + where(rb == cb, mm3(kd, delta, "tn"), 0.0)
    ms_o = _lift(_group_sum)(o * o, bd512_ref[0:HW, 0:HW]) * (1.0 / 64)
    out = o * _lift(lax.rsqrt)(ms_o + EPS) * dn_ref[...] * _lift(_silu)(_Lock([bg_ref[j] for j in seqs]))
    for j in seqs:
        s_scr[j] = s_new.v[j]
        o_ref[j] = out.v[j]

    @pl.when(t == nt - 1)
    def _():
        sout_ref[...] = s_scr[...]


def _delta_st(z, nb, t_pad, t_valid, buf0, s0_bd, cw, al_row, dt_row, dn_row, bd512):
    c = CHUNK
    ns = SEQ_PER_STEP
    assert c == 64 and nb % ns == 0
    nt = t_pad // c
    z3 = z.reshape(nb, t_pad, NZ)
    const = lambda *shape: pl.BlockSpec(shape, lambda b, t: (0,) * len(shape))
    tok = lambda w, col: pl.BlockSpec((ns, c, w), lambda b, t: (b, t, col))
    per_seq = lambda *shape: pl.BlockSpec((ns,) + shape, lambda b, t: (b,) + (0,) * len(shape))
    o, s_out, nbuf = pl.pallas_call(
        functools.partial(_delta_st_kernel, c=c, t_valid=t_valid, nt=nt),
        grid=(nb // ns, nt),
        in_specs=[tok(B_CONV, Z_CONV // B_CONV), tok(HW, Z_BG // HW), tok(LANES, Z_MISC // LANES),
                  per_seq(CONV_W - 1, B_CONV), per_seq(HW, HW),
                  const(CONV_W, B_CONV), const(1, LANES), const(1, LANES), const(1, HW),
                  const(512, 512)],
        out_specs=[tok(HW, 0), per_seq(HW, HW), per_seq(CONV_W - 1, B_CONV)],
        out_shape=[jax.ShapeDtypeStruct((nb, t_pad, HW), F32),
                   jax.ShapeDtypeStruct((nb, HW, HW), F32),
                   jax.ShapeDtypeStruct((nb, CONV_W - 1, B_CONV), F32)],
        scratch_shapes=[pltpu.VMEM((ns, 8 + c, B_CONV), F32), pltpu.VMEM((ns, HW, HW), F32)],
        compiler_params=_cparams(("arbitrary", "arbitrary")),
        name="delta",
    )(z3, z3, z3, buf0, s0_bd, cw, al_row, dt_row, dn_row, bd512)
    return o.reshape(nb * t_pad, HW), s_out, nbuf


def _hgrn_st_kernel(x_ref, s0_ref, lb_ref, cn_ref, bd512_ref, o_ref, sout_ref, s_scr, *, c, t_valid, nt):
    t = pl.program_id(1)
    seqs = range(SEQ_PER_STEP)
    where, exp, minimum = _lift(jnp.where), _lift(jnp.exp), _lift(jnp.minimum)
    st1, mm1, mm3, msel, gsum = _lift(_st1), _lift(_mm1), _lift(_mm3), _lift(_mm_sel), _lift(_group_sum)

    @pl.when(t == 0)
    def _():
        s_scr[...] = s0_ref[...]

    lb = lb_ref[...]
    cf = _Lock([x_ref[j, :, HW:2 * HW] for j in seqs])
    logf = _lift(jnp.log)(_lift(jax.nn.sigmoid)(cf) * (1.0 - lb) + lb)
    kk = _lift(jax.nn.sigmoid)(-cf) * (1.0 - lb)
    qq = _lift(_silu)(_Lock([x_ref[j, :, 0:HW] for j in seqs]))
    vv = _Lock([x_ref[j, :, 2 * HW:3 * HW] for j in seqs])
    if t_valid % c != 0:
        vm = (t * c + lax.broadcasted_iota(I32, (c, 1), 0) < t_valid).astype(F32)
        logf = logf * vm
        kk = kk * vm

    r64 = lax.broadcasted_iota(I32, (c, c), 0)
    q64 = lax.broadcasted_iota(I32, (c, c), 1)
    b = msel(r64 >= q64, logf)
    r = lax.broadcasted_iota(I32, (c, HW), 0)
    q = lax.broadcasted_iota(I32, (c, HW), 1) % 64

    amat = _Lock([jnp.zeros((c, HW), F32)] * SEQ_PER_STEP)
    half = c // 2
    while half >= 1:
        sz = 2 * half
        bnd = (r64 // sz) * sz + half - 1
        rl = msel(q64 == bnd, b)
        qf = qq * exp(minimum(b - rl, 0.0))
        kf = kk * exp(minimum(rl - b, 0.0))
        lmask = jnp.logical_and(r // sz == q // sz, jnp.logical_and(r % sz >= half, q % sz < half))
        amat = amat + where(lmask, st1(qf, kf, "nt"), 0.0)
        half //= 2

    st_old = _Lock([s_scr[j] for j in seqs])
    o = mm1(qq * exp(b), st_old, "nt") + st1(amat, vv)
    o = o + gsum(qq * kk, bd512_ref[0:HW, 0:HW]) * vv
    blast = b[c - 1:c, :]
    kd = kk * exp(blast - b)
    rb = lax.broadcasted_iota(I32, (HW, HW), 0) // 64
    cb = lax.broadcasted_iota(I32, (HW, HW), 1) // 64
    st_new = st_old * exp(blast) + where(rb == cb, mm3(vv, kd, "tn"), 0.0)
    ms_o = gsum(o * o, bd512_ref[0:HW, 0:HW]) * (1.0 / 64)
    out = o * _lift(lax.rsqrt)(ms_o + EPS) * cn_ref[...]
    for j in seqs:
        s_scr[j] = st_new.v[j]
        o_ref[j] = out.v[j]

    @pl.when(t == nt - 1)
    def _():
        sout_ref[...] = s_scr[...]


def _hgrn_st(z, nb, t_pad, t_valid, s0_bd, lb_row, cn_row, bd512):
    c = CHUNK
    ns = SEQ_PER_STEP
    assert c == 64 and nb % ns == 0
    nt = t_pad // c
    z3 = z.reshape(nb, t_pad, NZ)
    const = lambda *shape: pl.BlockSpec(shape, lambda b, t: (0,) * len(shape))
    per_seq = lambda *shape: pl.BlockSpec((ns,) + shape, lambda b, t: (b,) + (0,) * len(shape))
    o, s_out = pl.pallas_call(
        functools.partial(_hgrn_st_kernel, c=c, t_valid=t_valid, nt=nt),
        grid=(nb // ns, nt),
        in_specs=[pl.BlockSpec((ns, c, 768), lambda b, t: (b, t, Z_HG // 768)),
                  per_seq(HW, HW),
                  const(1, HW), const(1, HW), const(512, 512)],
        out_specs=[pl.BlockSpec((ns, c, HW), lambda b, t: (b, t, 0)), per_seq(HW, HW)],
        out_shape=[jax.ShapeDtypeStruct((nb, t_pad, HW), F32),
                   jax.ShapeDtypeStruct((nb, HW, HW), F32)],
        scratch_shapes=[pltpu.VMEM((ns, HW, HW), F32)],
        compiler_params=_cparams(("arbitrary", "arbitrary")),
        name="hgrn",
    )(z3, s0_bd, lb_row, cn_row, bd512)
    return o.reshape(nb * t_pad, HW), s_out


def _out_mlp_kernel(x_ref, oa_ref, ob_ref, oc_ref, wo_ref, g2_ref, w1_ref, w2_ref, y_ref, h_scr):
    @pl.when(pl.program_id(1) == 0)
    def _():
        x1 = (x_ref[...]
              + _dot(oa_ref[...].astype(BF16), wo_ref[0:A_WIDTH, :])
              + _dot(ob_ref[...].astype(BF16), wo_ref[A_WIDTH:A_WIDTH + 256, :])
              + _dot(oc_ref[...].astype(BF16), wo_ref[A_WIDTH + 256:A_WIDTH + 512, :]))
        y_ref[...] = x1
        ms = jnp.mean(x1 * x1, axis=-1, keepdims=True)
        h_scr[...] = (x1 * lax.rsqrt(ms + EPS) * g2_ref[...]).astype(BF16)

    u = jnp.maximum(_dot(h_scr[...], w1_ref[...]), 0.0)
    y_ref[...] += _dot((u * u).astype(BF16), w2_ref[...])


def _out_mlp(x, oa, ob, oc, wo, g2, w1, w2):
    m, d = x.shape
    tm = min(512, m)
    tf = 512
    nf = w1.shape[1] // tf
    tok = lambda w: pl.BlockSpec((tm, w), lambda i, j: (i, 0))
    return pl.pallas_call(
        _out_mlp_kernel,
        grid=(m // tm, nf),
        in_specs=[tok(d), tok(A_WIDTH), tok(256), tok(256),
                  pl.BlockSpec((d, d), lambda i, j: (0, 0)),
                  pl.BlockSpec((1, d), lambda i, j: (0, 0)),
                  pl.BlockSpec((d, tf), lambda i, j: (0, j)),
                  pl.BlockSpec((tf, d), lambda i, j: (j, 0))],
        out_specs=tok(d),
        out_shape=jax.ShapeDtypeStruct((m, d), F32),
        scratch_shapes=[pltpu.VMEM((tm, d), BF16)],
        compiler_params=_cparams(("parallel", "arbitrary")),
        name="out_mlp",
    )(x, oa, ob, oc, wo, g2, w1, w2)


def _relayout_w_in(w):
    d = w.shape[0]
    off = np.cumsum([0, 512, 512, 512, 1024, 16, 64, 256, 256, 256, 4, 4, 256, 256, 256, 256])
    (aq, ak, av, iq, iw, ik, bq, bk, bv, ba, bb, bg, cq, cf, ci, end) = [int(o) for o in off]
    zpad = lambda n: jnp.zeros((d, n), w.dtype)
    cols = [w[:, aq:iw],
            w[:, bg:cq],
            w[:, ik:bq], w[:, iw:ik], w[:, ba:bb], w[:, bb:bg], zpad(LANES - 88),
            zpad(Z_CONV - Z_MISC - LANES),
            w[:, bq:ba],
            w[:, cq:end]]
    out = jnp.concatenate(cols, axis=1)
    assert out.shape[1] == NZ
    return out.astype(BF16)


def _rope_tables(pos):
    pos = pos.astype(F32)
    half = HEAD_DIM // 2
    inv = ROPE_THETA ** (-jnp.arange(half, dtype=F32) / half)
    ang = pos[:, None] * inv[None, :]
    cos, sin = jnp.cos(ang), jnp.sin(ang)
    ca = jnp.tile(jnp.concatenate([cos, cos], -1), (1, 2))
    sa = jnp.tile(jnp.concatenate([-sin, sin], -1), (1, 2))
    half_i = IDX_ROPE // 2
    inv_i = ROPE_THETA ** (-jnp.arange(half_i, dtype=F32) / half_i)
    ang_i = pos[:, None] * inv_i[None, :]
    cos_i, sin_i = jnp.cos(ang_i), jnp.sin(ang_i)
    rest = IDX_DIM - IDX_ROPE
    one = jnp.ones((pos.shape[0], rest), F32)
    zero = jnp.zeros((pos.shape[0], rest), F32)
    ci = jnp.tile(jnp.concatenate([cos_i, cos_i, one], -1), (1, 2))
    si = jnp.tile(jnp.concatenate([-sin_i, sin_i, zero], -1), (1, 2))
    return ca, sa, ci, si


def _to_bd4(s):
    z = jnp.zeros_like(s[:, 0])
    rows = [jnp.concatenate([s[:, h] if g == h else z for g in range(NH)], axis=-1) for h in range(NH)]
    return jnp.concatenate(rows, axis=-2)


def _from_bd4(p):
    return jnp.stack([p[:, 64 * h:64 * (h + 1), 64 * h:64 * (h + 1)] for h in range(NH)], axis=1)


def _lane_row(vals, offset):
    return jnp.zeros((1, LANES), F32).at[0, offset:offset + vals.shape[0]].set(vals.astype(F32))


def _block_diag_ones(n):
    i = np.arange(n) // 64
    return jnp.asarray((i[:, None] == i[None, :]).astype(np.float32)).astype(BF16)


def kernel(x_prompt, x_sample, cache_k, cache_v, cache_idx_k, state_delta, state_delta_conv, state_hgrn,
           page_table, norm1_g, w_in, q_norm_g, k_norm_g, idx_k_ln_g, idx_k_ln_b, conv_w, a_log, dt_bias,
           delta_norm_g, lb_param, hgrn_norm_g, w_out, norm2_g, w_mlp_in, w_mlp_out):
    depth = w_in.shape[0]
    bp, s_len, d = x_prompt.shape
    bd_, t_dec, _ = x_sample.shape
    assert t_dec == 1 and d == D_MODEL
    past = page_table.shape[1] * PAGE_SIZE

    lb_all = jnp.cumsum(jax.nn.softmax(lb_param.astype(F32), axis=0), axis=0)
    lb_all = lb_all - lb_all[0:1]
    tabs_p = _rope_tables(jnp.arange(s_len))
    tabs_s = _rope_tables(past + jnp.arange(t_dec))
    bd512 = _block_diag_ones(512)

    xp = x_prompt.reshape(bp * s_len, d)
    xs = x_sample.reshape(bd_ * t_dec, d)
    t_pad_s = CHUNK
    zeros_buf = jnp.zeros((bp, CONV_W - 1, B_CONV), F32)
    zeros_state = jnp.zeros((bp, HW, HW), F32)

    outs_p = {k: [] for k in ("k", "v", "ik", "ds", "dc", "hs")}
    outs_s = {k: [] for k in ("k", "v", "ik", "ds", "dc", "hs")}
    for l in range(depth):
        w_in_l = _relayout_w_in(w_in[l])
        g1 = norm1_g[l].reshape(1, d)
        qg = jnp.tile(q_norm_g[l], A_HEADS).reshape(1, A_WIDTH)
        kg = jnp.tile(k_norm_g[l], A_HEADS).reshape(1, A_WIDTH)
        ikg = _lane_row(idx_k_ln_g[l], 0)
        ikb = _lane_row(idx_k_ln_b[l], 0)
        al_row = _lane_row(a_log[l], M_BA)
        dt_row = _lane_row(dt_bias[l], M_BA)
        dn_row = jnp.tile(delta_norm_g[l], NH).reshape(1, HW).astype(F32)
        cn_row = jnp.tile(hgrn_norm_g[l], NH).reshape(1, HW).astype(F32)
        lb_row = lb_all[l].reshape(1, 256)
        wo = w_out[l].astype(BF16)
        g2 = norm2_g[l].reshape(1, d)
        w1 = w_mlp_in[l].astype(BF16)
        w2 = w_mlp_out[l].astype(BF16)
        cw = conv_w[l]

        z = _norm_matmul(xp, g1, w_in_l, 768)
        pp = _prep_t(z, tabs_p, bp, s_len, qg, kg, ikg, ikb, bd512)
        oa = _attn_prompt(pp, bp, s_len)
        ob, ds, dc = _delta_st(z, bp, s_len, s_len, zeros_buf, zeros_state, cw, al_row, dt_row, dn_row, bd512)
        oc, hs = _hgrn_st(z, bp, s_len, s_len, zeros_state, lb_row, cn_row, bd512)
        xp = _out_mlp(xp, oa, ob, oc, wo, g2, w1, w2)
        heads_last = lambda a: jnp.transpose(a.reshape(bp, A_HEADS, HEAD_DIM, s_len), (0, 3, 1, 2))
        outs_p["k"].append(heads_last(pp["kt"]))
        outs_p["v"].append(heads_last(pp["vt"]))
        outs_p["ik"].append(jnp.transpose(pp["ikt"], (0, 2, 1)))
        outs_p["ds"].append(_from_bd4(ds))
        outs_p["dc"].append(dc)
        outs_p["hs"].append(jnp.swapaxes(_from_bd4(hs), -1, -2))

        zs = _norm_matmul(xs, g1, w_in_l, 768)
        ps = _prep(zs, tabs_s, qg, kg, ikg, ikb, bd512)
        oa_s = _attn_sample(ps, cache_k, cache_v, cache_idx_k, page_table, l)
        zs_pad = jnp.pad(zs.reshape(bd_, 1, NZ), ((0, 0), (0, t_pad_s - 1), (0, 0))).reshape(bd_ * t_pad_s, NZ)
        ob_s, ds_s, dc_s = _delta_st(zs_pad, bd_, t_pad_s, 1, state_delta_conv[l].astype(F32),
                                     _to_bd4(state_delta[l].astype(F32)), cw, al_row, dt_row, dn_row, bd512)
        oc_s, hs_s = _hgrn_st(zs_pad, bd_, t_pad_s, 1, _to_bd4(jnp.swapaxes(state_hgrn[l].astype(F32), -1, -2)),
                              lb_row, cn_row, bd512)
        ob_s = ob_s.reshape(bd_, t_pad_s, 256)[:, 0]
        oc_s = oc_s.reshape(bd_, t_pad_s, 256)[:, 0]
        xs = _out_mlp(xs, oa_s, ob_s, oc_s, wo, g2, w1, w2)
        outs_s["k"].append(ps["kf"].reshape(bd_, t_dec, A_HEADS, HEAD_DIM))
        outs_s["v"].append(ps["vf"].reshape(bd_, t_dec, A_HEADS, HEAD_DIM))
        outs_s["ik"].append(ps["ikf"].reshape(bd_, t_dec, IDX_DIM))
        outs_s["ds"].append(_from_bd4(ds_s))
        outs_s["dc"].append(dc_s)
        outs_s["hs"].append(jnp.swapaxes(_from_bd4(hs_s), -1, -2))

    st = lambda o, k: jnp.stack(o[k])
    return (xp.reshape(bp, s_len, d), xs.reshape(bd_, t_dec, d),
            st(outs_p, "k"), st(outs_p, "v"), st(outs_p, "ik"), st(outs_p, "ds"), st(outs_p, "dc"), st(outs_p, "hs"),
            st(outs_s, "k"), st(outs_s, "v"), st(outs_s, "ik"), st(outs_s, "ds"), st(outs_s, "dc"), st(outs_s, "hs"))
```
